```python
import jax, jax.numpy as jnp
from jax import lax
import numpy as np

D_MODEL = 1024
BATCH = 8
SEQ = 2048
DEPTH = 4

GRID_W = 64
CTX_LEN = 256
HEAD_DIM = 64
EPS = 1e-6
ROPE_BASE = 10000.0
MIX_WIDTH = 1024

A_HEADS = 8
A_KV_HEADS = 2
A_GROUP = A_HEADS // A_KV_HEADS
A_WINDOW = 128
A_BLOCK = 128

POOL_WINDOWS = (2, 4, 8, 16)
POOL_GROUP = 128
POOL_WIDTH = POOL_GROUP * len(POOL_WINDOWS)

RWKV_HEADS = 8
RWKV_HEAD = 64
RWKV_WIDTH = RWKV_HEADS * RWKV_HEAD
DECAY_LORA = 64
AAA_LORA = 64
GATE_LORA = 128
LNX_EPS = 64e-5
RWKV_IN = 3 * RWKV_WIDTH + 2 * DECAY_LORA + 2 * AAA_LORA + GATE_LORA

MLA_HEADS = 8
MLA_Q_RANK = 256
MLA_KV_RANK = 128
MLA_NOPE = 64
MLA_ROPE = 32
MLA_V = 64
MLA_BLOCK = 128
MLA_IN = MLA_Q_RANK + MLA_KV_RANK + MLA_ROPE

EVEN_IN = A_HEADS * HEAD_DIM + 2 * A_KV_HEADS * HEAD_DIM + POOL_WIDTH
ODD_IN = RWKV_IN + MLA_IN

N_EXPERTS = 16
EXPERT_FF = 2048
EC_FACTOR = 2

kernel_name = "hybrid_diffusion_trunk_swa_pool_rwkv7_mla_ecmoe"


def split_cols(u, sizes):
    offsets = [int(o) for o in np.cumsum(sizes)[:-1]]
    return jnp.split(u, offsets, axis=-1)


def rmsnorm(x, g):
    xf = x.astype(jnp.float32)
    y = xf * lax.rsqrt(jnp.mean(xf * xf, axis=-1, keepdims=True) + EPS)
    return (y * g.astype(jnp.float32)).astype(x.dtype)


def modulate(x, g, shift, scale):
    return rmsnorm(x, g) * (1 + scale) + shift


def grid_positions(n_tokens):
    rows = n_tokens // GRID_W
    row = jnp.repeat(jnp.arange(rows, dtype=jnp.int32), GRID_W)
    col = jnp.tile(jnp.arange(GRID_W, dtype=jnp.int32), rows)
    return row, col


def rope_1d(x, pos):
    half = x.shape[-1] // 2
    inv = ROPE_BASE ** (-jnp.arange(half, dtype=jnp.float32) / half)
    ang = pos.astype(jnp.float32)[:, None] * inv[None, :]
    cos = jnp.cos(ang)[:, None, :].astype(x.dtype)
    sin = jnp.sin(ang)[:, None, :].astype(x.dtype)
    x1, x2 = x[..., :half], x[..., half:]
    return jnp.concatenate([x1 * cos - x2 * sin, x2 * cos + x1 * sin], axis=-1)


def rope_2d(x, row, col):
    d = x.shape[-1] // 2
    return jnp.concatenate([rope_1d(x[..., :d], row), rope_1d(x[..., d:], col)], axis=-1)


def softmax_with_sink(s, sink):
    s_all = jnp.concatenate([s, jnp.broadcast_to(sink, s.shape[:-1] + (1,))], axis=-1)
    return jax.nn.softmax(s_all, axis=-1)[..., :-1]


def window_gqa_latent(q, k, v, kc, vc, sink):
    B, T = q.shape[:2]
    nb = T // A_BLOCK
    Lc = kc.shape[1]
    qb = q.reshape(B, nb, A_BLOCK, A_KV_HEADS, A_GROUP, HEAD_DIM)
    pad = ((0, 0), (A_BLOCK, A_BLOCK), (0, 0), (0, 0))
    kp = jnp.pad(k, pad).reshape(B, nb + 2, A_BLOCK, A_KV_HEADS, HEAD_DIM)
    vp = jnp.pad(v, pad).reshape(B, nb + 2, A_BLOCK, A_KV_HEADS, HEAD_DIM)
    kw = jnp.concatenate([kp[:, :-2], kp[:, 1:-1], kp[:, 2:]], axis=2)
    vw = jnp.concatenate([vp[:, :-2], vp[:, 1:-1], vp[:, 2:]], axis=2)
    qi = jnp.arange(A_BLOCK)[:, None]
    kj = jnp.arange(3 * A_BLOCK)[None, :]
    rel = kj - A_BLOCK - qi
    kpos = jnp.arange(nb)[:, None, None] * A_BLOCK - A_BLOCK + kj[None]
    valid = (jnp.abs(rel)[None] <= A_WINDOW) & (kpos >= 0) & (kpos < T)
    s_loc = jnp.einsum("bnqhgd,bnkhd->bnhgqk", qb, kw).astype(jnp.float32)
    s_loc = jnp.where(valid[None, :, None, None], s_loc, -jnp.inf)
    s_ctx = jnp.einsum("bnqhgd,bchd->bnhgqc", qb, kc).astype(jnp.float32)
    p = softmax_with_sink(jnp.concatenate([s_ctx, s_loc], axis=-1),
                          sink.astype(jnp.float32)[None, None, :, :, None, None]).astype(v.dtype)
    o = (jnp.einsum("bnhgqc,bchd->bnqhgd", p[..., :Lc], vc)
         + jnp.einsum("bnhgqk,bnkhd->bnqhgd", p[..., Lc:], vw))
    return o.reshape(B, T, A_HEADS * HEAD_DIM)


def context_gqa(qc, kc, vc, sink):
    s = jnp.einsum("bqhgd,bkhd->bhgqk", qc, kc).astype(jnp.float32)
    p = softmax_with_sink(s, sink.astype(jnp.float32)[None, :, :, None, None]).astype(vc.dtype)
    o = jnp.einsum("bhgqk,bkhd->bqhgd", p, vc)
    return o.reshape(o.shape[0], o.shape[1], A_HEADS * HEAD_DIM)


def multiscale_pool(u, pool_w, pool_scale):
    B, T, _ = u.shape
    uf = u.astype(jnp.float32)
    cs = jnp.pad(jnp.cumsum(uf, axis=1), ((0, 0), (1, 0), (0, 0)))
    t = jnp.arange(T)
    outs = []
    for gi, w in enumerate(POOL_WINDOWS):
        lo_c, hi_c = gi * POOL_GROUP, (gi + 1) * POOL_GROUP
        lo = jnp.clip(t - w // 2, 0, T)
        hi = jnp.clip(t + w // 2, 0, T)
        csg = cs[..., lo_c:hi_c]
        mean = (jnp.take(csg, hi, axis=1) - jnp.take(csg, lo, axis=1)) / (hi - lo).astype(jnp.float32)[None, :, None]
        outs.append((mean - uf[..., lo_c:hi_c]).astype(u.dtype) @ pool_w[gi])
    return jnp.concatenate(outs, axis=-1) * pool_scale


def attn_pool_mixer(hl, hc, w_in, w_out, sink, pool_w, pool_scale, row, col, need_ctx):
    B, T, _ = hl.shape
    Lc = hc.shape[1]
    sizes = (A_HEADS * HEAD_DIM, A_KV_HEADS * HEAD_DIM, A_KV_HEADS * HEAD_DIM, POOL_WIDTH)
    ql, kl, vl, pl = split_cols(hl @ w_in, sizes)
    qc, kc, vc, pc = split_cols(hc @ w_in, sizes)
    scale = HEAD_DIM ** -0.5
    sink = sink.reshape(A_KV_HEADS, A_GROUP)
    ql = rope_2d(ql.reshape(B, T, A_HEADS, HEAD_DIM), row, col) * scale
    kl = rope_2d(kl.reshape(B, T, A_KV_HEADS, HEAD_DIM), row, col)
    vl = vl.reshape(B, T, A_KV_HEADS, HEAD_DIM)
    kc = kc.reshape(B, Lc, A_KV_HEADS, HEAD_DIM)
    vc = vc.reshape(B, Lc, A_KV_HEADS, HEAD_DIM)
    al = window_gqa_latent(ql.reshape(B, T, A_KV_HEADS, A_GROUP, HEAD_DIM), kl, vl, kc, vc, sink)
    yl = jnp.concatenate([al, multiscale_pool(pl, pool_w, pool_scale)], axis=-1) @ w_out
    if not need_ctx:
        return yl, None
    qc = qc.reshape(B, Lc, A_KV_HEADS, A_GROUP, HEAD_DIM) * scale
    ac = context_gqa(qc, kc, vc, sink)
    yc = jnp.concatenate([ac, multiscale_pool(pc, pool_w, pool_scale)], axis=-1) @ w_out
    return yl, yc


def token_shift(u, mu_prev, mu_next):
    prev = jnp.pad(u, ((0, 0), (1, 0), (0, 0)))[:, :-1]
    nxt = jnp.pad(u, ((0, 0), (0, 1), (0, 0)))[:, 1:]
    return u + mu_prev * (prev - u) + mu_next * (nxt - u)


def rwkv_prep(u, mu_prev, mu_next, w0, w2, a0, a2, g2, k_k, k_a):
    B, T, _ = u.shape
    u = token_shift(u, mu_prev, mu_next)
    r, k, v, wl_f, wl_b, al_f, al_b, gl = split_cols(
        u, (RWKV_WIDTH,) * 3 + (DECAY_LORA,) * 2 + (AAA_LORA,) * 2 + (GATE_LORA,))
    heads = lambda t: t.astype(jnp.float32).reshape(B, T, RWKV_HEADS, RWKV_HEAD)
    kk = heads(k * k_k)
    kk = kk * lax.rsqrt(jnp.maximum(jnp.sum(kk * kk, axis=-1, keepdims=True), 1e-24))
    g = jax.nn.sigmoid(gl) @ g2
    dirs = []
    for d, wl, al in ((0, wl_f, al_f), (1, wl_b, al_b)):
        wlog = (-jax.nn.softplus(-(w0[d] + jnp.tanh(wl) @ w2[d])) - 0.5).astype(jnp.float32)
        a = jax.nn.sigmoid(a0[d] + al @ a2[d])
        dirs.append((heads(jnp.exp(-jnp.exp(wlog))), heads(a), heads(k * (1 + (a - 1) * k_a))))
    return heads(r), heads(v), kk, g, dirs


def wkv7_scan(r, w, k, v, kk, a, s0, reverse):
    def step(state, inp):
        r_t, w_t, k_t, v_t, kk_t, a_t = inp
        sa = jnp.einsum("bhvk,bhk->bhv", state, -kk_t)
        state = (state * w_t[:, :, None, :] + sa[..., None] * (kk_t * a_t)[:, :, None, :]
                 + v_t[..., None] * k_t[:, :, None, :])
        return state, jnp.einsum("bhvk,bhk->bhv", state, r_t)
    xs = tuple(jnp.moveaxis(t, 1, 0) for t in (r, w, k, v, kk, a))
    state, out = lax.scan(step, s0, xs, reverse=reverse)
    return jnp.moveaxis(out, 0, 1), state


def rwkv_finish(o, r, v, g, k_f, k_b, r_k, lnx_g, lnx_b):
    B, T = o.shape[:2]
    mu = jnp.mean(o, axis=-1, keepdims=True)
    var = jnp.mean(jnp.square(o - mu), axis=-1, keepdims=True)
    on = (o - mu) * lax.rsqrt(var + LNX_EPS)
    on = on * lnx_g.reshape(RWKV_HEADS, RWKV_HEAD) + lnx_b.reshape(RWKV_HEADS, RWKV_HEAD)
    bonus = jnp.sum(r * (k_f + k_b) * r_k, axis=-1, keepdims=True) * v
    return (on + bonus).reshape(B, T, RWKV_WIDTH).astype(g.dtype) * g


def rwkv7_bidirectional(ul, uc, prep_params, r_k, lnx_g, lnx_b, need_ctx):
    B = ul.shape[0]
    rl, vl, kkl, gl, dl = rwkv_prep(ul, *prep_params)
    rc, vc, kkc, gc, dc = rwkv_prep(uc, *prep_params)
    s0 = jnp.zeros((B, RWKV_HEADS, RWKV_HEAD, RWKV_HEAD), jnp.float32)
    outs_l, outs_c = [], []
    for d in (0, 1):
        rev = d == 1
        wc, ac, kc = dc[d]
        o_c, s_ctx = wkv7_scan(rc, wc, kc, vc, kkc, ac, s0, rev)
        wl, al, kl = dl[d]
        o_l, _ = wkv7_scan(rl, wl, kl, vl, kkl, al, s_ctx, rev)
        outs_l.append(o_l)
        outs_c.append(o_c)
    yl = rwkv_finish(outs_l[0] + outs_l[1], rl, vl, gl, dl[0][2], dl[1][2], r_k, lnx_g, lnx_b)
    if not need_ctx:
        return yl, None
    yc = rwkv_finish(outs_c[0] + outs_c[1], rc, vc, gc, dc[0][2], dc[1][2], r_k, lnx_g, lnx_b)
    return yl, yc


def mla_project(u, q_norm_g, w_uq, kv_norm_g, w_ukv):
    B, T, _ = u.shape
    qd, kvd, k_rope = split_cols(u, (MLA_Q_RANK, MLA_KV_RANK, MLA_ROPE))
    q = (rmsnorm(qd, q_norm_g) @ w_uq).reshape(B, T, MLA_HEADS, MLA_NOPE + MLA_ROPE)
    kv = (rmsnorm(kvd, kv_norm_g) @ w_ukv).reshape(B, T, MLA_HEADS, MLA_NOPE + MLA_V)
    return q[..., :MLA_NOPE], q[..., MLA_NOPE:], kv[..., :MLA_NOPE], k_rope[:, :, None, :], kv[..., MLA_NOPE:]


def mla_qk(q_nope, q_rope, k_nope, k_rope):
    q = jnp.concatenate([q_nope, q_rope], axis=-1) * (MLA_NOPE + MLA_ROPE) ** -0.5
    k = jnp.concatenate([k_nope, jnp.broadcast_to(k_rope, k_nope.shape[:-1] + (MLA_ROPE,))], axis=-1)
    return q, k


def dense_attention_blocks(q, k, v):
    B, T, H, dq = q.shape
    nb = T // MLA_BLOCK
    qb = jnp.moveaxis(q.reshape(B, nb, MLA_BLOCK, H, dq), 1, 0)

    def one_block(q_blk):
        s = jnp.einsum("bqhd,bkhd->bhqk", q_blk, k).astype(jnp.float32)
        p = jax.nn.softmax(s, axis=-1).astype(v.dtype)
        return jnp.einsum("bhqk,bkhd->bqhd", p, v)

    o = lax.map(one_block, qb)
    return jnp.moveaxis(o, 0, 1).reshape(B, T, H * v.shape[-1])


def mla_attention(ul, uc, mla_params, row, col, need_ctx):
    qn_l, qr_l, kn_l, kr_l, v_l = mla_project(ul, *mla_params)
    qn_c, qr_c, kn_c, kr_c, v_c = mla_project(uc, *mla_params)
    q_l, k_l = mla_qk(qn_l, rope_2d(qr_l, row, col), kn_l, rope_2d(kr_l, row, col))
    q_c, k_c = mla_qk(qn_c, qr_c, kn_c, kr_c)
    yl = dense_attention_blocks(q_l, jnp.concatenate([k_c, k_l], axis=1), jnp.concatenate([v_c, v_l], axis=1))
    if not need_ctx:
        return yl, None
    return yl, dense_attention_blocks(q_c, k_c, v_c)


def rwkv_mla_mixer(hl, hc, w_in, w_out, prep_params, r_k, lnx_g, lnx_b, mla_params, row, col, need_ctx):
    ul = hl @ w_in
    uc = hc @ w_in
    rw_l, rw_c = rwkv7_bidirectional(ul[..., :RWKV_IN], uc[..., :RWKV_IN], prep_params, r_k, lnx_g, lnx_b, need_ctx)
    at_l, at_c = mla_attention(ul[..., RWKV_IN:], uc[..., RWKV_IN:], mla_params, row, col, need_ctx)
    yl = jnp.concatenate([rw_l, at_l], axis=-1) @ w_out
    if not need_ctx:
        return yl, None
    return yl, jnp.concatenate([rw_c, at_c], axis=-1) @ w_out


def expert_choice_ffn(h, w_router, w_gate, w_up, w_down):
    B, T, D = h.shape
    cap = EC_FACTOR * T // N_EXPERTS
    aff = jax.nn.softmax((h @ w_router).astype(jnp.float32), axis=-1)
    gate, idx = lax.top_k(jnp.swapaxes(aff, 1, 2), cap)
    xs = jax.vmap(lambda hb, ib: hb[ib])(h, idx)
    hid = jax.nn.silu(jnp.einsum("becd,edf->becf", xs, w_gate)) * jnp.einsum("becd,edf->becf", xs, w_up)
    y = jnp.einsum("becf,efd->becd", hid, w_down) * gate[..., None].astype(h.dtype)
    return jax.vmap(lambda yb, ib: jnp.zeros((T, D), h.dtype).at[ib.reshape(-1)].add(yb.reshape(-1, D)))(y, idx)


def setup_inputs(seed: int = 0) -> dict:
    key = jax.random.key(seed)
    ks = iter(jax.random.split(key, 48))
    n_even = (DEPTH + 1) // 2
    n_odd = DEPTH // 2
    D = D_MODEL

    def nrm(shape, scale):
        return jax.random.normal(next(ks), shape, jnp.float32) * scale

    def gain(shape):
        return 1.0 + nrm(shape, 0.02)

    def unif(shape, lo, hi):
        return jax.random.uniform(next(ks), shape, jnp.float32, lo, hi)

    return {
        "x": nrm((BATCH, SEQ, D), 1.0),
        "c": nrm((BATCH, D), 1.0),
        "ctx": nrm((BATCH, CTX_LEN, D), 1.0),
        "c_ctx": nrm((D,), 1.0),
        "ada_w": nrm((DEPTH, D, 6 * D), D ** -0.5),
        "ada_b": nrm((DEPTH, 6 * D), 0.02),
        "norm_mix_g": gain((DEPTH, D)),
        "norm_ffn_g": gain((DEPTH, D)),
        "router_w": nrm((DEPTH, D, N_EXPERTS), D ** -0.5),
        "exp_w_gate": nrm((DEPTH, N_EXPERTS, D, EXPERT_FF), D ** -0.5),
        "exp_w_up": nrm((DEPTH, N_EXPERTS, D, EXPERT_FF), D ** -0.5),
        "exp_w_down": nrm((DEPTH, N_EXPERTS, EXPERT_FF, D), EXPERT_FF ** -0.5),
        "ev_w_in": nrm((n_even, D, EVEN_IN), D ** -0.5),
        "ev_w_out": nrm((n_even, MIX_WIDTH, D), MIX_WIDTH ** -0.5),
        "ev_sink": nrm((n_even, A_HEADS), 1.0),
        "ev_pool_w": nrm((n_even, len(POOL_WINDOWS), POOL_GROUP, POOL_GROUP), POOL_GROUP ** -0.5),
        "ev_pool_scale": 1.0 + nrm((n_even, POOL_WIDTH), 0.1),
        "od_w_in": nrm((n_odd, D, ODD_IN), D ** -0.5),
        "od_w_out": nrm((n_odd, MIX_WIDTH, D), MIX_WIDTH ** -0.5),
        "od_mu_prev": unif((n_odd, RWKV_IN), 0.0, 0.5),
        "od_mu_next": unif((n_odd, RWKV_IN), 0.0, 0.5),
        "od_w0": unif((n_odd, 2, RWKV_WIDTH), -5.0, 0.0),
        "od_w2": nrm((n_odd, 2, DECAY_LORA, RWKV_WIDTH), 0.5 * DECAY_LORA ** -0.5),
        "od_a0": nrm((n_odd, 2, RWKV_WIDTH), 0.1),
        "od_a2": nrm((n_odd, 2, AAA_LORA, RWKV_WIDTH), 0.5 * AAA_LORA ** -0.5),
        "od_g2": nrm((n_odd, GATE_LORA, RWKV_WIDTH), GATE_LORA ** -0.5),
        "od_k_k": 0.85 + nrm((n_odd, RWKV_WIDTH), 0.05),
        "od_k_a": 1.0 + nrm((n_odd, RWKV_WIDTH), 0.05),
        "od_r_k": nrm((n_odd, RWKV_HEADS, RWKV_HEAD), 0.1),
        "od_lnx_g": gain((n_odd, RWKV_WIDTH)),
        "od_lnx_b": nrm((n_odd, RWKV_WIDTH), 0.02),
        "od_q_norm_g": gain((n_odd, MLA_Q_RANK)),
        "od_w_uq": nrm((n_odd, MLA_Q_RANK, MLA_HEADS * (MLA_NOPE + MLA_ROPE)), MLA_Q_RANK ** -0.5),
        "od_kv_norm_g": gain((n_odd, MLA_KV_RANK)),
        "od_w_ukv": nrm((n_odd, MLA_KV_RANK, MLA_HEADS * (MLA_NOPE + MLA_V)), MLA_KV_RANK ** -0.5),
        "final_norm_g": gain((D,)),
    }


def reference(x, c, ctx, c_ctx, ada_w, ada_b, norm_mix_g, norm_ffn_g, router_w, exp_w_gate, exp_w_up,
              exp_w_down, ev_w_in, ev_w_out, ev_sink, ev_pool_w, ev_pool_scale, od_w_in, od_w_out,
              od_mu_prev, od_mu_next, od_w0, od_w2, od_a0, od_a2, od_g2, od_k_k, od_k_a, od_r_k,
              od_lnx_g, od_lnx_b, od_q_norm_g, od_w_uq, od_kv_norm_g, od_w_ukv, final_norm_g):
    B, T, D = x.shape
    row, col = grid_positions(T)
    xl, xc = x, ctx
    for l in range(DEPTH):
        need_ctx = l < DEPTH - 1
        i = l // 2
        ml = (jax.nn.silu(c) @ ada_w[l] + ada_b[l]).reshape(B, 1, 6, D)
        mc = (jax.nn.silu(c_ctx) @ ada_w[l] + ada_b[l]).reshape(1, 1, 6, D)
        hl = modulate(xl, norm_mix_g[l], ml[:, :, 0], ml[:, :, 1])
        hc = modulate(xc, norm_mix_g[l], mc[:, :, 0], mc[:, :, 1])
        if l % 2 == 0:
            yl, yc = attn_pool_mixer(hl, hc, ev_w_in[i], ev_w_out[i], ev_sink[i], ev_pool_w[i],
                                     ev_pool_scale[i], row, col, need_ctx)
        else:
            prep_params = (od_mu_prev[i], od_mu_next[i], od_w0[i], od_w2[i], od_a0[i], od_a2[i],
                           od_g2[i], od_k_k[i], od_k_a[i])
            mla_params = (od_q_norm_g[i], od_w_uq[i], od_kv_norm_g[i], od_w_ukv[i])
            yl, yc = rwkv_mla_mixer(hl, hc, od_w_in[i], od_w_out[i], prep_params, od_r_k[i],
                                    od_lnx_g[i], od_lnx_b[i], mla_params, row, col, need_ctx)
        ffn = (router_w[l], exp_w_gate[l], exp_w_up[l], exp_w_down[l])
        xl = xl + ml[:, :, 2] * yl
        xl = xl + ml[:, :, 5] * expert_choice_ffn(modulate(xl, norm_ffn_g[l], ml[:, :, 3], ml[:, :, 4]), *ffn)
        if need_ctx:
            xc = xc + mc[:, :, 2] * yc
            xc = xc + mc[:, :, 5] * expert_choice_ffn(modulate(xc, norm_ffn_g[l], mc[:, :, 3], mc[:, :, 4]), *ffn)
    return rmsnorm(xl, final_norm_g)
```

```python
import functools

import numpy as np
import jax
import jax.numpy as jnp
from jax import lax
from jax.experimental import pallas as pl
from jax.experimental.pallas import tpu as pltpu

F32 = jnp.float32
BF16 = jnp.bfloat16
I32 = jnp.int32
HIGHEST = lax.Precision.HIGHEST

D = 1024
T_L = 2048
T_C = 256
NT = T_L + T_C
GRID_W = 64
EPS = 1e-6
ROPE_BASE = 10000.0

TM = 256
N_TILES = NT // TM
N_LAT_TILES = T_L // TM

A_HEADS = 8
A_KV_HEADS = 2
HEAD_DIM = 64
A_WINDOW = 128
A_BLOCK = 128
N_ABLK = NT // A_BLOCK
N_LAT_ABLK = T_L // A_BLOCK

POOL_WINDOWS = (2, 4, 8, 16)
POOL_GROUP = 128

RWKV_HEADS = 8
RWKV_HEAD = 64
RWKV_WIDTH = 512
RWKV_IN = 1920
LNX_EPS = 64e-5
WKV_CHUNK = 64
N_CHUNKS = NT // WKV_CHUNK
N_CTX_CHUNKS = T_C // WKV_CHUNK

MLA_HEADS = 8
MLA_Q_RANK = 256
MLA_KV_RANK = 128
MLA_NOPE = 64
MLA_ROPE = 32
MLA_V = 64
MLA_SLOT = 128
MLA_U = 640

N_EXPERTS = 16
EXPERT_FF = 2048
CAP_L = 2 * T_L // N_EXPERTS
CAP_C = 2 * T_C // N_EXPERTS
FF_CHUNK = 256

VMEM_LIMIT_BYTES = 56 * 1024 * 1024


def _cp(*sem):
    return pltpu.CompilerParams(dimension_semantics=sem, vmem_limit_bytes=VMEM_LIMIT_BYTES)


def _split_bf16(x):
    hi = x.astype(BF16)
    lo = (x - hi.astype(F32)).astype(BF16)
    return hi, lo


def _dot(a, b):
    return jnp.dot(a, b, preferred_element_type=F32)


def _dot_nt(a, b):
    return lax.dot_general(a, b, (((1,), (1,)), ((), ())), preferred_element_type=F32)


def _dot_tn(a, b):
    return lax.dot_general(a, b, (((0,), (0,)), ((), ())), preferred_element_type=F32)


def _dot2(a_f32, b_bf16):
    hi, lo = _split_bf16(a_f32)
    return _dot(hi, b_bf16) + _dot(lo, b_bf16)


def _modulate(x, g, shift, scale):
    ms = jnp.mean(x * x, axis=-1, keepdims=True)
    return x * lax.rsqrt(ms + EPS) * g * (1.0 + scale) + shift


def _rmsnorm(x, g):
    ms = jnp.mean(x * x, axis=-1, keepdims=True)
    return x * lax.rsqrt(ms + EPS) * g


def _modvec_kernel(a_ref, w_ref, b_ref, o_ref):
    a = a_ref[...]
    a = a * jax.nn.sigmoid(a)
    o_ref[0] = jnp.dot(a, w_ref[0], precision=HIGHEST, preferred_element_type=F32) + b_ref[0]


def _modvec(c, c_ctx, ada_w, ada_b):
    L = ada_w.shape[0]
    B = c.shape[0]
    assert B < 16
    rows = jnp.zeros((16, D), F32).at[:B].set(c).at[B].set(c_ctx)
    tn = 512
    out = pl.pallas_call(
        _modvec_kernel,
        grid=(L, 6 * D // tn),
        in_specs=[pl.BlockSpec((16, D), lambda l, n: (0, 0)),
                  pl.BlockSpec((1, D, tn), lambda l, n: (l, 0, n)),
                  pl.BlockSpec((1, 1, tn), lambda l, n: (l, 0, n))],
        out_specs=pl.BlockSpec((1, 16, tn), lambda l, n: (l, 0, n)),
        out_shape=jax.ShapeDtypeStruct((L, 16, 6 * D), F32),
        compiler_params=_cp("parallel", "parallel"),
        name="modvec",
    )(rows, ada_w, ada_b.reshape(L, 1, 6 * D))
    m = out.reshape(L, 16, 6, D)
    ml = m[:, :B]
    mc = jnp.broadcast_to(m[:, B:B + 1], (L, B, 6, D))
    mod = jnp.stack([ml, mc], axis=2)
    mod = jnp.pad(mod, ((0, 0), (0, 0), (0, 0), (0, 2), (0, 0)))
    return mod.reshape(L, B * 2, 8, D)


def _mod_spec():
    return pl.BlockSpec((1, 8, D), lambda b, t: (b * 2 + t // N_LAT_TILES, 0, 0))


def _rope_block(width, pos_row, pos_col):
    d = width // 2
    half = d // 2
    inv = ROPE_BASE ** (-np.arange(half, dtype=np.float32) / half)
    cos = np.zeros((pos_row.shape[0], width), np.float32)
    sin = np.zeros_like(cos)
    partner = np.zeros((width,), np.int64)
    for part, pos in ((0, pos_row), (1, pos_col)):
        ang = pos.astype(np.float32)[:, None] * inv[None, :]
        c, s = np.cos(ang), np.sin(ang)
        o = part * d
        cos[:, o:o + half] = c
        cos[:, o + half:o + d] = c
        sin[:, o:o + half] = -s
        sin[:, o + half:o + d] = s
        partner[o:o + half] = np.arange(o + half, o + d)
        partner[o + half:o + d] = np.arange(o, o + half)
    return cos, sin, partner


def _latent_positions():
    t = np.arange(T_L)
    return t // GRID_W, t % GRID_W


def _even_tables():
    cos, sin, partner = _rope_block(HEAD_DIM, *_latent_positions())
    scale = HEAD_DIM ** -0.5
    ones = np.ones((T_C, HEAD_DIM), np.float32)
    zeros = np.zeros((T_C, HEAD_DIM), np.float32)
    cos = np.concatenate([cos, ones], axis=0)
    sin = np.concatenate([sin, zeros], axis=0)
    cq = np.tile(cos, (1, A_HEADS)) * scale
    sq = np.tile(sin, (1, A_HEADS)) * scale
    ck = np.tile(cos, (1, 2 * A_KV_HEADS))
    sk = np.tile(sin, (1, 2 * A_KV_HEADS))
    return (jnp.asarray(cq), jnp.asarray(sq), jnp.asarray(ck), jnp.asarray(sk)), partner


def _mla_tables():
    cos, sin, partner = _rope_block(MLA_ROPE, *_latent_positions())
    cos = np.concatenate([cos, np.ones((T_C, MLA_ROPE), np.float32)], axis=0)
    sin = np.concatenate([sin, np.zeros((T_C, MLA_ROPE), np.float32)], axis=0)
    scale = (MLA_NOPE + MLA_ROPE) ** -0.5
    cq1 = np.zeros((NT, MLA_SLOT), np.float32)
    sq1 = np.zeros((NT, MLA_SLOT), np.float32)
    cq1[:, :MLA_NOPE] = 1.0
    cq1[:, MLA_NOPE:MLA_NOPE + MLA_ROPE] = cos
    sq1[:, MLA_NOPE:MLA_NOPE + MLA_ROPE] = sin
    cq = np.tile(cq1, (1, MLA_HEADS)) * scale
    sq = np.tile(sq1, (1, MLA_HEADS)) * scale
    ck = np.zeros((NT, MLA_SLOT), np.float32)
    sk = np.zeros((NT, MLA_SLOT), np.float32)
    ck[:, :MLA_ROPE] = cos
    sk[:, :MLA_ROPE] = sin
    return (jnp.asarray(cq), jnp.asarray(sq), jnp.asarray(ck), jnp.asarray(sk)), partner


EV_Q, EV_K, EV_V, EV_P = 512, 256, 256, 512
EV_N = 2 * EV_Q + 2 * EV_K + EV_V + EV_P


def _even_weights(w_in, partner):
    q = w_in[:, :512]
    k = w_in[:, 512:640]
    v = w_in[:, 640:768]
    p = w_in[:, 768:1280]
    pq = np.concatenate([h * HEAD_DIM + partner for h in range(A_HEADS)])
    pk = np.concatenate([h * HEAD_DIM + partner for h in range(A_KV_HEADS)])
    dup = np.concatenate([np.arange(HEAD_DIM), np.arange(HEAD_DIM),
                          HEAD_DIM + np.arange(HEAD_DIM), HEAD_DIM + np.arange(HEAD_DIM)])
    w = jnp.concatenate([q, q[:, pq], k[:, dup], k[:, pk][:, dup], v[:, dup], p], axis=1)
    return w.astype(BF16)


def _inproj_even_kernel(x_ref, mod_ref, g_ref, w_ref, cq_ref, sq_ref, ck_ref, sk_ref,
                        q_ref, k_ref, v_ref, p_ref):
    mod = mod_ref[0]
    h = _modulate(x_ref[0], g_ref[...], mod[0:1], mod[1:2])
    y = _dot(h.astype(BF16), w_ref[...])
    o = 0
    q = y[:, o:o + EV_Q] * cq_ref[...] + y[:, o + EV_Q:o + 2 * EV_Q] * sq_ref[...]
    o += 2 * EV_Q
    k = y[:, o:o + EV_K] * ck_ref[...] + y[:, o + EV_K:o + 2 * EV_K] * sk_ref[...]
    o += 2 * EV_K
    q_ref[0] = q.astype(BF16)
    k_ref[0] = k.astype(BF16)
    v_ref[0] = y[:, o:o + EV_V].astype(BF16)
    p_ref[0] = y[:, o + EV_V:o + EV_V + EV_P]


def _inproj_even(x, mod, g, w, tables):
    B = x.shape[0]
    cq, sq, ck, sk = tables
    tok = lambda n: pl.BlockSpec((1, TM, n), lambda b, t: (b, t, 0))
    tab = lambda n: pl.BlockSpec((TM, n), lambda b, t: (t, 0))
    return pl.pallas_call(
        _inproj_even_kernel,
        grid=(B, N_TILES),
        in_specs=[tok(D), _mod_spec(), pl.BlockSpec((1, D), lambda b, t: (0, 0)),
                  pl.BlockSpec((D, EV_N), lambda b, t: (0, 0)),
                  tab(EV_Q), tab(EV_Q), tab(EV_K), tab(EV_K)],
        out_specs=[tok(EV_Q), tok(EV_K), tok(EV_V), tok(EV_P)],
        out_shape=[jax.ShapeDtypeStruct((B, NT, EV_Q), BF16),
                   jax.ShapeDtypeStruct((B, NT, EV_K), BF16),
                   jax.ShapeDtypeStruct((B, NT, EV_V), BF16),
                   jax.ShapeDtypeStruct((B, NT, EV_P), F32)],
        compiler_params=_cp("parallel", "parallel"),
        name="inproj_even",
    )(x, mod, g.reshape(1, D), w, cq, sq, ck, sk)


def _attn_even_kernel(sink_ref, q_ref, kp_ref, kc_ref, kn_ref, kx_ref,
                      vp_ref, vc_ref, vn_ref, vx_ref, o_ref):
    j = pl.program_id(1)
    is_lat = j < N_LAT_ABLK
    q = q_ref[0]
    qpos = j * A_BLOCK + lax.broadcasted_iota(I32, (A_BLOCK, 3 * A_BLOCK), 0)
    kpos = (j - 1) * A_BLOCK + lax.broadcasted_iota(I32, (A_BLOCK, 3 * A_BLOCK), 1)
    valid = (jnp.abs(kpos - qpos) <= A_WINDOW) & (kpos >= 0) & (kpos < T_L) & is_lat
    lo_lanes = lax.broadcasted_iota(I32, (A_BLOCK, 128), 1) < HEAD_DIM
    for p in range(A_HEADS // 2):
        kv = p // 2
        ks = slice(kv * 128, (kv + 1) * 128)
        kall = jnp.concatenate([kx_ref[0, :, ks], kp_ref[0, :, ks], kc_ref[0, :, ks], kn_ref[0, :, ks]], axis=0)
        vall = jnp.concatenate([vx_ref[0, :, ks], vp_ref[0, :, ks], vc_ref[0, :, ks], vn_ref[0, :, ks]], axis=0)
        qp = q[:, p * 128:(p + 1) * 128]
        outs = []
        for hh in range(2):
            snk = sink_ref[2 * p + hh]
            qm = jnp.where(lo_lanes if hh == 0 else jnp.logical_not(lo_lanes), qp, jnp.zeros_like(qp))
            s = _dot_nt(qm, kall)
            s_ctx = s[:, :T_C]
            s_loc = jnp.where(valid, s[:, T_C:], -jnp.inf)
            m = jnp.maximum(jnp.max(s_ctx, axis=-1, keepdims=True), jnp.max(s_loc, axis=-1, keepdims=True))
            m = jnp.maximum(m, snk)
            e_ctx = jnp.exp(s_ctx - m)
            e_loc = jnp.exp(s_loc - m)
            den = (jnp.sum(e_ctx, axis=-1, keepdims=True) + jnp.sum(e_loc, axis=-1, keepdims=True)
                   + jnp.exp(snk - m))
            pr = jnp.concatenate([e_ctx, e_loc], axis=1) * (1.0 / den)
            outs.append(_dot(pr.astype(BF16), vall))
        o_ref[0, :, p * 128:(p + 1) * 128] = jnp.where(lo_lanes, outs[0], outs[1]).astype(o_ref.dtype)


def _attn_even(q, kd, vd, sink):
    B = q.shape[0]
    last = N_LAT_ABLK - 1
    blk = lambda f: pl.BlockSpec((1, A_BLOCK, 256), f)
    prev = lambda b, j: (b, jnp.clip(j - 1, 0, last), 0)
    cur = lambda b, j: (b, j, 0)
    nxt = lambda b, j: (b, jnp.clip(j + 1, 0, last), 0)
    ctx = pl.BlockSpec((1, T_C, 256), lambda b, j: (b, T_L // T_C, 0))
    return pl.pallas_call(
        _attn_even_kernel,
        grid=(B, N_ABLK),
        in_specs=[pl.BlockSpec(memory_space=pltpu.SMEM),
                  pl.BlockSpec((1, A_BLOCK, EV_Q), cur),
                  blk(prev), blk(cur), blk(nxt), ctx,
                  blk(prev), blk(cur), blk(nxt), ctx],
        out_specs=pl.BlockSpec((1, A_BLOCK, 512), cur),
        out_shape=jax.ShapeDtypeStruct((B, NT, 512), BF16),
        compiler_params=_cp("parallel", "parallel"),
        name="attn_even",
    )(sink, q, kd, kd, kd, kd, vd, vd, vd, vd)


def _pool_kernel(up_ref, uc_ref, un_ref, w_ref, sc_ref, o_ref):
    j = pl.program_id(1)
    is_lat = j < N_LAT_ABLK
    seg_lo = jnp.where(is_lat, 0, T_L)
    seg_hi = jnp.where(is_lat, T_L, NT)
    u3 = jnp.concatenate([up_ref[0], uc_ref[0], un_ref[0]], axis=0)
    uc = uc_ref[0]
    r = j * A_BLOCK + lax.broadcasted_iota(I32, (A_BLOCK, 3 * A_BLOCK), 0)
    c = (j - 1) * A_BLOCK + lax.broadcasted_iota(I32, (A_BLOCK, 3 * A_BLOCK), 1)
    r1 = j * A_BLOCK + lax.broadcasted_iota(I32, (A_BLOCK, 1), 0)
    outs = []
    for gi, w in enumerate(POOL_WINDOWS):
        cs = slice(gi * POOL_GROUP, (gi + 1) * POOL_GROUP)
        lo = jnp.maximum(r - w // 2, seg_lo)
        hi = jnp.minimum(r + w // 2, seg_hi)
        band = ((c >= lo) & (c < hi)).astype(BF16)
        cnt = (jnp.minimum(r1 + w // 2, seg_hi) - jnp.maximum(r1 - w // 2, seg_lo)).astype(F32)
        hi_u, lo_u = _split_bf16(u3[:, cs])
        mean = (_dot(band, hi_u) + _dot(band, lo_u)) / cnt
        outs.append(_dot((mean - uc[:, cs]).astype(BF16), w_ref[gi]))
    o_ref[0] = (jnp.concatenate(outs, axis=1) * sc_ref[...]).astype(o_ref.dtype)


def _pool(u, pool_w, pool_scale):
    B = u.shape[0]
    blk = lambda f: pl.BlockSpec((1, A_BLOCK, 512), f)
    return pl.pallas_call(
        _pool_kernel,
        grid=(B, N_ABLK),
        in_specs=[blk(lambda b, j: (b, jnp.maximum(j - 1, 0), 0)),
                  blk(lambda b, j: (b, j, 0)),
                  blk(lambda b, j: (b, jnp.minimum(j + 1, N_ABLK - 1), 0)),
                  pl.BlockSpec((4, POOL_GROUP, POOL_GROUP), lambda b, j: (0, 0, 0)),
                  pl.BlockSpec((1, 512), lambda b, j: (0, 0))],
        out_specs=blk(lambda b, j: (b, j, 0)),
        out_shape=jax.ShapeDtypeStruct((B, NT, 512), BF16),
        compiler_params=_cp("parallel", "parallel"),
        name="pool",
    )(u, u, u, pool_w.astype(BF16), pool_scale.reshape(1, 512))


def _outproj_kernel(a_ref, p_ref, w_ref, x_ref, mod_ref, g_ref, wr_ref, xo_ref, h_ref, lg_ref):
    mod = mod_ref[0]
    y = _dot(a_ref[0], w_ref[0:512, :]) + _dot(p_ref[0], w_ref[512:1024, :])
    x = x_ref[0] + mod[2:3] * y
    xo_ref[0] = x
    h = _modulate(x, g_ref[...], mod[3:4], mod[4:5])
    h_ref[0] = h.astype(BF16)
    lg_ref[0] = lax.dot_general(wr_ref[...], h, (((1,), (1,)), ((), ())),
                                precision=HIGHEST, preferred_element_type=F32)


def _outproj(a, p, w_out, x, mod, g_ffn, w_router):
    B = x.shape[0]
    tok = lambda n: pl.BlockSpec((1, TM, n), lambda b, t: (b, t, 0))
    return pl.pallas_call(
        _outproj_kernel,
        grid=(B, N_TILES),
        in_specs=[tok(512), tok(512), pl.BlockSpec((D, D), lambda b, t: (0, 0)), tok(D), _mod_spec(),
                  pl.BlockSpec((1, D), lambda b, t: (0, 0)),
                  pl.BlockSpec((N_EXPERTS, D), lambda b, t: (0, 0))],
        out_specs=[tok(D), tok(D), pl.BlockSpec((1, N_EXPERTS, TM), lambda b, t: (b, 0, t))],
        out_shape=[jax.ShapeDtypeStruct((B, NT, D), F32),
                   jax.ShapeDtypeStruct((B, NT, D), BF16),
                   jax.ShapeDtypeStruct((B, N_EXPERTS, NT), F32)],
        compiler_params=_cp("parallel", "parallel"),
        name="outproj",
    )(a, p, w_out.astype(BF16), x, mod, g_ffn.reshape(1, D), w_router.T)


def _topk_kernel(lg_ref, sid_ref, gate_ref, *, cap):
    B, E, T = lg_ref.shape
    R = B * E
    lg = lg_ref[...]
    m = jnp.max(lg, axis=1, keepdims=True)
    e = jnp.exp(lg - m)
    aff = (e / jnp.sum(e, axis=1, keepdims=True)).reshape(R, T)
    bits = lax.bitcast_convert_type(aff, I32)

    def step(i, thr):
        cand = thr | jnp.left_shift(jnp.int32(1), 30 - i)
        cnt = jnp.sum((bits >= cand).astype(F32), axis=1, keepdims=True)
        return jnp.where(cnt >= cap, cand, thr)

    thr = lax.fori_loop(0, 31, step, jnp.zeros((R, 1), I32))
    gt = bits > thr
    eq = bits == thr
    need = cap - jnp.sum(gt.astype(F32), axis=1, keepdims=True)

    blk = 256
    tri = (lax.broadcasted_iota(I32, (blk, blk), 0) <= lax.broadcasted_iota(I32, (blk, blk), 1)).astype(BF16)

    def prefix(mask_f32):
        off = jnp.zeros((R, 1), F32)
        parts = []
        for k in range(T // blk):
            cb = _dot(mask_f32[:, k * blk:(k + 1) * blk].astype(BF16), tri) + off
            parts.append(cb)
            off = cb[:, blk - 1:blk]
        return parts

    eq_parts = prefix(eq.astype(F32))
    sel_parts = []
    for k in range(T // blk):
        sl = slice(k * blk, (k + 1) * blk)
        sel_parts.append(gt[:, sl] | (eq[:, sl] & (eq_parts[k] <= need)))
    sel = jnp.concatenate(sel_parts, axis=1)
    pos_parts = prefix(sel.astype(F32))
    for k in range(T // blk):
        sl = slice(k * blk, (k + 1) * blk)
        sid = jnp.where(sel_parts[k], pos_parts[k].astype(I32) - 1, -1)
        sid_ref[:, :, sl] = sid.reshape(B, E, blk)
        gate_ref[:, :, sl] = jnp.where(sel_parts[k], aff[:, sl], 0.0).reshape(B, E, blk)


def _topk(logits, cap):
    B, E, T = logits.shape
    full = pl.BlockSpec((B, E, T), lambda i: (0, 0, 0))
    return pl.pallas_call(
        functools.partial(_topk_kernel, cap=cap),
        grid=(1,),
        in_specs=[full],
        out_specs=[full, full],
        out_shape=[jax.ShapeDtypeStruct((B, E, T), I32), jax.ShapeDtypeStruct((B, E, T), F32)],
        compiler_params=_cp("arbitrary"),
        name=f"topk_{T}",
    )(logits)


def _gather_kernel(h_ref, sid_ref, gate_ref, xs_ref, gs_ref, *, cap, chunk):
    T = h_ref.shape[1]
    acc = jnp.zeros((cap, D), F32)
    gacc = jnp.zeros((cap, 1), F32)
    slot = lax.broadcasted_iota(I32, (cap, chunk), 0)
    for k in range(T // chunk):
        sl = slice(k * chunk, (k + 1) * chunk)
        hit = slot == sid_ref[0, 0, :, sl]
        acc = acc + _dot(hit.astype(BF16), h_ref[0, sl, :])
        gacc = gacc + jnp.sum(jnp.where(hit, gate_ref[0, 0, :, sl], 0.0), axis=1, keepdims=True)
    xs_ref[0] = acc.astype(BF16)
    gs_ref[0] = jnp.broadcast_to(gacc, (cap, 128))


def _gather(h, sid, gate, cap, row_block):
    B, E, T = sid.shape
    chunk = min(T, 512)
    idx = lambda b, e: (b, e, 0, 0)
    return pl.pallas_call(
        functools.partial(_gather_kernel, cap=cap, chunk=chunk),
        grid=(B, E),
        in_specs=[pl.BlockSpec((1, T, D), lambda b, e: (b, row_block, 0)),
                  pl.BlockSpec((1, 1, 1, T), idx), pl.BlockSpec((1, 1, 1, T), idx)],
        out_specs=[pl.BlockSpec((1, cap, D), lambda b, e: (e, b, 0)),
                   pl.BlockSpec((1, cap, 128), lambda b, e: (e, b, 0))],
        out_shape=[jax.ShapeDtypeStruct((E, B * cap, D), BF16),
                   jax.ShapeDtypeStruct((E, B * cap, 128), F32)],
        compiler_params=_cp("parallel", "arbitrary"),
        name=f"gather_{T}",
    )(h, sid.reshape(B, E, 1, T), gate.reshape(B, E, 1, T))


def _ffn_kernel(*refs, row_tiles):
    n_seg = len(row_tiles)
    xs_refs = refs[:n_seg]
    gs_refs = refs[n_seg:2 * n_seg]
    wg_ref, wu_ref, wd_ref = refs[2 * n_seg:2 * n_seg + 3]
    y_refs = refs[2 * n_seg + 3:3 * n_seg + 3]
    acc_refs = refs[3 * n_seg + 3:]
    f = pl.program_id(1)
    wg = wg_ref[0].astype(BF16)
    wu = wu_ref[0].astype(BF16)
    wd = wd_ref[0].astype(BF16)
    for xs_ref, gs_ref, y_ref, acc_ref, rt in zip(xs_refs, gs_refs, y_refs, acc_refs, row_tiles):
        M = xs_ref.shape[1]
        for r0 in range(0, M, rt):
            rows = pl.ds(r0, rt)
            x = xs_ref[0, rows, :]
            g = _dot(x, wg)
            u = _dot(x, wu)
            hid = (g * jax.nn.sigmoid(g) * u).astype(BF16)
            contrib = _dot(hid, wd)

            @pl.when(f == 0)
            def _():
                acc_ref[rows, :] = contrib

            @pl.when(f > 0)
            def _():
                acc_ref[rows, :] += contrib

        @pl.when(f == pl.num_programs(1) - 1)
        def _():
            gs = gs_ref[0]
            for k in range(D // 128):
                cs = slice(k * 128, (k + 1) * 128)
                y_ref[0, :, cs] = (acc_ref[:, cs] * gs).astype(y_ref.dtype)


def _ffn(xs_list, gs_list, w_gate, w_up, w_down):
    E = w_gate.shape[0]
    row_tiles = tuple(min(xs.shape[1], 512) for xs in xs_list)
    nf = EXPERT_FF // FF_CHUNK
    in_specs = [pl.BlockSpec((1, xs.shape[1], D), lambda e, f: (e, 0, 0)) for xs in xs_list]
    in_specs += [pl.BlockSpec((1, gs.shape[1], 128), lambda e, f: (e, 0, 0)) for gs in gs_list]
    in_specs += [pl.BlockSpec((1, D, FF_CHUNK), lambda e, f: (e, 0, f)),
                 pl.BlockSpec((1, D, FF_CHUNK), lambda e, f: (e, 0, f)),
                 pl.BlockSpec((1, FF_CHUNK, D), lambda e, f: (e, f, 0))]
    return pl.pallas_call(
        functools.partial(_ffn_kernel, row_tiles=row_tiles),
        grid=(E, nf),
        in_specs=in_specs,
        out_specs=[pl.BlockSpec((1, xs.shape[1], D), lambda e, f: (e, 0, 0)) for xs in xs_list],
        out_shape=[jax.ShapeDtypeStruct(xs.shape, BF16) for xs in xs_list],
        scratch_shapes=[pltpu.VMEM((xs.shape[1], D), F32) for xs in xs_list],
        compiler_params=_cp("parallel", "arbitrary"),
        name="expert_ffn",
    )(*xs_list, *gs_list, w_gate, w_up, w_down)


def _combine_kernel(x_ref, sid_ref, y_ref, mod_ref, o_ref, acc_ref, *, cap):
    e = pl.program_id(2)
    tbs = x_ref.shape[1]
    hit = lax.broadcasted_iota(I32, (cap, tbs), 0) == sid_ref[0, 0]
    z = _dot_tn(hit.astype(BF16), y_ref[0])

    @pl.when(e == 0)
    def _():
        acc_ref[...] = z

    @pl.when(e > 0)
    def _():
        acc_ref[...] += z

    @pl.when(e == pl.num_programs(2) - 1)
    def _():
        o_ref[0] = x_ref[0] + mod_ref[0][5:6] * acc_ref[...]


def _combine(x, sid, y, mod, cap, tbs, row_block0, segment):
    B, E, T = sid.shape
    ntb = T // tbs
    xmap = lambda b, tb, e: (b, row_block0 + tb, 0)
    return pl.pallas_call(
        functools.partial(_combine_kernel, cap=cap),
        grid=(B, ntb, E),
        in_specs=[pl.BlockSpec((1, tbs, D), xmap),
                  pl.BlockSpec((1, 1, 1, tbs), lambda b, tb, e: (b, e, 0, tb)),
                  pl.BlockSpec((1, cap, D), lambda b, tb, e: (e, b, 0)),
                  pl.BlockSpec((1, 8, D), lambda b, tb, e: (b * 2 + segment, 0, 0))],
        out_specs=pl.BlockSpec((1, tbs, D), xmap),
        out_shape=jax.ShapeDtypeStruct(x.shape, F32),
        scratch_shapes=[pltpu.VMEM((tbs, D), F32)],
        input_output_aliases={0: 0},
        compiler_params=_cp("parallel", "parallel", "arbitrary"),
        name=f"combine_{T}",
    )(x, sid.reshape(B, E, 1, T), y, mod)


def _moe(x, h, logits, mod, w_gate, w_up, w_down, need_ctx):
    sid_l, gate_l = _topk(logits[:, :, :T_L], CAP_L)
    xs_l, gs_l = _gather(h, sid_l, gate_l, CAP_L, 0)
    if need_ctx:
        sid_c, gate_c = _topk(logits[:, :, T_L:], CAP_C)
        xs_c, gs_c = _gather(h, sid_c, gate_c, CAP_C, T_L // T_C)
        y_l, y_c = _ffn([xs_l, xs_c], [gs_l, gs_c], w_gate, w_up, w_down)
    else:
        (y_l,) = _ffn([xs_l], [gs_l], w_gate, w_up, w_down)
    x = _combine(x, sid_l, y_l, mod, CAP_L, 1024, 0, 0)
    if need_ctx:
        x = _combine(x, sid_c, y_c, mod, CAP_C, T_C, T_L // T_C, 1)
    return x


OD_N = RWKV_IN + MLA_U


def _odd_weights(w_in, partner):
    rw = w_in[:, :RWKV_IN]
    qd = w_in[:, RWKV_IN:RWKV_IN + MLA_Q_RANK]
    kvd = w_in[:, RWKV_IN + MLA_Q_RANK:RWKV_IN + MLA_Q_RANK + MLA_KV_RANK]
    kr = w_in[:, RWKV_IN + MLA_Q_RANK + MLA_KV_RANK:]
    pad = jnp.zeros((D, 128 - MLA_ROPE), w_in.dtype)
    w = jnp.concatenate([rw, qd, kvd, kr, pad, kr[:, partner], pad], axis=1)
    return w.astype(BF16)


def _inproj_odd_kernel(x_ref, mod_ref, g_ref, w_ref, ur_ref, um_ref):
    mod = mod_ref[0]
    h = _modulate(x_ref[0], g_ref[...], mod[0:1], mod[1:2])
    y = _dot(h.astype(BF16), w_ref[...])
    ur_ref[0] = y[:, :RWKV_IN]
    um_ref[0] = y[:, RWKV_IN:]


def _inproj_odd(x, mod, g, w):
    B = x.shape[0]
    tok = lambda n: pl.BlockSpec((1, TM, n), lambda b, t: (b, t, 0))
    return pl.pallas_call(
        _inproj_odd_kernel,
        grid=(B, N_TILES),
        in_specs=[tok(D), _mod_spec(), pl.BlockSpec((1, D), lambda b, t: (0, 0)),
                  pl.BlockSpec((D, OD_N), lambda b, t: (0, 0))],
        out_specs=[tok(RWKV_IN), tok(MLA_U)],
        out_shape=[jax.ShapeDtypeStruct((B, NT, RWKV_IN), F32),
                   jax.ShapeDtypeStruct((B, NT, MLA_U), F32)],
        compiler_params=_cp("parallel", "parallel"),
        name="inproj_odd",
    )(x, mod, g.reshape(1, D), w)


def _head_ones():
    h = np.arange(RWKV_WIDTH) // RWKV_HEAD
    return jnp.asarray((h[:, None] == h[None, :]).astype(np.float32)).astype(BF16)


def _headsum(x, bd):
    return _dot2(x, bd)


def _softplus(z):
    return jnp.maximum(z, 0.0) + jnp.log(1.0 + jnp.exp(-jnp.abs(z)))


def _rwkv_prep_kernel(u_ref, up_ref, un_ref, mup_ref, mun_ref, w0_ref, w2_ref, a0_ref, a2_ref, g2_ref,
                      kk_ref, ka_ref, bd_ref,
                      r_ref, v_ref, kkn_ref, g_ref, lw_ref, bq_ref, kd_ref):
    t = pl.program_id(1)
    u = u_ref[0]
    rows = lax.broadcasted_iota(I32, (TM, 1), 0)
    seg_start = (t == 0) | (t == N_LAT_TILES)
    seg_end = (t == N_LAT_TILES - 1) | (t == N_TILES - 1)
    prev_row = jnp.where(seg_start, 0.0, up_ref[0, 7:8, :])
    next_row = jnp.where(seg_end, 0.0, un_ref[0, 0:1, :])
    prev = jnp.where(rows == 0, prev_row, pltpu.roll(u, 1, 0))
    nxt = jnp.where(rows == TM - 1, next_row, pltpu.roll(u, TM - 1, 0))
    us = u + mup_ref[...] * (prev - u) + mun_ref[...] * (nxt - u)
    W = RWKV_WIDTH
    r = us[:, :W]
    k = us[:, W:2 * W]
    v = us[:, 2 * W:3 * W]
    wl = jnp.tanh(us[:, 3 * W:3 * W + 128]).astype(BF16)
    al = us[:, 3 * W + 128:3 * W + 256].astype(BF16)
    gl = jax.nn.sigmoid(us[:, 3 * W + 256:3 * W + 384]).astype(BF16)
    bd = bd_ref[...]
    kk = k * kk_ref[...]
    kk = kk * lax.rsqrt(jnp.maximum(_headsum(kk * kk, bd), 1e-24))
    r_ref[0] = r
    v_ref[0] = v
    kkn_ref[0] = kk
    g_ref[0] = _dot(gl, g2_ref[...])
    for d in range(2):
        wlog = -_softplus(-(w0_ref[d:d + 1, :] + _dot(wl, w2_ref[d]))) - 0.5
        a = jax.nn.sigmoid(a0_ref[d:d + 1, :] + _dot(al, a2_ref[d]))
        lw_ref[d, 0] = -jnp.exp(wlog)
        bq_ref[d, 0] = kk * a
        kd_ref[d, 0] = k * (1.0 + (a - 1.0) * ka_ref[...])


def _rwkv_prep(ur, mu_prev, mu_next, w0, w2, a0, a2, g2, k_k, k_a, bd):
    B = ur.shape[0]
    W = RWKV_WIDTH
    z = jnp.zeros((64, W), F32)
    w2p = jnp.stack([jnp.concatenate([w2[0], z]), jnp.concatenate([z, w2[1]])]).astype(BF16)
    a2p = jnp.stack([jnp.concatenate([a2[0], z]), jnp.concatenate([z, a2[1]])]).astype(BF16)
    tok = lambda n: pl.BlockSpec((1, TM, n), lambda b, t: (b, t, 0))
    tok2 = pl.BlockSpec((2, 1, TM, W), lambda b, t: (0, b, t, 0))
    row = lambda n: pl.BlockSpec((1, n), lambda b, t: (0, 0))
    full = lambda *s: pl.BlockSpec(s, lambda b, t: (0,) * len(s))
    slab = TM // 8
    f32 = lambda: jax.ShapeDtypeStruct((B, NT, W), F32)
    f32d = lambda: jax.ShapeDtypeStruct((2, B, NT, W), F32)
    return pl.pallas_call(
        _rwkv_prep_kernel,
        grid=(B, N_TILES),
        in_specs=[tok(RWKV_IN),
                  pl.BlockSpec((1, 8, RWKV_IN), lambda b, t: (b, jnp.maximum(t * slab - 1, 0), 0)),
                  pl.BlockSpec((1, 8, RWKV_IN), lambda b, t: (b, jnp.minimum((t + 1) * slab, NT // 8 - 1), 0)),
                  row(RWKV_IN), row(RWKV_IN), full(2, W), full(2, 128, W), full(2, W), full(2, 128, W),
                  full(128, W), row(W), row(W), full(W, W)],
        out_specs=[tok(W), tok(W), tok(W), tok(W), tok2, tok2, tok2],
        out_shape=[f32(), f32(), f32(), f32(), f32d(), f32d(), f32d()],
        compiler_params=_cp("parallel", "parallel"),
        name="rwkv_prep",
    )(ur, ur, ur, mu_prev.reshape(1, RWKV_IN), mu_next.reshape(1, RWKV_IN), w0, w2p, a0, a2p,
      g2.astype(BF16), k_k.reshape(1, W), k_a.reshape(1, W), bd)


def _wkv_kernel(r_ref, v_ref, kk_ref, lw_ref, bq_ref, kd_ref, o_ref, s_ref):
    d = pl.program_id(0)
    i = pl.program_id(2)
    C = WKV_CHUNK

    @pl.when(i == 0)
    def _():
        s_ref[...] = jnp.zeros_like(s_ref)

    sgn = 1 - 2 * d
    rel = (lax.broadcasted_iota(I32, (C, C), 0) - lax.broadcasted_iota(I32, (C, C), 1)) * sgn
    incl = rel >= 0
    strict = rel > 0
    eye = (rel == 0).astype(F32)

    lw = lw_ref[0, 0]
    lw_hi, lw_lo = _split_bf16(lw)
    ci = _dot(incl.astype(BF16), lw_hi) + _dot(incl.astype(BF16), lw_lo)
    ce = ci - lw
    tot = jnp.sum(lw, axis=0, keepdims=True)
    m = 0.5 * tot
    r = r_ref[0]
    kk = kk_ref[0]
    bq = bq_ref[0, 0]
    kd = kd_ref[0, 0]
    e_ci = jnp.exp(ci - m)
    e_mci = jnp.exp(m - ci)
    e_m = jnp.exp(m)
    at = (-kk * jnp.exp(ce - m)).astype(BF16)
    rt = (r * e_ci).astype(BF16)
    bt = (bq * e_mci).astype(BF16)
    kt = (kd * e_mci).astype(BF16)
    a0 = -kk * jnp.exp(ce)
    r0 = r * (e_ci * e_m)
    e_end = jnp.exp(tot - ci)
    bh = bq * e_end
    kh = kd * e_end
    p_end = jnp.exp(tot)
    v = v_ref[0]

    for h in range(RWKV_HEADS):
        hs = slice(h * RWKV_HEAD, (h + 1) * RWKV_HEAD)
        S = s_ref[h]
        s_hi, s_lo = _split_bf16(S)
        v_hi, v_lo = _split_bf16(v[:, hs])
        left = jnp.concatenate([at[:, hs], rt[:, hs]], axis=0)
        right = jnp.concatenate([bt[:, hs], kt[:, hs]], axis=0)
        A = _dot_nt(left, right)
        n_ab = jnp.where(strict, A[:C, :C], 0.0)
        a_ak = jnp.where(strict, A[:C, C:], 0.0)
        a_rb = jnp.where(incl, A[C:, :C], 0.0)
        a_rk = jnp.where(incl, A[C:, C:], 0.0)

        def with_state(x):
            x_hi, x_lo = _split_bf16(x)
            return _dot_nt(x_hi, s_hi) + _dot_nt(x_lo, s_hi) + _dot_nt(x_hi, s_lo)

        def with_values(a, x_hi, x_lo):
            a_hi, a_lo = _split_bf16(a)
            return _dot(a_hi, x_hi) + _dot(a_lo, x_hi) + _dot(a_hi, x_lo)

        w_rhs = with_state(a0[:, hs]) + with_values(a_ak, v_hi, v_lo)
        inv = eye + n_ab
        pw = n_ab
        for _ in range(5):
            p_hi, p_lo = _split_bf16(pw)
            pw = with_values(pw, p_hi, p_lo)
            p_hi, p_lo = _split_bf16(pw)
            inv = inv + with_values(inv, p_hi, p_lo)
        w_hi, w_lo = _split_bf16(w_rhs)
        U = with_values(inv, w_hi, w_lo)
        u_hi, u_lo = _split_bf16(U)
        o_ref[0, 0, :, hs] = (with_state(r0[:, hs]) + with_values(a_rb, u_hi, u_lo)
                              + with_values(a_rk, v_hi, v_lo))
        bh_hi, bh_lo = _split_bf16(bh[:, hs])
        kh_hi, kh_lo = _split_bf16(kh[:, hs])
        s_ref[h] = (S * p_end[:, hs]
                    + _dot_tn(u_hi, bh_hi) + _dot_tn(u_lo, bh_hi) + _dot_tn(u_hi, bh_lo)
                    + _dot_tn(v_hi, kh_hi) + _dot_tn(v_lo, kh_hi) + _dot_tn(v_hi, kh_lo))


def _wkv(r, v, kk, lw, bq, kd):
    B = r.shape[0]
    C = WKV_CHUNK
    n_lat = T_L // C

    def chunk(d, i):
        fwd = jnp.where(i < N_CTX_CHUNKS, n_lat + i, i - N_CTX_CHUNKS)
        return jnp.where(d == 0, fwd, N_CHUNKS - 1 - i)

    shared = pl.BlockSpec((1, C, RWKV_WIDTH), lambda d, b, i: (b, chunk(d, i), 0))
    perdir = pl.BlockSpec((1, 1, C, RWKV_WIDTH), lambda d, b, i: (d, b, chunk(d, i), 0))
    return pl.pallas_call(
        _wkv_kernel,
        grid=(2, B, N_CHUNKS),
        in_specs=[shared, shared, shared, perdir, perdir, perdir],
        out_specs=perdir,
        out_shape=jax.ShapeDtypeStruct((2, B, NT, RWKV_WIDTH), F32),
        scratch_shapes=[pltpu.VMEM((RWKV_HEADS, RWKV_HEAD, RWKV_HEAD), F32)],
        compiler_params=_cp("parallel", "parallel", "arbitrary"),
        name="wkv7",
    )(r, v, kk, lw, bq, kd)


def _rwkv_finish_kernel(o_ref, r_ref, v_ref, g_ref, kd_ref, rk_ref, lg_ref, lb_ref, bd_ref, y_ref):
    bd = bd_ref[...]
    o = o_ref[0, 0] + o_ref[1, 0]
    mu = _headsum(o, bd) * (1.0 / RWKV_HEAD)
    xc = o - mu
    var = _headsum(xc * xc, bd) * (1.0 / RWKV_HEAD)
    on = xc * lax.rsqrt(var + LNX_EPS) * lg_ref[...] + lb_ref[...]
    bonus = _headsum(r_ref[0] * (kd_ref[0, 0] + kd_ref[1, 0]) * rk_ref[...], bd) * v_ref[0]
    y_ref[0] = ((on + bonus) * g_ref[0]).astype(y_ref.dtype)


def _rwkv_finish(o, r, v, g, kd, r_k, lnx_g, lnx_b, bd):
    B = r.shape[0]
    W = RWKV_WIDTH
    tok = pl.BlockSpec((1, TM, W), lambda b, t: (b, t, 0))
    tok2 = pl.BlockSpec((2, 1, TM, W), lambda b, t: (0, b, t, 0))
    row = pl.BlockSpec((1, W), lambda b, t: (0, 0))
    return pl.pallas_call(
        _rwkv_finish_kernel,
        grid=(B, N_TILES),
        in_specs=[tok2, tok, tok, tok, tok2, row, row, row, pl.BlockSpec((W, W), lambda b, t: (0, 0))],
        out_specs=tok,
        out_shape=jax.ShapeDtypeStruct((B, NT, W), BF16),
        compiler_params=_cp("parallel", "parallel"),
        name="rwkv_finish",
    )(o, r, v, g, kd, r_k.reshape(1, W), lnx_g.reshape(1, W), lnx_b.reshape(1, W), bd)


MLA_QW = MLA_HEADS * MLA_SLOT


def _mla_weights(w_uq, w_ukv, partner):
    dq = MLA_NOPE + MLA_ROPE
    wq = jnp.zeros((MLA_Q_RANK, 2 * MLA_QW), F32)
    wk = jnp.zeros((MLA_KV_RANK, MLA_QW), F32)
    wv = jnp.zeros((MLA_KV_RANK, MLA_HEADS * MLA_V), F32)
    place = np.zeros((128, MLA_QW), np.float32)
    for h in range(MLA_HEADS):
        qh = w_uq[:, h * dq:(h + 1) * dq]
        wq = wq.at[:, h * MLA_SLOT:h * MLA_SLOT + dq].set(qh)
        wq = wq.at[:, MLA_QW + h * MLA_SLOT + MLA_NOPE:MLA_QW + h * MLA_SLOT + dq].set(qh[:, MLA_NOPE + partner])
        kvh = w_ukv[:, h * (MLA_NOPE + MLA_V):(h + 1) * (MLA_NOPE + MLA_V)]
        wk = wk.at[:, h * MLA_SLOT:h * MLA_SLOT + MLA_NOPE].set(kvh[:, :MLA_NOPE])
        wv = wv.at[:, h * MLA_V:(h + 1) * MLA_V].set(kvh[:, MLA_NOPE:])
        place[np.arange(MLA_ROPE), h * MLA_SLOT + MLA_NOPE + np.arange(MLA_ROPE)] = 1.0
    return wq.astype(BF16), wk.astype(BF16), wv.astype(BF16), jnp.asarray(place).astype(BF16)


def _mla_proj_kernel(u_ref, qg_ref, kg_ref, wq_ref, wk_ref, wv_ref, pl_ref, cq_ref, sq_ref, ck_ref, sk_ref,
                     q_ref, k_ref, v_ref):
    u = u_ref[0]
    qn = _rmsnorm(u[:, :MLA_Q_RANK], qg_ref[...]).astype(BF16)
    kvn = _rmsnorm(u[:, MLA_Q_RANK:MLA_Q_RANK + MLA_KV_RANK], kg_ref[...]).astype(BF16)
    o = MLA_Q_RANK + MLA_KV_RANK
    kr = u[:, o:o + 128] * ck_ref[...] + u[:, o + 128:o + 256] * sk_ref[...]
    qq = _dot(qn, wq_ref[...])
    q_ref[0] = (qq[:, :MLA_QW] * cq_ref[...] + qq[:, MLA_QW:] * sq_ref[...]).astype(BF16)
    k_ref[0] = (_dot(kvn, wk_ref[...]) + _dot2(kr, pl_ref[...])).astype(BF16)
    v_ref[0] = _dot(kvn, wv_ref[...]).astype(BF16)


def _mla_proj(um, q_norm_g, kv_norm_g, weights, tables):
    B = um.shape[0]
    wq, wk, wv, place = weights
    cq, sq, ck, sk = tables
    tok = lambda n: pl.BlockSpec((1, TM, n), lambda b, t: (b, t, 0))
    tab = lambda n: pl.BlockSpec((TM, n), lambda b, t: (t, 0))
    full = lambda a: pl.BlockSpec(a.shape, lambda b, t: (0, 0))
    return pl.pallas_call(
        _mla_proj_kernel,
        grid=(B, N_TILES),
        in_specs=[tok(MLA_U), pl.BlockSpec((1, MLA_Q_RANK), lambda b, t: (0, 0)),
                  pl.BlockSpec((1, MLA_KV_RANK), lambda b, t: (0, 0)),
                  full(wq), full(wk), full(wv), full(place),
                  tab(MLA_QW), tab(MLA_QW), tab(MLA_SLOT), tab(MLA_SLOT)],
        out_specs=[tok(MLA_QW), tok(MLA_QW), tok(MLA_HEADS * MLA_V)],
        out_shape=[jax.ShapeDtypeStruct((B, NT, MLA_QW), BF16),
                   jax.ShapeDtypeStruct((B, NT, MLA_QW), BF16),
                   jax.ShapeDtypeStruct((B, NT, MLA_HEADS * MLA_V), BF16)],
        compiler_params=_cp("parallel", "parallel"),
        name="mla_proj",
    )(um, q_norm_g.reshape(1, -1), kv_norm_g.reshape(1, -1), wq, wk, wv, place, cq, sq, ck, sk)


def _mla_attn_kernel(q_ref, k_ref, v_ref, o_ref):
    t = pl.program_id(2)
    is_ctx = t == N_TILES - 1
    col = lax.broadcasted_iota(I32, (TM, NT), 1)
    valid = jnp.logical_not(is_ctx) | (col >= T_L)
    lo_lanes = lax.broadcasted_iota(I32, (TM, 128), 1) < MLA_V
    vall = v_ref[0]
    outs = []
    for hh in range(2):
        hs = slice(hh * MLA_SLOT, (hh + 1) * MLA_SLOT)
        s = jnp.where(valid, _dot_nt(q_ref[0, :, hs], k_ref[0, :, hs]), -jnp.inf)
        m = jnp.max(s, axis=-1, keepdims=True)
        e = jnp.exp(s - m)
        pr = e * (1.0 / jnp.sum(e, axis=-1, keepdims=True))
        outs.append(_dot(pr.astype(BF16), vall))
    o_ref[0] = jnp.where(lo_lanes, outs[0], outs[1]).astype(o_ref.dtype)


def _mla_attn(q, k, v):
    B = q.shape[0]
    return pl.pallas_call(
        _mla_attn_kernel,
        grid=(B, MLA_HEADS // 2, N_TILES),
        in_specs=[pl.BlockSpec((1, TM, 2 * MLA_SLOT), lambda b, p, t: (b, t, p)),
                  pl.BlockSpec((1, NT, 2 * MLA_SLOT), lambda b, p, t: (b, 0, p)),
                  pl.BlockSpec((1, NT, 2 * MLA_V), lambda b, p, t: (b, 0, p))],
        out_specs=pl.BlockSpec((1, TM, 2 * MLA_V), lambda b, p, t: (b, t, p)),
        out_shape=jax.ShapeDtypeStruct((B, NT, MLA_HEADS * MLA_V), BF16),
        compiler_params=_cp("parallel", "parallel", "parallel"),
        name="mla_attn",
    )(q, k, v)


def _final_norm_kernel(x_ref, g_ref, o_ref):
    o_ref[0] = _rmsnorm(x_ref[0], g_ref[...])


def _final_norm(x, g):
    B = x.shape[0]
    tok = pl.BlockSpec((1, TM, D), lambda b, t: (b, t, 0))
    return pl.pallas_call(
        _final_norm_kernel,
        grid=(B, N_LAT_TILES),
        in_specs=[tok, pl.BlockSpec((1, D), lambda b, t: (0, 0))],
        out_specs=tok,
        out_shape=jax.ShapeDtypeStruct((B, T_L, D), F32),
        compiler_params=_cp("parallel", "parallel"),
        name="final_norm",
    )(x, g.reshape(1, D))


def kernel(x, c, ctx, c_ctx, ada_w, ada_b, norm_mix_g, norm_ffn_g, router_w, exp_w_gate, exp_w_up, exp_w_down, ev_w_in, ev_w_out, ev_sink, ev_pool_w, ev_pool_scale, od_w_in, od_w_out, od_mu_prev, od_mu_next, od_w0, od_w2, od_a0, od_a2, od_g2, od_k_k, od_k_a, od_r_k, od_lnx_g, od_lnx_b, od_q_norm_g, od_w_uq, od_kv_norm_g, od_w_ukv, final_norm_g):
    depth = ada_w.shape[0]
    assert x.shape[1:] == (T_L, D) and ctx.shape[1:] == (T_C, D)
    mods = _modvec(c, c_ctx, ada_w, ada_b)
    ev_tables, ev_partner = _even_tables()
    mla_tables, mla_partner = _mla_tables()
    bd = _head_ones()
    xs = jnp.concatenate([x, ctx], axis=1)
    for l in range(depth):
        need_ctx = l < depth - 1
        i = l // 2
        mod = mods[l]
        if l % 2 == 0:
            q, kd, vd, pu = _inproj_even(xs, mod, norm_mix_g[l], _even_weights(ev_w_in[i], ev_partner), ev_tables)
            mix_a = _attn_even(q, kd, vd, ev_sink[i])
            mix_b = _pool(pu, ev_pool_w[i], ev_pool_scale[i])
            w_out = ev_w_out[i]
        else:
            ur, um = _inproj_odd(xs, mod, norm_mix_g[l], _odd_weights(od_w_in[i], mla_partner))
            r, v, kk, g, lw, bq, kdir = _rwkv_prep(ur, od_mu_prev[i], od_mu_next[i], od_w0[i], od_w2[i],
                                                   od_a0[i], od_a2[i], od_g2[i], od_k_k[i], od_k_a[i], bd)
            o = _wkv(r, v, kk, lw, bq, kdir)
            mix_a = _rwkv_finish(o, r, v, g, kdir, od_r_k[i], od_lnx_g[i], od_lnx_b[i], bd)
            mq, mk, mv = _mla_proj(um, od_q_norm_g[i], od_kv_norm_g[i],
                                   _mla_weights(od_w_uq[i], od_w_ukv[i], mla_partner), mla_tables)
            mix_b = _mla_attn(mq, mk, mv)
            w_out = od_w_out[i]
        xs, h, logits = _outproj(mix_a, mix_b, w_out, xs, mod, norm_ffn_g[l], router_w[l])
        xs = _moe(xs, h, logits, mod, exp_w_gate[l], exp_w_up[l], exp_w_down[l], need_ctx)
    return _final_norm(xs, final_norm_g)
```

```python
import functools

import numpy as np
import jax
import jax.numpy as jnp
from jax import lax
from jax.experimental import pallas as pl
from jax.experimental.pallas import tpu as pltpu

F32 = jnp.float32
BF16 = jnp.bfloat16
I32 = jnp.int32
HIGHEST = lax.Precision.HIGHEST

D = 1024
T_L = 2048
T_C = 256
NT = T_L + T_C
GRID_W = 64
EPS = 1e-6
ROPE_BASE = 10000.0

TM = 256
N_TILES = NT // TM
N_LAT_TILES = T_L // TM

A_HEADS = 8
A_KV_HEADS = 2
HEAD_DIM = 64
A_WINDOW = 128
A_BLOCK = 128
N_ABLK = NT // A_BLOCK
N_LAT_ABLK = T_L // A_BLOCK

POOL_WINDOWS = (2, 4, 8, 16)
POOL_GROUP = 128

RWKV_HEADS = 8
RWKV_HEAD = 64
RWKV_WIDTH = 512
RWKV_IN = 1920
LNX_EPS = 64e-5
WKV_CHUNK = 64
N_CHUNKS = NT // WKV_CHUNK
N_CTX_CHUNKS = T_C // WKV_CHUNK

MLA_HEADS = 8
MLA_Q_RANK = 256
MLA_KV_RANK = 128
MLA_NOPE = 64
MLA_ROPE = 32
MLA_V = 64
MLA_SLOT = 128
MLA_U = 640

N_EXPERTS = 16
EXPERT_FF = 2048
CAP_L = 2 * T_L // N_EXPERTS
CAP_C = 2 * T_C // N_EXPERTS
FF_CHUNK = 256

VMEM_LIMIT_BYTES = 56 * 1024 * 1024


def _cp(*sem):
    return pltpu.CompilerParams(dimension_semantics=sem, vmem_limit_bytes=VMEM_LIMIT_BYTES)


def _split_bf16(x):
    hi = x.astype(BF16)
    lo = (x - hi.astype(F32)).astype(BF16)
    return hi, lo


def _dot(a, b):
    return jnp.dot(a, b, preferred_element_type=F32)


def _dot_nt(a, b):
    return lax.dot_general(a, b, (((1,), (1,)), ((), ())), preferred_element_type=F32)


def _dot_tn(a, b):
    return lax.dot_general(a, b, (((0,), (0,)), ((), ())), preferred_element_type=F32)


def _dot2(a_f32, b_bf16):
    hi, lo = _split_bf16(a_f32)
    return _dot(hi, b_bf16) + _dot(lo, b_bf16)


def _modulate(x, g, shift, scale):
    ms = jnp.mean(x * x, axis=-1, keepdims=True)
    return x * lax.rsqrt(ms + EPS) * g * (1.0 + scale) + shift


def _rmsnorm(x, g):
    ms = jnp.mean(x * x, axis=-1, keepdims=True)
    return x * lax.rsqrt(ms + EPS) * g


def _modvec_kernel(a_ref, w_ref, b_ref, o_ref):
    a = a_ref[...]
    a = a * jax.nn.sigmoid(a)
    o_ref[0] = jnp.dot(a, w_ref[0], precision=HIGHEST, preferred_element_type=F32) + b_ref[0]


def _modvec(c, c_ctx, ada_w, ada_b):
    L = ada_w.shape[0]
    B = c.shape[0]
    assert B < 16
    rows = jnp.zeros((16, D), F32).at[:B].set(c).at[B].set(c_ctx)
    tn = 512
    out = pl.pallas_call(
        _modvec_kernel,
        grid=(L, 6 * D // tn),
        in_specs=[pl.BlockSpec((16, D), lambda l, n: (0, 0)),
                  pl.BlockSpec((1, D, tn), lambda l, n: (l, 0, n)),
                  pl.BlockSpec((1, 1, tn), lambda l, n: (l, 0, n))],
        out_specs=pl.BlockSpec((1, 16, tn), lambda l, n: (l, 0, n)),
        out_shape=jax.ShapeDtypeStruct((L, 16, 6 * D), F32),
        compiler_params=_cp("parallel", "parallel"),
        name="modvec",
    )(rows, ada_w, ada_b.reshape(L, 1, 6 * D))
    m = out.reshape(L, 16, 6, D)
    ml = m[:, :B]
    mc = jnp.broadcast_to(m[:, B:B + 1], (L, B, 6, D))
    mod = jnp.stack([ml, mc], axis=2)
    mod = jnp.pad(mod, ((0, 0), (0, 0), (0, 0), (0, 2), (0, 0)))
    return mod.reshape(L, B * 2, 8, D)


def _mod_spec():
    return pl.BlockSpec((1, 8, D), lambda b, t: (b * 2 + t // N_LAT_TILES, 0, 0))


def _rope_block(width, pos_row, pos_col):
    d = width // 2
    half = d // 2
    inv = ROPE_BASE ** (-np.arange(half, dtype=np.float32) / half)
    cos = np.zeros((pos_row.shape[0], width), np.float32)
    sin = np.zeros_like(cos)
    partner = np.zeros((width,), np.int64)
    for part, pos in ((0, pos_row), (1, pos_col)):
        ang = pos.astype(np.float32)[:, None] * inv[None, :]
        c, s = np.cos(ang), np.sin(ang)
        o = part * d
        cos[:, o:o + half] = c
        cos[:, o + half:o + d] = c
        sin[:, o:o + half] = -s
        sin[:, o + half:o + d] = s
        partner[o:o + half] = np.arange(o + half, o + d)
        partner[o + half:o + d] = np.arange(o, o + half)
    return cos, sin, partner


def _latent_positions():
    t = np.arange(T_L)
    return t // GRID_W, t % GRID_W


def _even_tables():
    cos, sin, partner = _rope_block(HEAD_DIM, *_latent_positions())
    scale = HEAD_DIM ** -0.5
    ones = np.ones((T_C, HEAD_DIM), np.float32)
    zeros = np.zeros((T_C, HEAD_DIM), np.float32)
    cos = np.concatenate([cos, ones], axis=0)
    sin = np.concatenate([sin, zeros], axis=0)
    cq = np.tile(cos, (1, A_HEADS)) * scale
    sq = np.tile(sin, (1, A_HEADS)) * scale
    ck = np.tile(cos, (1, 2 * A_KV_HEADS))
    sk = np.tile(sin, (1, 2 * A_KV_HEADS))
    return (jnp.asarray(cq), jnp.asarray(sq), jnp.asarray(ck), jnp.asarray(sk)), partner


def _mla_tables():
    cos, sin, partner = _rope_block(MLA_ROPE, *_latent_positions())
    cos = np.concatenate([cos, np.ones((T_C, MLA_ROPE), np.float32)], axis=0)
    sin = np.concatenate([sin, np.zeros((T_C, MLA_ROPE), np.float32)], axis=0)
    scale = (MLA_NOPE + MLA_ROPE) ** -0.5
    cq1 = np.zeros((NT, MLA_SLOT), np.float32)
    sq1 = np.zeros((NT, MLA_SLOT), np.float32)
    cq1[:, :MLA_NOPE] = 1.0
    cq1[:, MLA_NOPE:MLA_NOPE + MLA_ROPE] = cos
    sq1[:, MLA_NOPE:MLA_NOPE + MLA_ROPE] = sin
    cq = np.tile(cq1, (1, MLA_HEADS)) * scale
    sq = np.tile(sq1, (1, MLA_HEADS)) * scale
    ck = np.zeros((NT, MLA_SLOT), np.float32)
    sk = np.zeros((NT, MLA_SLOT), np.float32)
    ck[:, :MLA_ROPE] = cos
    sk[:, :MLA_ROPE] = sin
    return (jnp.asarray(cq), jnp.asarray(sq), jnp.asarray(ck), jnp.asarray(sk)), partner


EV_Q, EV_K, EV_V, EV_P = 512, 256, 256, 512
EV_N = 2 * EV_Q + 2 * EV_K + EV_V + EV_P


def _even_weights(w_in, partner):
    q = w_in[:, :512]
    k = w_in[:, 512:640]
    v = w_in[:, 640:768]
    p = w_in[:, 768:1280]
    pq = np.concatenate([h * HEAD_DIM + partner for h in range(A_HEADS)])
    pk = np.concatenate([h * HEAD_DIM + partner for h in range(A_KV_HEADS)])
    dup = np.concatenate([np.arange(HEAD_DIM), np.arange(HEAD_DIM),
                          HEAD_DIM + np.arange(HEAD_DIM), HEAD_DIM + np.arange(HEAD_DIM)])
    w = jnp.concatenate([q, q[:, pq], k[:, dup], k[:, pk][:, dup], v[:, dup], p], axis=1)
    return w.astype(BF16)


def _inproj_even_kernel(x_ref, mod_ref, g_ref, w_ref, cq_ref, sq_ref, ck_ref, sk_ref,
                        q_ref, k_ref, v_ref, p_ref):
    mod = mod_ref[0]
    h = _modulate(x_ref[0], g_ref[...], mod[0:1], mod[1:2])
    y = _dot(h.astype(BF16), w_ref[...])
    o = 0
    q = y[:, o:o + EV_Q] * cq_ref[...] + y[:, o + EV_Q:o + 2 * EV_Q] * sq_ref[...]
    o += 2 * EV_Q
    k = y[:, o:o + EV_K] * ck_ref[...] + y[:, o + EV_K:o + 2 * EV_K] * sk_ref[...]
    o += 2 * EV_K
    q_ref[0] = q.astype(BF16)
    k_ref[0] = k.astype(BF16)
    v_ref[0] = y[:, o:o + EV_V].astype(BF16)
    p_ref[0] = y[:, o + EV_V:o + EV_V + EV_P]


def _inproj_even(x, mod, g, w, tables):
    B = x.shape[0]
    cq, sq, ck, sk = tables
    tok = lambda n: pl.BlockSpec((1, TM, n), lambda b, t: (b, t, 0))
    tab = lambda n: pl.BlockSpec((TM, n), lambda b, t: (t, 0))
    return pl.pallas_call(
        _inproj_even_kernel,
        grid=(B, N_TILES),
        in_specs=[tok(D), _mod_spec(), pl.BlockSpec((1, D), lambda b, t: (0, 0)),
                  pl.BlockSpec((D, EV_N), lambda b, t: (0, 0)),
                  tab(EV_Q), tab(EV_Q), tab(EV_K), tab(EV_K)],
        out_specs=[tok(EV_Q), tok(EV_K), tok(EV_V), tok(EV_P)],
        out_shape=[jax.ShapeDtypeStruct((B, NT, EV_Q), BF16),
                   jax.ShapeDtypeStruct((B, NT, EV_K), BF16),
                   jax.ShapeDtypeStruct((B, NT, EV_V), BF16),
                   jax.ShapeDtypeStruct((B, NT, EV_P), F32)],
        compiler_params=_cp("parallel", "parallel"),
        name="inproj_even",
    )(x, mod, g.reshape(1, D), w, cq, sq, ck, sk)


def _attn_even_kernel(sink_ref, q_ref, kp_ref, kc_ref, kn_ref, kx_ref,
                      vp_ref, vc_ref, vn_ref, vx_ref, o_ref):
    j = pl.program_id(1)
    is_lat = j < N_LAT_ABLK
    q = q_ref[0]
    qpos = j * A_BLOCK + lax.broadcasted_iota(I32, (A_BLOCK, 3 * A_BLOCK), 0)
    kpos = (j - 1) * A_BLOCK + lax.broadcasted_iota(I32, (A_BLOCK, 3 * A_BLOCK), 1)
    valid = (jnp.abs(kpos - qpos) <= A_WINDOW) & (kpos >= 0) & (kpos < T_L) & is_lat
    lo_lanes = lax.broadcasted_iota(I32, (A_BLOCK, 128), 1) < HEAD_DIM
    for p in range(A_HEADS // 2):
        kv = p // 2
        ks = slice(kv * 128, (kv + 1) * 128)
        kall = jnp.concatenate([kx_ref[0, :, ks], kp_ref[0, :, ks], kc_ref[0, :, ks], kn_ref[0, :, ks]], axis=0)
        vall = jnp.concatenate([vx_ref[0, :, ks], vp_ref[0, :, ks], vc_ref[0, :, ks], vn_ref[0, :, ks]], axis=0)
        qp = q[:, p * 128:(p + 1) * 128]
        outs = []
        for hh in range(2):
            snk = sink_ref[2 * p + hh]
            qm = jnp.where(lo_lanes if hh == 0 else jnp.logical_not(lo_lanes), qp, jnp.zeros_like(qp))
            s = _dot_nt(qm, kall)
            s_ctx = s[:, :T_C]
            s_loc = jnp.where(valid, s[:, T_C:], -jnp.inf)
            m = jnp.maximum(jnp.max(s_ctx, axis=-1, keepdims=True), jnp.max(s_loc, axis=-1, keepdims=True))
            m = jnp.maximum(m, snk)
            e_ctx = jnp.exp(s_ctx - m)
            e_loc = jnp.exp(s_loc - m)
            den = (jnp.sum(e_ctx, axis=-1, keepdims=True) + jnp.sum(e_loc, axis=-1, keepdims=True)
                   + jnp.exp(snk - m))
            pr = jnp.concatenate([e_ctx, e_loc], axis=1) * (1.0 / den)
            outs.append(_dot(pr.astype(BF16), vall))
        o_ref[0, :, p * 128:(p + 1) * 128] = jnp.where(lo_lanes, outs[0], outs[1]).astype(o_ref.dtype)


def _attn_even(q, kd, vd, sink):
    B = q.shape[0]
    last = N_LAT_ABLK - 1
    blk = lambda f: pl.BlockSpec((1, A_BLOCK, 256), f)
    prev = lambda b, j: (b, jnp.clip(j - 1, 0, last), 0)
    cur = lambda b, j: (b, j, 0)
    nxt = lambda b, j: (b, jnp.clip(j + 1, 0, last), 0)
    ctx = pl.BlockSpec((1, T_C, 256), lambda b, j: (b, T_L // T_C, 0))
    return pl.pallas_call(
        _attn_even_kernel,
        grid=(B, N_ABLK),
        in_specs=[pl.BlockSpec(memory_space=pltpu.SMEM),
                  pl.BlockSpec((1, A_BLOCK, EV_Q), cur),
                  blk(prev), blk(cur), blk(nxt), ctx,
                  blk(prev), blk(cur), blk(nxt), ctx],
        out_specs=pl.BlockSpec((1, A_BLOCK, 512), cur),
        out_shape=jax.ShapeDtypeStruct((B, NT, 512), BF16),
        compiler_params=_cp("parallel", "parallel"),
        name="attn_even",
    )(sink, q, kd, kd, kd, kd, vd, vd, vd, vd)


def _pool_kernel(up_ref, uc_ref, un_ref, w_ref, sc_ref, o_ref):
    j = pl.program_id(1)
    is_lat = j < N_LAT_ABLK
    seg_lo = jnp.where(is_lat, 0, T_L)
    seg_hi = jnp.where(is_lat, T_L, NT)
    u3 = jnp.concatenate([up_ref[0], uc_ref[0], un_ref[0]], axis=0)
    uc = uc_ref[0]
    r = j * A_BLOCK + lax.broadcasted_iota(I32, (A_BLOCK, 3 * A_BLOCK), 0)
    c = (j - 1) * A_BLOCK + lax.broadcasted_iota(I32, (A_BLOCK, 3 * A_BLOCK), 1)
    r1 = j * A_BLOCK + lax.broadcasted_iota(I32, (A_BLOCK, 1), 0)
    outs = []
    for gi, w in enumerate(POOL_WINDOWS):
        cs = slice(gi * POOL_GROUP, (gi + 1) * POOL_GROUP)
        lo = jnp.maximum(r - w // 2, seg_lo)
        hi = jnp.minimum(r + w // 2, seg_hi)
        band = ((c >= lo) & (c < hi)).astype(BF16)
        cnt = (jnp.minimum(r1 + w // 2, seg_hi) - jnp.maximum(r1 - w // 2, seg_lo)).astype(F32)
        hi_u, lo_u = _split_bf16(u3[:, cs])
        mean = (_dot(band, hi_u) + _dot(band, lo_u)) / cnt
        outs.append(_dot((mean - uc[:, cs]).astype(BF16), w_ref[gi]))
    o_ref[0] = (jnp.concatenate(outs, axis=1) * sc_ref[...]).astype(o_ref.dtype)


def _pool(u, pool_w, pool_scale):
    B = u.shape[0]
    blk = lambda f: pl.BlockSpec((1, A_BLOCK, 512), f)
    return pl.pallas_call(
        _pool_kernel,
        grid=(B, N_ABLK),
        in_specs=[blk(lambda b, j: (b, jnp.maximum(j - 1, 0), 0)),
                  blk(lambda b, j: (b, j, 0)),
                  blk(lambda b, j: (b, jnp.minimum(j + 1, N_ABLK - 1), 0)),
                  pl.BlockSpec((4, POOL_GROUP, POOL_GROUP), lambda b, j: (0, 0, 0)),
                  pl.BlockSpec((1, 512), lambda b, j: (0, 0))],
        out_specs=blk(lambda b, j: (b, j, 0)),
        out_shape=jax.ShapeDtypeStruct((B, NT, 512), BF16),
        compiler_params=_cp("parallel", "parallel"),
        name="pool",
    )(u, u, u, pool_w.astype(BF16), pool_scale.reshape(1, 512))


def _outproj_kernel(a_ref, p_ref, w_ref, x_ref, mod_ref, g_ref, wr_ref, xo_ref, h_ref, lg_ref):
    mod = mod_ref[0]
    y = _dot(a_ref[0], w_ref[0:512, :]) + _dot(p_ref[0], w_ref[512:1024, :])
    x = x_ref[0] + mod[2:3] * y
    xo_ref[0] = x
    h = _modulate(x, g_ref[...], mod[3:4], mod[4:5])
    h_ref[0] = h.astype(BF16)
    lg_ref[0] = lax.dot_general(wr_ref[...], h, (((1,), (1,)), ((), ())),
                                precision=HIGHEST, preferred_element_type=F32)


def _outproj(a, p, w_out, x, mod, g_ffn, w_router):
    B = x.shape[0]
    tok = lambda n: pl.BlockSpec((1, TM, n), lambda b, t: (b, t, 0))
    return pl.pallas_call(
        _outproj_kernel,
        grid=(B, N_TILES),
        in_specs=[tok(512), tok(512), pl.BlockSpec((D, D), lambda b, t: (0, 0)), tok(D), _mod_spec(),
                  pl.BlockSpec((1, D), lambda b, t: (0, 0)),
                  pl.BlockSpec((N_EXPERTS, D), lambda b, t: (0, 0))],
        out_specs=[tok(D), tok(D), pl.BlockSpec((1, N_EXPERTS, TM), lambda b, t: (b, 0, t))],
        out_shape=[jax.ShapeDtypeStruct((B, NT, D), F32),
                   jax.ShapeDtypeStruct((B, NT, D), BF16),
                   jax.ShapeDtypeStruct((B, N_EXPERTS, NT), F32)],
        compiler_params=_cp("parallel", "parallel"),
        name="outproj",
    )(a, p, w_out.astype(BF16), x, mod, g_ffn.reshape(1, D), w_router.T)


def _topk_kernel(lg_ref, sid_ref, gate_ref, *, cap):
    B, E, T = lg_ref.shape
    R = B * E
    lg = lg_ref[...]
    m = jnp.max(lg, axis=1, keepdims=True)
    e = jnp.exp(lg - m)
    aff = (e / jnp.sum(e, axis=1, keepdims=True)).reshape(R, T)
    bits = lax.bitcast_convert_type(aff, I32)

    def step(i, thr):
        cand = thr | jnp.left_shift(jnp.int32(1), 30 - i)
        cnt = jnp.sum((bits >= cand).astype(F32), axis=1, keepdims=True)
        return jnp.where(cnt >= cap, cand, thr)

    thr = lax.fori_loop(0, 31, step, jnp.zeros((R, 1), I32))
    gt = bits > thr
    eq = bits == thr
    need = cap - jnp.sum(gt.astype(F32), axis=1, keepdims=True)

    blk = 256
    tri = (lax.broadcasted_iota(I32, (blk, blk), 0) <= lax.broadcasted_iota(I32, (blk, blk), 1)).astype(BF16)

    def prefix(mask_f32):
        off = jnp.zeros((R, 1), F32)
        parts = []
        for k in range(T // blk):
            cb = _dot(mask_f32[:, k * blk:(k + 1) * blk].astype(BF16), tri) + off
            parts.append(cb)
            off = cb[:, blk - 1:blk]
        return parts

    eq_parts = prefix(eq.astype(F32))
    sel_parts = []
    for k in range(T // blk):
        sl = slice(k * blk, (k + 1) * blk)
        sel_parts.append(gt[:, sl] | (eq[:, sl] & (eq_parts[k] <= need)))
    sel = jnp.concatenate(sel_parts, axis=1)
    pos_parts = prefix(sel.astype(F32))
    for k in range(T // blk):
        sl = slice(k * blk, (k + 1) * blk)
        sid = jnp.where(sel_parts[k], pos_parts[k].astype(I32) - 1, -1)
        sid_ref[:, :, sl] = sid.reshape(B, E, blk)
        gate_ref[:, :, sl] = jnp.where(sel_parts[k], aff[:, sl], 0.0).reshape(B, E, blk)


def _topk(logits, cap):
    B, E, T = logits.shape
    full = pl.BlockSpec((B, E, T), lambda i: (0, 0, 0))
    return pl.pallas_call(
        functools.partial(_topk_kernel, cap=cap),
        grid=(1,),
        in_specs=[full],
        out_specs=[full, full],
        out_shape=[jax.ShapeDtypeStruct((B, E, T), I32), jax.ShapeDtypeStruct((B, E, T), F32)],
        compiler_params=_cp("arbitrary"),
        name=f"topk_{T}",
    )(logits)


def _gather_kernel(h_ref, sid_ref, gate_ref, xs_ref, gs_ref, *, cap, chunk):
    T = h_ref.shape[1]
    acc = jnp.zeros((cap, D), F32)
    gacc = jnp.zeros((cap, 1), F32)
    slot = lax.broadcasted_iota(I32, (cap, chunk), 0)
    for k in range(T // chunk):
        sl = slice(k * chunk, (k + 1) * chunk)
        hit = slot == sid_ref[0, 0, :, sl]
        acc = acc + _dot(hit.astype(BF16), h_ref[0, sl, :])
        gacc = gacc + jnp.sum(jnp.where(hit, gate_ref[0, 0, :, sl], 0.0), axis=1, keepdims=True)
    xs_ref[0] = acc.astype(BF16)
    gs_ref[0] = jnp.broadcast_to(gacc, (cap, 128))


def _gather(h, sid, gate, cap, row_block):
    B, E, T = sid.shape
    chunk = min(T, 512)
    idx = lambda b, e: (b, e, 0, 0)
    return pl.pallas_call(
        functools.partial(_gather_kernel, cap=cap, chunk=chunk),
        grid=(B, E),
        in_specs=[pl.BlockSpec((1, T, D), lambda b, e: (b, row_block, 0)),
                  pl.BlockSpec((1, 1, 1, T), idx), pl.BlockSpec((1, 1, 1, T), idx)],
        out_specs=[pl.BlockSpec((1, cap, D), lambda b, e: (e, b, 0)),
                   pl.BlockSpec((1, cap, 128), lambda b, e: (e, b, 0))],
        out_shape=[jax.ShapeDtypeStruct((E, B * cap, D), BF16),
                   jax.ShapeDtypeStruct((E, B * cap, 128), F32)],
        compiler_params=_cp("parallel", "arbitrary"),
        name=f"gather_{T}",
    )(h, sid.reshape(B, E, 1, T), gate.reshape(B, E, 1, T))


def _ffn_kernel(*refs, row_tiles):
    n_seg = len(row_tiles)
    xs_refs = refs[:n_seg]
    gs_refs = refs[n_seg:2 * n_seg]
    wg_ref, wu_ref, wd_ref = refs[2 * n_seg:2 * n_seg + 3]
    y_refs = refs[2 * n_seg + 3:3 * n_seg + 3]
    acc_refs = refs[3 * n_seg + 3:]
    f = pl.program_id(1)
    wg = wg_ref[0].astype(BF16)
    wu = wu_ref[0].astype(BF16)
    wd = wd_ref[0].astype(BF16)
    for xs_ref, gs_ref, y_ref, acc_ref, rt in zip(xs_refs, gs_refs, y_refs, acc_refs, row_tiles):
        M = xs_ref.shape[1]
        for r0 in range(0, M, rt):
            rows = pl.ds(r0, rt)
            x = xs_ref[0, rows, :]
            g = _dot(x, wg)
            u = _dot(x, wu)
            hid = (g * jax.nn.sigmoid(g) * u).astype(BF16)
            contrib = _dot(hid, wd)

            @pl.when(f == 0)
            def _():
                acc_ref[rows, :] = contrib

            @pl.when(f > 0)
            def _():
                acc_ref[rows, :] += contrib

        @pl.when(f == pl.num_programs(1) - 1)
        def _():
            gs = gs_ref[0]
            for k in range(D // 128):
                cs = slice(k * 128, (k + 1) * 128)
                y_ref[0, :, cs] = (acc_ref[:, cs] * gs).astype(y_ref.dtype)


def _ffn(xs_list, gs_list, w_gate, w_up, w_down, layer):
    E = w_gate.shape[1]
    row_tiles = tuple(min(xs.shape[1], 512) for xs in xs_list)
    nf = EXPERT_FF // FF_CHUNK
    in_specs = [pl.BlockSpec((1, xs.shape[1], D), lambda e, f: (e, 0, 0)) for xs in xs_list]
    in_specs += [pl.BlockSpec((1, gs.shape[1], 128), lambda e, f: (e, 0, 0)) for gs in gs_list]
    in_specs += [pl.BlockSpec((None, 1, D, FF_CHUNK), lambda e, f: (layer, e, 0, f)),
                 pl.BlockSpec((None, 1, D, FF_CHUNK), lambda e, f: (layer, e, 0, f)),
                 pl.BlockSpec((None, 1, FF_CHUNK, D), lambda e, f: (layer, e, f, 0))]
    return pl.pallas_call(
        functools.partial(_ffn_kernel, row_tiles=row_tiles),
        grid=(E, nf),
        in_specs=in_specs,
        out_specs=[pl.BlockSpec((1, xs.shape[1], D), lambda e, f: (e, 0, 0)) for xs in xs_list],
        out_shape=[jax.ShapeDtypeStruct(xs.shape, BF16) for xs in xs_list],
        scratch_shapes=[pltpu.VMEM((xs.shape[1], D), F32) for xs in xs_list],
        compiler_params=_cp("parallel", "arbitrary"),
        name="expert_ffn",
    )(*xs_list, *gs_list, w_gate, w_up, w_down)


def _combine_kernel(x_ref, sid_ref, y_ref, mod_ref, o_ref, acc_ref, *, cap):
    e = pl.program_id(2)
    tbs = x_ref.shape[1]
    hit = lax.broadcasted_iota(I32, (cap, tbs), 0) == sid_ref[0, 0]
    z = _dot_tn(hit.astype(BF16), y_ref[0])

    @pl.when(e == 0)
    def _():
        acc_ref[...] = z

    @pl.when(e > 0)
    def _():
        acc_ref[...] += z

    @pl.when(e == pl.num_programs(2) - 1)
    def _():
        o_ref[0] = x_ref[0] + mod_ref[0][5:6] * acc_ref[...]


def _combine(x, sid, y, mod, cap, tbs, row_block0, segment):
    B, E, T = sid.shape
    ntb = T // tbs
    xmap = lambda b, tb, e: (b, row_block0 + tb, 0)
    return pl.pallas_call(
        functools.partial(_combine_kernel, cap=cap),
        grid=(B, ntb, E),
        in_specs=[pl.BlockSpec((1, tbs, D), xmap),
                  pl.BlockSpec((1, 1, 1, tbs), lambda b, tb, e: (b, e, 0, tb)),
                  pl.BlockSpec((1, cap, D), lambda b, tb, e: (e, b, 0)),
                  pl.BlockSpec((1, 8, D), lambda b, tb, e: (b * 2 + segment, 0, 0))],
        out_specs=pl.BlockSpec((1, tbs, D), xmap),
        out_shape=jax.ShapeDtypeStruct(x.shape, F32),
        scratch_shapes=[pltpu.VMEM((tbs, D), F32)],
        input_output_aliases={0: 0},
        compiler_params=_cp("parallel", "parallel", "arbitrary"),
        name=f"combine_{T}",
    )(x, sid.reshape(B, E, 1, T), y, mod)


def _moe(x, h, logits, mod, w_gate, w_up, w_down, layer, need_ctx):
    sid_l, gate_l = _topk(logits[:, :, :T_L], CAP_L)
    xs_l, gs_l = _gather(h, sid_l, gate_l, CAP_L, 0)
    if need_ctx:
        sid_c, gate_c = _topk(logits[:, :, T_L:], CAP_C)
        xs_c, gs_c = _gather(h, sid_c, gate_c, CAP_C, T_L // T_C)
        y_l, y_c = _ffn([xs_l, xs_c], [gs_l, gs_c], w_gate, w_up, w_down, layer)
    else:
        (y_l,) = _ffn([xs_l], [gs_l], w_gate, w_up, w_down, layer)
    x = _combine(x, sid_l, y_l, mod, CAP_L, 1024, 0, 0)
    if need_ctx:
        x = _combine(x, sid_c, y_c, mod, CAP_C, T_C, T_L // T_C, 1)
    return x


OD_N = RWKV_IN + MLA_U


def _odd_weights(w_in, partner):
    rw = w_in[:, :RWKV_IN]
    qd = w_in[:, RWKV_IN:RWKV_IN + MLA_Q_RANK]
    kvd = w_in[:, RWKV_IN + MLA_Q_RANK:RWKV_IN + MLA_Q_RANK + MLA_KV_RANK]
    kr = w_in[:, RWKV_IN + MLA_Q_RANK + MLA_KV_RANK:]
    pad = jnp.zeros((D, 128 - MLA_ROPE), w_in.dtype)
    w = jnp.concatenate([rw, qd, kvd, kr, pad, kr[:, partner], pad], axis=1)
    return w.astype(BF16)


def _inproj_odd_kernel(x_ref, mod_ref, g_ref, w_ref, ur_ref, um_ref):
    mod = mod_ref[0]
    h = _modulate(x_ref[0], g_ref[...], mod[0:1], mod[1:2])
    y = _dot(h.astype(BF16), w_ref[...])
    ur_ref[0] = y[:, :RWKV_IN]
    um_ref[0] = y[:, RWKV_IN:]


def _inproj_odd(x, mod, g, w):
    B = x.shape[0]
    tok = lambda n: pl.BlockSpec((1, TM, n), lambda b, t: (b, t, 0))
    return pl.pallas_call(
        _inproj_odd_kernel,
        grid=(B, N_TILES),
        in_specs=[tok(D), _mod_spec(), pl.BlockSpec((1, D), lambda b, t: (0, 0)),
                  pl.BlockSpec((D, OD_N), lambda b, t: (0, 0))],
        out_specs=[tok(RWKV_IN), tok(MLA_U)],
        out_shape=[jax.ShapeDtypeStruct((B, NT, RWKV_IN), F32),
                   jax.ShapeDtypeStruct((B, NT, MLA_U), F32)],
        compiler_params=_cp("parallel", "parallel"),
        name="inproj_odd",
    )(x, mod, g.reshape(1, D), w)


def _head_ones():
    h = np.arange(RWKV_WIDTH) // RWKV_HEAD
    return jnp.asarray((h[:, None] == h[None, :]).astype(np.float32)).astype(BF16)


def _headsum(x, bd):
    return _dot2(x, bd)


def _softplus(z):
    return jnp.maximum(z, 0.0) + jnp.log(1.0 + jnp.exp(-jnp.abs(z)))


def _rwkv_prep_kernel(u_ref, up_ref, un_ref, mup_ref, mun_ref, w0_ref, w2_ref, a0_ref, a2_ref, g2_ref,
                      kk_ref, ka_ref, bd_ref,
                      r_ref, v_ref, kkn_ref, g_ref, lw_ref, bq_ref, kd_ref):
    t = pl.program_id(1)
    u = u_ref[0]
    rows = lax.broadcasted_iota(I32, (TM, 1), 0)
    seg_start = (t == 0) | (t == N_LAT_TILES)
    seg_end = (t == N_LAT_TILES - 1) | (t == N_TILES - 1)
    prev_row = jnp.where(seg_start, 0.0, up_ref[0, 7:8, :])
    next_row = jnp.where(seg_end, 0.0, un_ref[0, 0:1, :])
    prev = jnp.where(rows == 0, prev_row, pltpu.roll(u, 1, 0))
    nxt = jnp.where(rows == TM - 1, next_row, pltpu.roll(u, TM - 1, 0))
    us = u + mup_ref[...] * (prev - u) + mun_ref[...] * (nxt - u)
    W = RWKV_WIDTH
    r = us[:, :W]
    k = us[:, W:2 * W]
    v = us[:, 2 * W:3 * W]
    wl = jnp.tanh(us[:, 3 * W:3 * W + 128]).astype(BF16)
    al = us[:, 3 * W + 128:3 * W + 256].astype(BF16)
    gl = jax.nn.sigmoid(us[:, 3 * W + 256:3 * W + 384]).astype(BF16)
    bd = bd_ref[...]
    kk = k * kk_ref[...]
    kk = kk * lax.rsqrt(jnp.maximum(_headsum(kk * kk, bd), 1e-24))
    r_ref[0] = r
    v_ref[0] = v
    kkn_ref[0] = kk
    g_ref[0] = _dot(gl, g2_ref[...])
    for d in range(2):
        wlog = -_softplus(-(w0_ref[d:d + 1, :] + _dot(wl, w2_ref[d]))) - 0.5
        a = jax.nn.sigmoid(a0_ref[d:d + 1, :] + _dot(al, a2_ref[d]))
        lw_ref[d, 0] = -jnp.exp(wlog)
        bq_ref[d, 0] = kk * a
        kd_ref[d, 0] = k * (1.0 + (a - 1.0) * ka_ref[...])


def _rwkv_prep(ur, mu_prev, mu_next, w0, w2, a0, a2, g2, k_k, k_a, bd):
    B = ur.shape[0]
    W = RWKV_WIDTH
    z = jnp.zeros((64, W), F32)
    w2p = jnp.stack([jnp.concatenate([w2[0], z]), jnp.concatenate([z, w2[1]])]).astype(BF16)
    a2p = jnp.stack([jnp.concatenate([a2[0], z]), jnp.concatenate([z, a2[1]])]).astype(BF16)
    tok = lambda n: pl.BlockSpec((1, TM, n), lambda b, t: (b, t, 0))
    tok2 = pl.BlockSpec((2, 1, TM, W), lambda b, t: (0, b, t, 0))
    row = lambda n: pl.BlockSpec((1, n), lambda b, t: (0, 0))
    full = lambda *s: pl.BlockSpec(s, lambda b, t: (0,) * len(s))
    slab = TM // 8
    f32 = lambda: jax.ShapeDtypeStruct((B, NT, W), F32)
    f32d = lambda: jax.ShapeDtypeStruct((2, B, NT, W), F32)
    return pl.pallas_call(
        _rwkv_prep_kernel,
        grid=(B, N_TILES),
        in_specs=[tok(RWKV_IN),
                  pl.BlockSpec((1, 8, RWKV_IN), lambda b, t: (b, jnp.maximum(t * slab - 1, 0), 0)),
                  pl.BlockSpec((1, 8, RWKV_IN), lambda b, t: (b, jnp.minimum((t + 1) * slab, NT // 8 - 1), 0)),
                  row(RWKV_IN), row(RWKV_IN), full(2, W), full(2, 128, W), full(2, W), full(2, 128, W),
                  full(128, W), row(W), row(W), full(W, W)],
        out_specs=[tok(W), tok(W), tok(W), tok(W), tok2, tok2, tok2],
        out_shape=[f32(), f32(), f32(), f32(), f32d(), f32d(), f32d()],
        compiler_params=_cp("parallel", "parallel"),
        name="rwkv_prep",
    )(ur, ur, ur, mu_prev.reshape(1, RWKV_IN), mu_next.reshape(1, RWKV_IN), w0, w2p, a0, a2p,
      g2.astype(BF16), k_k.reshape(1, W), k_a.reshape(1, W), bd)


def _wkv_kernel(r_ref, v_ref, kk_ref, lw_ref, bq_ref, kd_ref, o_ref, s_ref, *, nb):
    d = pl.program_id(0)
    i = pl.program_id(2)
    C = WKV_CHUNK
    H = RWKV_HEADS
    G = nb * H

    @pl.when(i == 0)
    def _():
        s_ref[...] = jnp.zeros_like(s_ref)

    sgn = 1 - 2 * d
    rel = (lax.broadcasted_iota(I32, (C, C), 0) - lax.broadcasted_iota(I32, (C, C), 1)) * sgn
    incl = rel >= 0
    strict = rel > 0
    eye = (rel == 0).astype(F32)
    incl_b = incl.astype(BF16)
    row_i = lax.broadcasted_iota(I32, (C, C), 0)
    col_i = lax.broadcasted_iota(I32, (C, C), 1)

    def same_block(k):
        return (row_i >> k) == (col_i >> k)


    at, rt, bt, kt, a0, r0, bh, kh, v, p_end = ([] for _ in range(10))
    for b in range(nb):
        lw = lw_ref[0, b]
        lw_hi, lw_lo = _split_bf16(lw)
        ci = _dot(incl_b, lw_hi) + _dot(incl_b, lw_lo)
        ce = ci - lw
        tot = jnp.sum(lw, axis=0, keepdims=True)
        m = 0.5 * tot
        r = r_ref[b]
        kk = kk_ref[b]
        bq = bq_ref[0, b]
        kd = kd_ref[0, b]
        e_ci = jnp.exp(ci - m)
        e_mci = jnp.exp(m - ci)
        e_end = jnp.exp(tot - ci)
        full = ((at, (-kk * jnp.exp(ce - m)).astype(BF16)), (rt, (r * e_ci).astype(BF16)),
                (bt, (bq * e_mci).astype(BF16)), (kt, (kd * e_mci).astype(BF16)),
                (a0, (-kk * jnp.exp(ce)).astype(BF16)), (r0, (r * (e_ci * jnp.exp(m))).astype(BF16)),
                (bh, (bq * e_end).astype(BF16)), (kh, (kd * e_end).astype(BF16)),
                (v, v_ref[b].astype(BF16)), (p_end, jnp.exp(tot)))
        for dst, x in full:
            dst.extend(x[:, h * RWKV_HEAD:(h + 1) * RWKV_HEAD] for h in range(H))

    S = [s_ref[g] for g in range(G)]
    s_b = [s.astype(BF16) for s in S]
    n_ab = [jnp.where(strict, _dot_nt(at[g], bt[g]), 0.0) for g in range(G)]
    a_ak = [jnp.where(strict, _dot_nt(at[g], kt[g]), 0.0).astype(BF16) for g in range(G)]
    a_rb = [jnp.where(incl, _dot_nt(rt[g], bt[g]), 0.0).astype(BF16) for g in range(G)]
    a_rk = [jnp.where(incl, _dot_nt(rt[g], kt[g]), 0.0).astype(BF16) for g in range(G)]
    w_rhs = [_dot_nt(a0[g], s_b[g]) + _dot(a_ak[g], v[g]) for g in range(G)]
    inv = [eye + jnp.where(same_block(1), n, 0.0) for n in n_ab]
    for k in range(1, 6):
        off_mask = same_block(k + 1) & jnp.logical_not(same_block(k))
        inv_b = [x.astype(BF16) for x in inv]
        e = [_dot(jnp.where(off_mask, n, 0.0).astype(BF16), x) for n, x in zip(n_ab, inv_b)]
        inv = [x + _dot(xb, y.astype(BF16)) for x, xb, y in zip(inv, inv_b, e)]
    U = [_dot(inv[g].astype(BF16), w_rhs[g].astype(BF16)).astype(BF16) for g in range(G)]
    for g in range(G):
        o = _dot_nt(r0[g], s_b[g]) + _dot(a_rb[g], U[g]) + _dot(a_rk[g], v[g])
        o_ref[0, g // H, :, (g % H) * RWKV_HEAD:(g % H + 1) * RWKV_HEAD] = o
    for g in range(G):
        s_ref[g] = S[g] * p_end[g] + _dot_tn(U[g], bh[g]) + _dot_tn(v[g], kh[g])


WKV_BATCH = 4


def _wkv(r, v, kk, lw, bq, kd, nb=WKV_BATCH):
    B = r.shape[0]
    C = WKV_CHUNK
    n_lat = T_L // C

    def chunk(d, i):
        fwd = jnp.where(i < N_CTX_CHUNKS, n_lat + i, i - N_CTX_CHUNKS)
        return jnp.where(d == 0, fwd, N_CHUNKS - 1 - i)

    shared = pl.BlockSpec((nb, C, RWKV_WIDTH), lambda d, b, i: (b, chunk(d, i), 0))
    perdir = pl.BlockSpec((1, nb, C, RWKV_WIDTH), lambda d, b, i: (d, b, chunk(d, i), 0))
    return pl.pallas_call(
        functools.partial(_wkv_kernel, nb=nb),
        grid=(2, B // nb, N_CHUNKS),
        in_specs=[shared, shared, shared, perdir, perdir, perdir],
        out_specs=perdir,
        out_shape=jax.ShapeDtypeStruct((2, B, NT, RWKV_WIDTH), F32),
        scratch_shapes=[pltpu.VMEM((nb * RWKV_HEADS, RWKV_HEAD, RWKV_HEAD), F32)],
        compiler_params=_cp("parallel", "parallel", "arbitrary"),
        name="wkv7",
    )(r, v, kk, lw, bq, kd)


def _rwkv_finish_kernel(o_ref, r_ref, v_ref, g_ref, kd_ref, rk_ref, lg_ref, lb_ref, bd_ref, y_ref):
    bd = bd_ref[...]
    o = o_ref[0, 0] + o_ref[1, 0]
    mu = _headsum(o, bd) * (1.0 / RWKV_HEAD)
    xc = o - mu
    var = _headsum(xc * xc, bd) * (1.0 / RWKV_HEAD)
    on = xc * lax.rsqrt(var + LNX_EPS) * lg_ref[...] + lb_ref[...]
    bonus = _headsum(r_ref[0] * (kd_ref[0, 0] + kd_ref[1, 0]) * rk_ref[...], bd) * v_ref[0]
    y_ref[0] = ((on + bonus) * g_ref[0]).astype(y_ref.dtype)


def _rwkv_finish(o, r, v, g, kd, r_k, lnx_g, lnx_b, bd):
    B = r.shape[0]
    W = RWKV_WIDTH
    tok = pl.BlockSpec((1, TM, W), lambda b, t: (b, t, 0))
    tok2 = pl.BlockSpec((2, 1, TM, W), lambda b, t: (0, b, t, 0))
    row = pl.BlockSpec((1, W), lambda b, t: (0, 0))
    return pl.pallas_call(
        _rwkv_finish_kernel,
        grid=(B, N_TILES),
        in_specs=[tok2, tok, tok, tok, tok2, row, row, row, pl.BlockSpec((W, W), lambda b, t: (0, 0))],
        out_specs=tok,
        out_shape=jax.ShapeDtypeStruct((B, NT, W), BF16),
        compiler_params=_cp("parallel", "parallel"),
        name="rwkv_finish",
    )(o, r, v, g, kd, r_k.reshape(1, W), lnx_g.reshape(1, W), lnx_b.reshape(1, W), bd)


MLA_QW = MLA_HEADS * MLA_SLOT


def _mla_weights(w_uq, w_ukv, partner):
    dq = MLA_NOPE + MLA_ROPE
    wq = jnp.zeros((MLA_Q_RANK, 2 * MLA_QW), F32)
    wk = jnp.zeros((MLA_KV_RANK, MLA_QW), F32)
    wv = jnp.zeros((MLA_KV_RANK, MLA_HEADS * MLA_V), F32)
    place = np.zeros((128, MLA_QW), np.float32)
    for h in range(MLA_HEADS):
        qh = w_uq[:, h * dq:(h + 1) * dq]
        wq = wq.at[:, h * MLA_SLOT:h * MLA_SLOT + dq].set(qh)
        wq = wq.at[:, MLA_QW + h * MLA_SLOT + MLA_NOPE:MLA_QW + h * MLA_SLOT + dq].set(qh[:, MLA_NOPE + partner])
        kvh = w_ukv[:, h * (MLA_NOPE + MLA_V):(h + 1) * (MLA_NOPE + MLA_V)]
        wk = wk.at[:, h * MLA_SLOT:h * MLA_SLOT + MLA_NOPE].set(kvh[:, :MLA_NOPE])
        wv = wv.at[:, h * MLA_V:(h + 1) * MLA_V].set(kvh[:, MLA_NOPE:])
        place[np.arange(MLA_ROPE), h * MLA_SLOT + MLA_NOPE + np.arange(MLA_ROPE)] = 1.0
    return wq.astype(BF16), wk.astype(BF16), wv.astype(BF16), jnp.asarray(place).astype(BF16)


def _mla_proj_kernel(u_ref, qg_ref, kg_ref, wq_ref, wk_ref, wv_ref, pl_ref, cq_ref, sq_ref, ck_ref, sk_ref,
                     q_ref, k_ref, v_ref):
    u = u_ref[0]
    qn = _rmsnorm(u[:, :MLA_Q_RANK], qg_ref[...]).astype(BF16)
    kvn = _rmsnorm(u[:, MLA_Q_RANK:MLA_Q_RANK + MLA_KV_RANK], kg_ref[...]).astype(BF16)
    o = MLA_Q_RANK + MLA_KV_RANK
    kr = u[:, o:o + 128] * ck_ref[...] + u[:, o + 128:o + 256] * sk_ref[...]
    qq = _dot(qn, wq_ref[...])
    q_ref[0] = (qq[:, :MLA_QW] * cq_ref[...] + qq[:, MLA_QW:] * sq_ref[...]).astype(BF16)
    k_ref[0] = (_dot(kvn, wk_ref[...]) + _dot2(kr, pl_ref[...])).astype(BF16)
    v_ref[0] = _dot(kvn, wv_ref[...]).astype(BF16)


def _mla_proj(um, q_norm_g, kv_norm_g, weights, tables):
    B = um.shape[0]
    wq, wk, wv, place = weights
    cq, sq, ck, sk = tables
    tok = lambda n: pl.BlockSpec((1, TM, n), lambda b, t: (b, t, 0))
    tab = lambda n: pl.BlockSpec((TM, n), lambda b, t: (t, 0))
    full = lambda a: pl.BlockSpec(a.shape, lambda b, t: (0, 0))
    return pl.pallas_call(
        _mla_proj_kernel,
        grid=(B, N_TILES),
        in_specs=[tok(MLA_U), pl.BlockSpec((1, MLA_Q_RANK), lambda b, t: (0, 0)),
                  pl.BlockSpec((1, MLA_KV_RANK), lambda b, t: (0, 0)),
                  full(wq), full(wk), full(wv), full(place),
                  tab(MLA_QW), tab(MLA_QW), tab(MLA_SLOT), tab(MLA_SLOT)],
        out_specs=[tok(MLA_QW), tok(MLA_QW), tok(MLA_HEADS * MLA_V)],
        out_shape=[jax.ShapeDtypeStruct((B, NT, MLA_QW), BF16),
                   jax.ShapeDtypeStruct((B, NT, MLA_QW), BF16),
                   jax.ShapeDtypeStruct((B, NT, MLA_HEADS * MLA_V), BF16)],
        compiler_params=_cp("parallel", "parallel"),
        name="mla_proj",
    )(um, q_norm_g.reshape(1, -1), kv_norm_g.reshape(1, -1), wq, wk, wv, place, cq, sq, ck, sk)


def _mla_attn_kernel(q_ref, k_ref, v_ref, o_ref):
    t = pl.program_id(2)
    is_ctx = t == N_TILES - 1
    col = lax.broadcasted_iota(I32, (TM, NT), 1)
    valid = jnp.logical_not(is_ctx) | (col >= T_L)
    lo_lanes = lax.broadcasted_iota(I32, (TM, 128), 1) < MLA_V
    vall = v_ref[0]
    outs = []
    for hh in range(2):
        hs = slice(hh * MLA_SLOT, (hh + 1) * MLA_SLOT)
        s = jnp.where(valid, _dot_nt(q_ref[0, :, hs], k_ref[0, :, hs]), -jnp.inf)
        m = jnp.max(s, axis=-1, keepdims=True)
        e = jnp.exp(s - m)
        pr = e * (1.0 / jnp.sum(e, axis=-1, keepdims=True))
        outs.append(_dot(pr.astype(BF16), vall))
    o_ref[0] = jnp.where(lo_lanes, outs[0], outs[1]).astype(o_ref.dtype)


def _mla_attn(q, k, v):
    B = q.shape[0]
    return pl.pallas_call(
        _mla_attn_kernel,
        grid=(B, MLA_HEADS // 2, N_TILES),
        in_specs=[pl.BlockSpec((1, TM, 2 * MLA_SLOT), lambda b, p, t: (b, t, p)),
                  pl.BlockSpec((1, NT, 2 * MLA_SLOT), lambda b, p, t: (b, 0, p)),
                  pl.BlockSpec((1, NT, 2 * MLA_V), lambda b, p, t: (b, 0, p))],
        out_specs=pl.BlockSpec((1, TM, 2 * MLA_V), lambda b, p, t: (b, t, p)),
        out_shape=jax.ShapeDtypeStruct((B, NT, MLA_HEADS * MLA_V), BF16),
        compiler_params=_cp("parallel", "parallel", "parallel"),
        name="mla_attn",
    )(q, k, v)


def _final_norm_kernel(x_ref, g_ref, o_ref):
    o_ref[0] = _rmsnorm(x_ref[0], g_ref[...])


def _final_norm(x, g):
    B = x.shape[0]
    tok = pl.BlockSpec((1, TM, D), lambda b, t: (b, t, 0))
    return pl.pallas_call(
        _final_norm_kernel,
        grid=(B, N_LAT_TILES),
        in_specs=[tok, pl.BlockSpec((1, D), lambda b, t: (0, 0))],
        out_specs=tok,
        out_shape=jax.ShapeDtypeStruct((B, T_L, D), F32),
        compiler_params=_cp("parallel", "parallel"),
        name="final_norm",
    )(x, g.reshape(1, D))


def kernel(x, c, ctx, c_ctx, ada_w, ada_b, norm_mix_g, norm_ffn_g, router_w, exp_w_gate, exp_w_up, exp_w_down, ev_w_in, ev_w_out, ev_sink, ev_pool_w, ev_pool_scale, od_w_in, od_w_out, od_mu_prev, od_mu_next, od_w0, od_w2, od_a0, od_a2, od_g2, od_k_k, od_k_a, od_r_k, od_lnx_g, od_lnx_b, od_q_norm_g, od_w_uq, od_kv_norm_g, od_w_ukv, final_norm_g):
    depth = ada_w.shape[0]
    assert x.shape[1:] == (T_L, D) and ctx.shape[1:] == (T_C, D)
    mods = _modvec(c, c_ctx, ada_w, ada_b)
    ev_tables, ev_partner = _even_tables()
    mla_tables, mla_partner = _mla_tables()
    bd = _head_ones()
    xs = jnp.concatenate([x, ctx], axis=1)
    for l in range(depth):
        need_ctx = l < depth - 1
        i = l // 2
        mod = mods[l]
        if l % 2 == 0:
            q, kd, vd, pu = _inproj_even(xs, mod, norm_mix_g[l], _even_weights(ev_w_in[i], ev_partner), ev_tables)
            mix_a = _attn_even(q, kd, vd, ev_sink[i])
            mix_b = _pool(pu, ev_pool_w[i], ev_pool_scale[i])
            w_out = ev_w_out[i]
        else:
            ur, um = _inproj_odd(xs, mod, norm_mix_g[l], _odd_weights(od_w_in[i], mla_partner))
            r, v, kk, g, lw, bq, kdir = _rwkv_prep(ur, od_mu_prev[i], od_mu_next[i], od_w0[i], od_w2[i],
                                                   od_a0[i], od_a2[i], od_g2[i], od_k_k[i], od_k_a[i], bd)
            o = _wkv(r, v, kk, lw, bq, kdir)
            mix_a = _rwkv_finish(o, r, v, g, kdir, od_r_k[i], od_lnx_g[i], od_lnx_b[i], bd)
            mq, mk, mv = _mla_proj(um, od_q_norm_g[i], od_kv_norm_g[i],
                                   _mla_weights(od_w_uq[i], od_w_ukv[i], mla_partner), mla_tables)
            mix_b = _mla_attn(mq, mk, mv)
            w_out = od_w_out[i]
        xs, h, logits = _outproj(mix_a, mix_b, w_out, xs, mod, norm_ffn_g[l], router_w[l])
        xs = _moe(xs, h, logits, mod, exp_w_gate, exp_w_up, exp_w_down, l, need_ctx)
    return _final_norm(xs, final_norm_g)
```

```python
import functools

import numpy as np
import jax
import jax.numpy as jnp
from jax import lax
from jax.experimental import pallas as pl
from jax.experimental.pallas import tpu as pltpu

F32 = jnp.float32
BF16 = jnp.bfloat16
I32 = jnp.int32
HIGHEST = lax.Precision.HIGHEST

D = 1024
T_L = 2048
T_C = 256
NT = T_L + T_C
GRID_W = 64
EPS = 1e-6
ROPE_BASE = 10000.0

TM = 256
N_TILES = NT // TM
N_LAT_TILES = T_L // TM

A_HEADS = 8
A_KV_HEADS = 2
HEAD_DIM = 64
A_WINDOW = 128
A_BLOCK = 128
N_ABLK = NT // A_BLOCK
N_LAT_ABLK = T_L // A_BLOCK

POOL_WINDOWS = (2, 4, 8, 16)
POOL_GROUP = 128

RWKV_HEADS = 8
RWKV_HEAD = 64
RWKV_WIDTH = 512
RWKV_IN = 1920
LNX_EPS = 64e-5
WKV_CHUNK = 64
N_CHUNKS = NT // WKV_CHUNK
N_CTX_CHUNKS = T_C // WKV_CHUNK

MLA_HEADS = 8
MLA_Q_RANK = 256
MLA_KV_RANK = 128
MLA_NOPE = 64
MLA_ROPE = 32
MLA_V = 64
MLA_SLOT = 128
MLA_U = 640

N_EXPERTS = 16
EXPERT_FF = 2048
CAP_L = 2 * T_L // N_EXPERTS
CAP_C = 2 * T_C // N_EXPERTS
FF_CHUNK = 512

VMEM_LIMIT_BYTES = 56 * 1024 * 1024


def _cp(*sem):
    return pltpu.CompilerParams(dimension_semantics=sem, vmem_limit_bytes=VMEM_LIMIT_BYTES)


def _split_bf16(x):
    hi = x.astype(BF16)
    lo = (x - hi.astype(F32)).astype(BF16)
    return hi, lo


def _dot(a, b):
    return jnp.dot(a, b, preferred_element_type=F32)


def _dot_nt(a, b):
    return lax.dot_general(a, b, (((1,), (1,)), ((), ())), preferred_element_type=F32)


def _dot_tn(a, b):
    return lax.dot_general(a, b, (((0,), (0,)), ((), ())), preferred_element_type=F32)


def _dot2(a_f32, b_bf16):
    hi, lo = _split_bf16(a_f32)
    return _dot(hi, b_bf16) + _dot(lo, b_bf16)


def _modulate(x, g, shift, scale):
    ms = jnp.mean(x * x, axis=-1, keepdims=True)
    return x * lax.rsqrt(ms + EPS) * g * (1.0 + scale) + shift


def _rmsnorm(x, g):
    ms = jnp.mean(x * x, axis=-1, keepdims=True)
    return x * lax.rsqrt(ms + EPS) * g


def _modvec_kernel(a_ref, w_ref, b_ref, o_ref):
    a = a_ref[...]
    a = a * jax.nn.sigmoid(a)
    o_ref[0] = jnp.dot(a, w_ref[0], precision=HIGHEST, preferred_element_type=F32) + b_ref[0]


def _modvec(c, c_ctx, ada_w, ada_b):
    L = ada_w.shape[0]
    B = c.shape[0]
    assert B < 16
    rows = jnp.zeros((16, D), F32).at[:B].set(c).at[B].set(c_ctx)
    tn = 512
    out = pl.pallas_call(
        _modvec_kernel,
        grid=(L, 6 * D // tn),
        in_specs=[pl.BlockSpec((16, D), lambda l, n: (0, 0)),
                  pl.BlockSpec((1, D, tn), lambda l, n: (l, 0, n)),
                  pl.BlockSpec((1, 1, tn), lambda l, n: (l, 0, n))],
        out_specs=pl.BlockSpec((1, 16, tn), lambda l, n: (l, 0, n)),
        out_shape=jax.ShapeDtypeStruct((L, 16, 6 * D), F32),
        compiler_params=_cp("parallel", "parallel"),
        name="modvec",
    )(rows, ada_w, ada_b.reshape(L, 1, 6 * D))
    m = out.reshape(L, 16, 6, D)
    ml = m[:, :B]
    mc = jnp.broadcast_to(m[:, B:B + 1], (L, B, 6, D))
    mod = jnp.stack([ml, mc], axis=2)
    mod = jnp.pad(mod, ((0, 0), (0, 0), (0, 0), (0, 2), (0, 0)))
    return mod.reshape(L, B * 2, 8, D)


def _mod_spec():
    return pl.BlockSpec((1, 8, D), lambda b, t: (b * 2 + t // N_LAT_TILES, 0, 0))


def _rope_block(width, pos_row, pos_col):
    d = width // 2
    half = d // 2
    inv = ROPE_BASE ** (-np.arange(half, dtype=np.float32) / half)
    cos = np.zeros((pos_row.shape[0], width), np.float32)
    sin = np.zeros_like(cos)
    partner = np.zeros((width,), np.int64)
    for part, pos in ((0, pos_row), (1, pos_col)):
        ang = pos.astype(np.float32)[:, None] * inv[None, :]
        c, s = np.cos(ang), np.sin(ang)
        o = part * d
        cos[:, o:o + half] = c
        cos[:, o + half:o + d] = c
        sin[:, o:o + half] = -s
        sin[:, o + half:o + d] = s
        partner[o:o + half] = np.arange(o + half, o + d)
        partner[o + half:o + d] = np.arange(o, o + half)
    return cos, sin, partner


def _latent_positions():
    t = np.arange(T_L)
    return t // GRID_W, t % GRID_W


def _even_tables():
    cos, sin, partner = _rope_block(HEAD_DIM, *_latent_positions())
    scale = HEAD_DIM ** -0.5
    ones = np.ones((T_C, HEAD_DIM), np.float32)
    zeros = np.zeros((T_C, HEAD_DIM), np.float32)
    cos = np.concatenate([cos, ones], axis=0)
    sin = np.concatenate([sin, zeros], axis=0)
    cq = np.tile(cos, (1, A_HEADS)) * scale
    sq = np.tile(sin, (1, A_HEADS)) * scale
    ck = np.tile(cos, (1, 2 * A_KV_HEADS))
    sk = np.tile(sin, (1, 2 * A_KV_HEADS))
    return (jnp.asarray(cq), jnp.asarray(sq), jnp.asarray(ck), jnp.asarray(sk)), partner


def _mla_tables():
    cos, sin, partner = _rope_block(MLA_ROPE, *_latent_positions())
    cos = np.concatenate([cos, np.ones((T_C, MLA_ROPE), np.float32)], axis=0)
    sin = np.concatenate([sin, np.zeros((T_C, MLA_ROPE), np.float32)], axis=0)
    scale = (MLA_NOPE + MLA_ROPE) ** -0.5
    cq1 = np.zeros((NT, MLA_SLOT), np.float32)
    sq1 = np.zeros((NT, MLA_SLOT), np.float32)
    cq1[:, :MLA_NOPE] = 1.0
    cq1[:, MLA_NOPE:MLA_NOPE + MLA_ROPE] = cos
    sq1[:, MLA_NOPE:MLA_NOPE + MLA_ROPE] = sin
    cq = np.tile(cq1, (1, MLA_HEADS)) * scale
    sq = np.tile(sq1, (1, MLA_HEADS)) * scale
    ck = np.zeros((NT, MLA_SLOT), np.float32)
    sk = np.zeros((NT, MLA_SLOT), np.float32)
    ck[:, :MLA_ROPE] = cos
    sk[:, :MLA_ROPE] = sin
    return (jnp.asarray(cq), jnp.asarray(sq), jnp.asarray(ck), jnp.asarray(sk)), partner


EV_Q, EV_K, EV_V, EV_P = 512, 256, 256, 512
EV_N = 2 * EV_Q + 2 * EV_K + EV_V + EV_P


def _even_weights(w_in, partner):
    q = w_in[:, :512]
    k = w_in[:, 512:640]
    v = w_in[:, 640:768]
    p = w_in[:, 768:1280]
    pq = np.concatenate([h * HEAD_DIM + partner for h in range(A_HEADS)])
    pk = np.concatenate([h * HEAD_DIM + partner for h in range(A_KV_HEADS)])
    dup = np.concatenate([np.arange(HEAD_DIM), np.arange(HEAD_DIM),
                          HEAD_DIM + np.arange(HEAD_DIM), HEAD_DIM + np.arange(HEAD_DIM)])
    w = jnp.concatenate([q, q[:, pq], k[:, dup], k[:, pk][:, dup], v[:, dup], p], axis=1)
    return w.astype(BF16)


def _inproj_even_kernel(x_ref, mod_ref, g_ref, w_ref, cq_ref, sq_ref, ck_ref, sk_ref,
                        q_ref, k_ref, v_ref, p_ref):
    mod = mod_ref[0]
    h = _modulate(x_ref[0], g_ref[...], mod[0:1], mod[1:2])
    y = _dot(h.astype(BF16), w_ref[...])
    o = 0
    q = y[:, o:o + EV_Q] * cq_ref[...] + y[:, o + EV_Q:o + 2 * EV_Q] * sq_ref[...]
    o += 2 * EV_Q
    k = y[:, o:o + EV_K] * ck_ref[...] + y[:, o + EV_K:o + 2 * EV_K] * sk_ref[...]
    o += 2 * EV_K
    q_ref[0] = q.astype(BF16)
    k_ref[0] = k.astype(BF16)
    v_ref[0] = y[:, o:o + EV_V].astype(BF16)
    p_ref[0] = y[:, o + EV_V:o + EV_V + EV_P]


def _inproj_even(x, mod, g, w, tables):
    B = x.shape[0]
    cq, sq, ck, sk = tables
    tok = lambda n: pl.BlockSpec((1, TM, n), lambda b, t: (b, t, 0))
    tab = lambda n: pl.BlockSpec((TM, n), lambda b, t: (t, 0))
    return pl.pallas_call(
        _inproj_even_kernel,
        grid=(B, N_TILES),
        in_specs=[tok(D), _mod_spec(), pl.BlockSpec((1, D), lambda b, t: (0, 0)),
                  pl.BlockSpec((D, EV_N), lambda b, t: (0, 0)),
                  tab(EV_Q), tab(EV_Q), tab(EV_K), tab(EV_K)],
        out_specs=[tok(EV_Q), tok(EV_K), tok(EV_V), tok(EV_P)],
        out_shape=[jax.ShapeDtypeStruct((B, NT, EV_Q), BF16),
                   jax.ShapeDtypeStruct((B, NT, EV_K), BF16),
                   jax.ShapeDtypeStruct((B, NT, EV_V), BF16),
                   jax.ShapeDtypeStruct((B, NT, EV_P), F32)],
        compiler_params=_cp("parallel", "parallel"),
        name="inproj_even",
    )(x, mod, g.reshape(1, D), w, cq, sq, ck, sk)


def _attn_even_kernel(sink_ref, q_ref, kp_ref, kc_ref, kn_ref, kx_ref,
                      vp_ref, vc_ref, vn_ref, vx_ref, o_ref):
    j = pl.program_id(1)
    is_lat = j < N_LAT_ABLK
    q = q_ref[0]
    qpos = j * A_BLOCK + lax.broadcasted_iota(I32, (A_BLOCK, 3 * A_BLOCK), 0)
    kpos = (j - 1) * A_BLOCK + lax.broadcasted_iota(I32, (A_BLOCK, 3 * A_BLOCK), 1)
    valid = (jnp.abs(kpos - qpos) <= A_WINDOW) & (kpos >= 0) & (kpos < T_L) & is_lat
    lo_lanes = lax.broadcasted_iota(I32, (A_BLOCK, 128), 1) < HEAD_DIM
    kall, vall = [], []
    for kv in range(A_KV_HEADS):
        ks = slice(kv * 128, (kv + 1) * 128)
        kall.append(jnp.concatenate([kx_ref[0, :, ks], kp_ref[0, :, ks], kc_ref[0, :, ks], kn_ref[0, :, ks]], axis=0))
        vall.append(jnp.concatenate([vx_ref[0, :, ks], vp_ref[0, :, ks], vc_ref[0, :, ks], vn_ref[0, :, ks]], axis=0))
    heads = range(A_HEADS)
    group = A_HEADS // A_KV_HEADS
    qm = [jnp.where(lo_lanes if h % 2 == 0 else jnp.logical_not(lo_lanes), q[:, (h // 2) * 128:(h // 2 + 1) * 128],
                    jnp.zeros((A_BLOCK, 128), q.dtype)) for h in heads]
    s = [_dot_nt(qm[h], kall[h // group]) for h in heads]
    e, inv_den = [], []
    for h in heads:
        snk = sink_ref[h]
        s_ctx = s[h][:, :T_C]
        s_loc = jnp.where(valid, s[h][:, T_C:], -jnp.inf)
        m = jnp.maximum(jnp.max(s_ctx, axis=-1, keepdims=True), jnp.max(s_loc, axis=-1, keepdims=True))
        m = jnp.maximum(m, snk)
        e_ctx = jnp.exp(s_ctx - m)
        e_loc = jnp.exp(s_loc - m)
        den = (jnp.sum(e_ctx, axis=-1, keepdims=True) + jnp.sum(e_loc, axis=-1, keepdims=True)
               + jnp.exp(snk - m))
        e.append(jnp.concatenate([e_ctx, e_loc], axis=1).astype(BF16))
        inv_den.append(1.0 / den)
    o = [_dot(e[h], vall[h // group]) * inv_den[h] for h in heads]
    for p in range(A_HEADS // 2):
        o_ref[0, :, p * 128:(p + 1) * 128] = jnp.where(lo_lanes, o[2 * p], o[2 * p + 1]).astype(o_ref.dtype)


def _attn_even(q, kd, vd, sink):
    B = q.shape[0]
    last = N_LAT_ABLK - 1
    blk = lambda f: pl.BlockSpec((1, A_BLOCK, 256), f)
    prev = lambda b, j: (b, jnp.clip(j - 1, 0, last), 0)
    cur = lambda b, j: (b, j, 0)
    nxt = lambda b, j: (b, jnp.clip(j + 1, 0, last), 0)
    ctx = pl.BlockSpec((1, T_C, 256), lambda b, j: (b, T_L // T_C, 0))
    return pl.pallas_call(
        _attn_even_kernel,
        grid=(B, N_ABLK),
        in_specs=[pl.BlockSpec(memory_space=pltpu.SMEM),
                  pl.BlockSpec((1, A_BLOCK, EV_Q), cur),
                  blk(prev), blk(cur), blk(nxt), ctx,
                  blk(prev), blk(cur), blk(nxt), ctx],
        out_specs=pl.BlockSpec((1, A_BLOCK, 512), cur),
        out_shape=jax.ShapeDtypeStruct((B, NT, 512), BF16),
        compiler_params=_cp("parallel", "parallel"),
        name="attn_even",
    )(sink, q, kd, kd, kd, kd, vd, vd, vd, vd)


def _pool_kernel(up_ref, uc_ref, un_ref, w_ref, sc_ref, o_ref):
    j = pl.program_id(1)
    is_lat = j < N_LAT_ABLK
    seg_lo = jnp.where(is_lat, 0, T_L)
    seg_hi = jnp.where(is_lat, T_L, NT)
    u3 = jnp.concatenate([up_ref[0], uc_ref[0], un_ref[0]], axis=0)
    uc = uc_ref[0]
    r = j * A_BLOCK + lax.broadcasted_iota(I32, (A_BLOCK, 3 * A_BLOCK), 0)
    c = (j - 1) * A_BLOCK + lax.broadcasted_iota(I32, (A_BLOCK, 3 * A_BLOCK), 1)
    r1 = j * A_BLOCK + lax.broadcasted_iota(I32, (A_BLOCK, 1), 0)
    outs = []
    for gi, w in enumerate(POOL_WINDOWS):
        cs = slice(gi * POOL_GROUP, (gi + 1) * POOL_GROUP)
        lo = jnp.maximum(r - w // 2, seg_lo)
        hi = jnp.minimum(r + w // 2, seg_hi)
        band = ((c >= lo) & (c < hi)).astype(BF16)
        cnt = (jnp.minimum(r1 + w // 2, seg_hi) - jnp.maximum(r1 - w // 2, seg_lo)).astype(F32)
        hi_u, lo_u = _split_bf16(u3[:, cs])
        mean = (_dot(band, hi_u) + _dot(band, lo_u)) / cnt
        outs.append(_dot((mean - uc[:, cs]).astype(BF16), w_ref[gi]))
    o_ref[0] = (jnp.concatenate(outs, axis=1) * sc_ref[...]).astype(o_ref.dtype)


def _pool(u, pool_w, pool_scale):
    B = u.shape[0]
    blk = lambda f: pl.BlockSpec((1, A_BLOCK, 512), f)
    return pl.pallas_call(
        _pool_kernel,
        grid=(B, N_ABLK),
        in_specs=[blk(lambda b, j: (b, jnp.maximum(j - 1, 0), 0)),
                  blk(lambda b, j: (b, j, 0)),
                  blk(lambda b, j: (b, jnp.minimum(j + 1, N_ABLK - 1), 0)),
                  pl.BlockSpec((4, POOL_GROUP, POOL_GROUP), lambda b, j: (0, 0, 0)),
                  pl.BlockSpec((1, 512), lambda b, j: (0, 0))],
        out_specs=blk(lambda b, j: (b, j, 0)),
        out_shape=jax.ShapeDtypeStruct((B, NT, 512), BF16),
        compiler_params=_cp("parallel", "parallel"),
        name="pool",
    )(u, u, u, pool_w.astype(BF16), pool_scale.reshape(1, 512))


def _outproj_kernel(a_ref, p_ref, w_ref, x_ref, mod_ref, g_ref, wr_ref, xo_ref, h_ref, lg_ref):
    mod = mod_ref[0]
    y = _dot(a_ref[0], w_ref[0:512, :]) + _dot(p_ref[0], w_ref[512:1024, :])
    x = x_ref[0] + mod[2:3] * y
    xo_ref[0] = x
    h = _modulate(x, g_ref[...], mod[3:4], mod[4:5])
    h_ref[0] = h.astype(BF16)
    lg_ref[0] = lax.dot_general(wr_ref[...], h, (((1,), (1,)), ((), ())),
                                precision=HIGHEST, preferred_element_type=F32)


def _outproj(a, p, w_out, x, mod, g_ffn, w_router):
    B = x.shape[0]
    tok = lambda n: pl.BlockSpec((1, TM, n), lambda b, t: (b, t, 0))
    return pl.pallas_call(
        _outproj_kernel,
        grid=(B, N_TILES),
        in_specs=[tok(512), tok(512), pl.BlockSpec((D, D), lambda b, t: (0, 0)), tok(D), _mod_spec(),
                  pl.BlockSpec((1, D), lambda b, t: (0, 0)),
                  pl.BlockSpec((N_EXPERTS, D), lambda b, t: (0, 0))],
        out_specs=[tok(D), tok(D), pl.BlockSpec((1, N_EXPERTS, TM), lambda b, t: (b, 0, t))],
        out_shape=[jax.ShapeDtypeStruct((B, NT, D), F32),
                   jax.ShapeDtypeStruct((B, NT, D), BF16),
                   jax.ShapeDtypeStruct((B, N_EXPERTS, NT), F32)],
        compiler_params=_cp("parallel", "parallel"),
        name="outproj",
    )(a, p, w_out.astype(BF16), x, mod, g_ffn.reshape(1, D), w_router.T)


def _topk_kernel(lg_ref, sid_ref, gate_ref, *, cap):
    B, E, T = lg_ref.shape
    R = B * E
    lg = lg_ref[...]
    m = jnp.max(lg, axis=1, keepdims=True)
    e = jnp.exp(lg - m)
    aff = (e / jnp.sum(e, axis=1, keepdims=True)).reshape(R, T)
    bits = lax.bitcast_convert_type(aff, I32)

    def step(i, thr):
        cand = thr | jnp.left_shift(jnp.int32(1), 30 - i)
        cnt = jnp.sum((bits >= cand).astype(F32), axis=1, keepdims=True)
        return jnp.where(cnt >= cap, cand, thr)

    thr = lax.fori_loop(0, 31, step, jnp.zeros((R, 1), I32))
    gt = bits > thr
    eq = bits == thr
    need = cap - jnp.sum(gt.astype(F32), axis=1, keepdims=True)

    blk = 256
    tri = (lax.broadcasted_iota(I32, (blk, blk), 0) <= lax.broadcasted_iota(I32, (blk, blk), 1)).astype(BF16)

    def prefix(mask_f32):
        off = jnp.zeros((R, 1), F32)
        parts = []
        for k in range(T // blk):
            cb = _dot(mask_f32[:, k * blk:(k + 1) * blk].astype(BF16), tri) + off
            parts.append(cb)
            off = cb[:, blk - 1:blk]
        return parts

    eq_parts = prefix(eq.astype(F32))
    sel_parts = []
    for k in range(T // blk):
        sl = slice(k * blk, (k + 1) * blk)
        sel_parts.append(gt[:, sl] | (eq[:, sl] & (eq_parts[k] <= need)))
    sel = jnp.concatenate(sel_parts, axis=1)
    pos_parts = prefix(sel.astype(F32))
    for k in range(T // blk):
        sl = slice(k * blk, (k + 1) * blk)
        sid = jnp.where(sel_parts[k], pos_parts[k].astype(I32) - 1, -1)
        sid_ref[:, :, sl] = sid.reshape(B, E, blk)
        gate_ref[:, :, sl] = jnp.where(sel_parts[k], aff[:, sl], 0.0).reshape(B, E, blk)


def _topk(logits, cap):
    B, E, T = logits.shape
    full = pl.BlockSpec((B, E, T), lambda i: (0, 0, 0))
    return pl.pallas_call(
        functools.partial(_topk_kernel, cap=cap),
        grid=(1,),
        in_specs=[full],
        out_specs=[full, full],
        out_shape=[jax.ShapeDtypeStruct((B, E, T), I32), jax.ShapeDtypeStruct((B, E, T), F32)],
        compiler_params=_cp("arbitrary"),
        name=f"topk_{T}",
    )(logits)


def _gather_kernel(h_ref, sid_ref, gate_ref, xs_ref, gs_ref, *, cap, chunk):
    T = h_ref.shape[1]
    acc = jnp.zeros((cap, D), F32)
    gacc = jnp.zeros((cap, 1), F32)
    slot = lax.broadcasted_iota(I32, (cap, chunk), 0)
    for k in range(T // chunk):
        sl = slice(k * chunk, (k + 1) * chunk)
        hit = slot == sid_ref[0, 0, :, sl]
        acc = acc + _dot(hit.astype(BF16), h_ref[0, sl, :])
        gacc = gacc + jnp.sum(jnp.where(hit, gate_ref[0, 0, :, sl], 0.0), axis=1, keepdims=True)
    xs_ref[0] = acc.astype(BF16)
    gs_ref[0] = jnp.broadcast_to(gacc, (cap, 128))


def _gather(h, sid, gate, cap, row_block):
    B, E, T = sid.shape
    chunk = min(T, 512)
    idx = lambda b, e: (b, e, 0, 0)
    return pl.pallas_call(
        functools.partial(_gather_kernel, cap=cap, chunk=chunk),
        grid=(B, E),
        in_specs=[pl.BlockSpec((1, T, D), lambda b, e: (b, row_block, 0)),
                  pl.BlockSpec((1, 1, 1, T), idx), pl.BlockSpec((1, 1, 1, T), idx)],
        out_specs=[pl.BlockSpec((1, cap, D), lambda b, e: (e, b, 0)),
                   pl.BlockSpec((1, cap, 128), lambda b, e: (e, b, 0))],
        out_shape=[jax.ShapeDtypeStruct((E, B * cap, D), BF16),
                   jax.ShapeDtypeStruct((E, B * cap, 128), F32)],
        compiler_params=_cp("parallel", "arbitrary"),
        name=f"gather_{T}",
    )(h, sid.reshape(B, E, 1, T), gate.reshape(B, E, 1, T))


def _ffn_kernel(*refs, row_tiles):
    n_seg = len(row_tiles)
    xs_refs = refs[:n_seg]
    gs_refs = refs[n_seg:2 * n_seg]
    wg_ref, wu_ref, wd_ref = refs[2 * n_seg:2 * n_seg + 3]
    y_refs = refs[2 * n_seg + 3:3 * n_seg + 3]
    acc_refs = refs[3 * n_seg + 3:]
    f = pl.program_id(1)
    wg = wg_ref[0].astype(BF16)
    wu = wu_ref[0].astype(BF16)
    wd = wd_ref[0].astype(BF16)
    for xs_ref, gs_ref, y_ref, acc_ref, rt in zip(xs_refs, gs_refs, y_refs, acc_refs, row_tiles):
        M = xs_ref.shape[1]
        for r0 in range(0, M, rt):
            rows = pl.ds(r0, rt)
            x = xs_ref[0, rows, :]
            g = _dot(x, wg)
            u = _dot(x, wu)
            hid = (g * jax.nn.sigmoid(g) * u).astype(BF16)
            contrib = _dot(hid, wd)

            @pl.when(f == 0)
            def _():
                acc_ref[rows, :] = contrib

            @pl.when(f > 0)
            def _():
                acc_ref[rows, :] += contrib

        @pl.when(f == pl.num_programs(1) - 1)
        def _():
            gs = gs_ref[0]
            for k in range(D // 128):
                cs = slice(k * 128, (k + 1) * 128)
                y_ref[0, :, cs] = (acc_ref[:, cs] * gs).astype(y_ref.dtype)


def _ffn(xs_list, gs_list, w_gate, w_up, w_down, layer):
    E = w_gate.shape[1]
    row_tiles = tuple(min(xs.shape[1], 512) for xs in xs_list)
    nf = EXPERT_FF // FF_CHUNK
    once = dict(pipeline_mode=pl.Buffered(1))
    in_specs = [pl.BlockSpec((1, xs.shape[1], D), lambda e, f: (e, 0, 0), **once) for xs in xs_list]
    in_specs += [pl.BlockSpec((1, gs.shape[1], 128), lambda e, f: (e, 0, 0), **once) for gs in gs_list]
    in_specs += [pl.BlockSpec((None, 1, D, FF_CHUNK), lambda e, f: (layer, e, 0, f)),
                 pl.BlockSpec((None, 1, D, FF_CHUNK), lambda e, f: (layer, e, 0, f)),
                 pl.BlockSpec((None, 1, FF_CHUNK, D), lambda e, f: (layer, e, f, 0))]
    return pl.pallas_call(
        functools.partial(_ffn_kernel, row_tiles=row_tiles),
        grid=(E, nf),
        in_specs=in_specs,
        out_specs=[pl.BlockSpec((1, xs.shape[1], D), lambda e, f: (e, 0, 0)) for xs in xs_list],
        out_shape=[jax.ShapeDtypeStruct(xs.shape, BF16) for xs in xs_list],
        scratch_shapes=[pltpu.VMEM((xs.shape[1], D), F32) for xs in xs_list],
        compiler_params=_cp("parallel", "arbitrary"),
        name="expert_ffn",
    )(*xs_list, *gs_list, w_gate, w_up, w_down)


def _combine_kernel(x_ref, sid_ref, y_ref, mod_ref, o_ref, *, cap):
    tbs = x_ref.shape[1]
    E = sid_ref.shape[1]
    slot = lax.broadcasted_iota(I32, (cap, tbs), 0)
    hit = jnp.concatenate([(slot == sid_ref[0, e]).astype(BF16) for e in range(E)], axis=0)
    z = _dot_tn(hit, y_ref[...].reshape(E * cap, D))
    o_ref[0] = x_ref[0] + mod_ref[0][5:6] * z


def _combine(x, sid, y, mod, cap, tbs, row_block0, segment):
    B, E, T = sid.shape
    ntb = T // tbs
    xmap = lambda b, tb: (b, row_block0 + tb, 0)
    return pl.pallas_call(
        functools.partial(_combine_kernel, cap=cap),
        grid=(B, ntb),
        in_specs=[pl.BlockSpec((1, tbs, D), xmap),
                  pl.BlockSpec((1, E, 1, tbs), lambda b, tb: (b, 0, 0, tb)),
                  pl.BlockSpec((E, cap, D), lambda b, tb: (0, b, 0)),
                  pl.BlockSpec((1, 8, D), lambda b, tb: (b * 2 + segment, 0, 0))],
        out_specs=pl.BlockSpec((1, tbs, D), xmap),
        out_shape=jax.ShapeDtypeStruct(x.shape, F32),
        input_output_aliases={0: 0},
        compiler_params=_cp("parallel", "parallel"),
        name=f"combine_{T}",
    )(x, sid.reshape(B, E, 1, T), y, mod)


def _moe(x, h, logits, mod, w_gate, w_up, w_down, layer, need_ctx):
    sid_l, gate_l = _topk(logits[:, :, :T_L], CAP_L)
    xs_l, gs_l = _gather(h, sid_l, gate_l, CAP_L, 0)
    if need_ctx:
        sid_c, gate_c = _topk(logits[:, :, T_L:], CAP_C)
        xs_c, gs_c = _gather(h, sid_c, gate_c, CAP_C, T_L // T_C)
        y_l, y_c = _ffn([xs_l, xs_c], [gs_l, gs_c], w_gate, w_up, w_down, layer)
    else:
        (y_l,) = _ffn([xs_l], [gs_l], w_gate, w_up, w_down, layer)
    x = _combine(x, sid_l, y_l, mod, CAP_L, 512, 0, 0)
    if need_ctx:
        x = _combine(x, sid_c, y_c, mod, CAP_C, T_C, T_L // T_C, 1)
    return x


OD_N = RWKV_IN + MLA_U


def _odd_weights(w_in, partner):
    rw = w_in[:, :RWKV_IN]
    qd = w_in[:, RWKV_IN:RWKV_IN + MLA_Q_RANK]
    kvd = w_in[:, RWKV_IN + MLA_Q_RANK:RWKV_IN + MLA_Q_RANK + MLA_KV_RANK]
    kr = w_in[:, RWKV_IN + MLA_Q_RANK + MLA_KV_RANK:]
    pad = jnp.zeros((D, 128 - MLA_ROPE), w_in.dtype)
    w = jnp.concatenate([rw, qd, kvd, kr, pad, kr[:, partner], pad], axis=1)
    return w.astype(BF16)


def _inproj_odd_kernel(x_ref, mod_ref, g_ref, w_ref, ur_ref, um_ref):
    mod = mod_ref[0]
    h = _modulate(x_ref[0], g_ref[...], mod[0:1], mod[1:2])
    y = _dot(h.astype(BF16), w_ref[...])
    ur_ref[0] = y[:, :RWKV_IN]
    um_ref[0] = y[:, RWKV_IN:]


def _inproj_odd(x, mod, g, w):
    B = x.shape[0]
    tok = lambda n: pl.BlockSpec((1, TM, n), lambda b, t: (b, t, 0))
    return pl.pallas_call(
        _inproj_odd_kernel,
        grid=(B, N_TILES),
        in_specs=[tok(D), _mod_spec(), pl.BlockSpec((1, D), lambda b, t: (0, 0)),
                  pl.BlockSpec((D, OD_N), lambda b, t: (0, 0))],
        out_specs=[tok(RWKV_IN), tok(MLA_U)],
        out_shape=[jax.ShapeDtypeStruct((B, NT, RWKV_IN), F32),
                   jax.ShapeDtypeStruct((B, NT, MLA_U), F32)],
        compiler_params=_cp("parallel", "parallel"),
        name="inproj_odd",
    )(x, mod, g.reshape(1, D), w)


def _head_ones():
    h = np.arange(RWKV_WIDTH) // RWKV_HEAD
    return jnp.asarray((h[:, None] == h[None, :]).astype(np.float32)).astype(BF16)


def _headsum(x, bd):
    return _dot2(x, bd)


def _softplus(z):
    return jnp.maximum(z, 0.0) + jnp.log(1.0 + jnp.exp(-jnp.abs(z)))


def _rwkv_prep_kernel(u_ref, up_ref, un_ref, mup_ref, mun_ref, w0_ref, w2_ref, a0_ref, a2_ref, g2_ref,
                      kk_ref, ka_ref, bd_ref,
                      r_ref, v_ref, kkn_ref, g_ref, lw_ref, bq_ref, kd_ref):
    t = pl.program_id(1)
    u = u_ref[0]
    rows = lax.broadcasted_iota(I32, (TM, 1), 0)
    seg_start = (t == 0) | (t == N_LAT_TILES)
    seg_end = (t == N_LAT_TILES - 1) | (t == N_TILES - 1)
    prev_row = jnp.where(seg_start, 0.0, up_ref[0, 7:8, :])
    next_row = jnp.where(seg_end, 0.0, un_ref[0, 0:1, :])
    prev = jnp.where(rows == 0, prev_row, pltpu.roll(u, 1, 0))
    nxt = jnp.where(rows == TM - 1, next_row, pltpu.roll(u, TM - 1, 0))
    us = u + mup_ref[...] * (prev - u) + mun_ref[...] * (nxt - u)
    W = RWKV_WIDTH
    r = us[:, :W]
    k = us[:, W:2 * W]
    v = us[:, 2 * W:3 * W]
    wl = jnp.tanh(us[:, 3 * W:3 * W + 128]).astype(BF16)
    al = us[:, 3 * W + 128:3 * W + 256].astype(BF16)
    gl = jax.nn.sigmoid(us[:, 3 * W + 256:3 * W + 384]).astype(BF16)
    bd = bd_ref[...]
    kk = k * kk_ref[...]
    kk = kk * lax.rsqrt(jnp.maximum(_headsum(kk * kk, bd), 1e-24))
    r_ref[0] = r
    v_ref[0] = v
    kkn_ref[0] = kk
    g_ref[0] = _dot(gl, g2_ref[...])
    for d in range(2):
        wlog = -_softplus(-(w0_ref[d:d + 1, :] + _dot(wl, w2_ref[d]))) - 0.5
        a = jax.nn.sigmoid(a0_ref[d:d + 1, :] + _dot(al, a2_ref[d]))
        lw_ref[d, 0] = -jnp.exp(wlog)
        bq_ref[d, 0] = kk * a
        kd_ref[d, 0] = k * (1.0 + (a - 1.0) * ka_ref[...])


def _rwkv_prep(ur, mu_prev, mu_next, w0, w2, a0, a2, g2, k_k, k_a, bd):
    B = ur.shape[0]
    W = RWKV_WIDTH
    z = jnp.zeros((64, W), F32)
    w2p = jnp.stack([jnp.concatenate([w2[0], z]), jnp.concatenate([z, w2[1]])]).astype(BF16)
    a2p = jnp.stack([jnp.concatenate([a2[0], z]), jnp.concatenate([z, a2[1]])]).astype(BF16)
    tok = lambda n: pl.BlockSpec((1, TM, n), lambda b, t: (b, t, 0))
    tok2 = pl.BlockSpec((2, 1, TM, W), lambda b, t: (0, b, t, 0))
    row = lambda n: pl.BlockSpec((1, n), lambda b, t: (0, 0))
    full = lambda *s: pl.BlockSpec(s, lambda b, t: (0,) * len(s))
    slab = TM // 8
    f32 = lambda: jax.ShapeDtypeStruct((B, NT, W), F32)
    f32d = lambda: jax.ShapeDtypeStruct((2, B, NT, W), F32)
    return pl.pallas_call(
        _rwkv_prep_kernel,
        grid=(B, N_TILES),
        in_specs=[tok(RWKV_IN),
                  pl.BlockSpec((1, 8, RWKV_IN), lambda b, t: (b, jnp.maximum(t * slab - 1, 0), 0)),
                  pl.BlockSpec((1, 8, RWKV_IN), lambda b, t: (b, jnp.minimum((t + 1) * slab, NT // 8 - 1), 0)),
                  row(RWKV_IN), row(RWKV_IN), full(2, W), full(2, 128, W), full(2, W), full(2, 128, W),
                  full(128, W), row(W), row(W), full(W, W)],
        out_specs=[tok(W), tok(W), tok(W), tok(W), tok2, tok2, tok2],
        out_shape=[f32(), f32(), f32(), f32(), f32d(), f32d(), f32d()],
        compiler_params=_cp("parallel", "parallel"),
        name="rwkv_prep",
    )(ur, ur, ur, mu_prev.reshape(1, RWKV_IN), mu_next.reshape(1, RWKV_IN), w0, w2p, a0, a2p,
      g2.astype(BF16), k_k.reshape(1, W), k_a.reshape(1, W), bd)


def _wkv_kernel(r_ref, v_ref, kk_ref, lw_ref, bq_ref, kd_ref, o_ref, s_ref, *, nb):
    d = pl.program_id(0)
    i = pl.program_id(2)
    C = WKV_CHUNK
    H = RWKV_HEADS
    G = nb * H

    @pl.when(i == 0)
    def _():
        s_ref[...] = jnp.zeros_like(s_ref)

    sgn = 1 - 2 * d
    rel = (lax.broadcasted_iota(I32, (C, C), 0) - lax.broadcasted_iota(I32, (C, C), 1)) * sgn
    incl = rel >= 0
    strict = rel > 0
    eye = (rel == 0).astype(F32)
    incl_b = incl.astype(BF16)
    row_i = lax.broadcasted_iota(I32, (C, C), 0)
    col_i = lax.broadcasted_iota(I32, (C, C), 1)

    def same_block(k):
        return (row_i >> k) == (col_i >> k)


    at, rt, bt, kt, a0, r0, bh, kh, v, p_end = ([] for _ in range(10))
    for b in range(nb):
        lw = lw_ref[0, b]
        lw_hi, lw_lo = _split_bf16(lw)
        ci = _dot(incl_b, lw_hi) + _dot(incl_b, lw_lo)
        ce = ci - lw
        tot = jnp.sum(lw, axis=0, keepdims=True)
        m = 0.5 * tot
        r = r_ref[b]
        kk = kk_ref[b]
        bq = bq_ref[0, b]
        kd = kd_ref[0, b]
        e_ci = jnp.exp(ci - m)
        e_mci = jnp.exp(m - ci)
        e_end = jnp.exp(tot - ci)
        full = ((at, (-kk * jnp.exp(ce - m)).astype(BF16)), (rt, (r * e_ci).astype(BF16)),
                (bt, (bq * e_mci).astype(BF16)), (kt, (kd * e_mci).astype(BF16)),
                (a0, (-kk * jnp.exp(ce)).astype(BF16)), (r0, (r * (e_ci * jnp.exp(m))).astype(BF16)),
                (bh, (bq * e_end).astype(BF16)), (kh, (kd * e_end).astype(BF16)),
                (v, v_ref[b].astype(BF16)), (p_end, jnp.exp(tot)))
        for dst, x in full:
            dst.extend(x[:, h * RWKV_HEAD:(h + 1) * RWKV_HEAD] for h in range(H))

    S = [s_ref[g] for g in range(G)]
    s_b = [s.astype(BF16) for s in S]
    n_ab = [jnp.where(strict, _dot_nt(at[g], bt[g]), 0.0) for g in range(G)]
    a_ak = [jnp.where(strict, _dot_nt(at[g], kt[g]), 0.0).astype(BF16) for g in range(G)]
    a_rb = [jnp.where(incl, _dot_nt(rt[g], bt[g]), 0.0).astype(BF16) for g in range(G)]
    a_rk = [jnp.where(incl, _dot_nt(rt[g], kt[g]), 0.0).astype(BF16) for g in range(G)]
    w_rhs = [_dot_nt(a0[g], s_b[g]) + _dot(a_ak[g], v[g]) for g in range(G)]
    inv = [eye + jnp.where(same_block(1), n, 0.0) for n in n_ab]
    for k in range(1, 6):
        off_mask = same_block(k + 1) & jnp.logical_not(same_block(k))
        inv_b = [x.astype(BF16) for x in inv]
        e = [_dot(jnp.where(off_mask, n, 0.0).astype(BF16), x) for n, x in zip(n_ab, inv_b)]
        inv = [x + _dot(xb, y.astype(BF16)) for x, xb, y in zip(inv, inv_b, e)]
    U = [_dot(inv[g].astype(BF16), w_rhs[g].astype(BF16)).astype(BF16) for g in range(G)]
    for g in range(G):
        o = _dot_nt(r0[g], s_b[g]) + _dot(a_rb[g], U[g]) + _dot(a_rk[g], v[g])
        o_ref[0, g // H, :, (g % H) * RWKV_HEAD:(g % H + 1) * RWKV_HEAD] = o
    for g in range(G):
        s_ref[g] = S[g] * p_end[g] + _dot_tn(U[g], bh[g]) + _dot_tn(v[g], kh[g])


WKV_BATCH = 4


def _wkv(r, v, kk, lw, bq, kd, nb=WKV_BATCH):
    B = r.shape[0]
    C = WKV_CHUNK
    n_lat = T_L // C

    def chunk(d, i):
        fwd = jnp.where(i < N_CTX_CHUNKS, n_lat + i, i - N_CTX_CHUNKS)
        return jnp.where(d == 0, fwd, N_CHUNKS - 1 - i)

    shared = pl.BlockSpec((nb, C, RWKV_WIDTH), lambda d, b, i: (b, chunk(d, i), 0))
    perdir = pl.BlockSpec((1, nb, C, RWKV_WIDTH), lambda d, b, i: (d, b, chunk(d, i), 0))
    return pl.pallas_call(
        functools.partial(_wkv_kernel, nb=nb),
        grid=(2, B // nb, N_CHUNKS),
        in_specs=[shared, shared, shared, perdir, perdir, perdir],
        out_specs=perdir,
        out_shape=jax.ShapeDtypeStruct((2, B, NT, RWKV_WIDTH), F32),
        scratch_shapes=[pltpu.VMEM((nb * RWKV_HEADS, RWKV_HEAD, RWKV_HEAD), F32)],
        compiler_params=_cp("parallel", "parallel", "arbitrary"),
        name="wkv7",
    )(r, v, kk, lw, bq, kd)


def _rwkv_finish_kernel(o_ref, r_ref, v_ref, g_ref, kd_ref, rk_ref, lg_ref, lb_ref, bd_ref, y_ref):
    bd = bd_ref[...]
    o = o_ref[0, 0] + o_ref[1, 0]
    mu = _headsum(o, bd) * (1.0 / RWKV_HEAD)
    xc = o - mu
    var = _headsum(xc * xc, bd) * (1.0 / RWKV_HEAD)
    on = xc * lax.rsqrt(var + LNX_EPS) * lg_ref[...] + lb_ref[...]
    bonus = _headsum(r_ref[0] * (kd_ref[0, 0] + kd_ref[1, 0]) * rk_ref[...], bd) * v_ref[0]
    y_ref[0] = ((on + bonus) * g_ref[0]).astype(y_ref.dtype)


def _rwkv_finish(o, r, v, g, kd, r_k, lnx_g, lnx_b, bd):
    B = r.shape[0]
    W = RWKV_WIDTH
    tok = pl.BlockSpec((1, TM, W), lambda b, t: (b, t, 0))
    tok2 = pl.BlockSpec((2, 1, TM, W), lambda b, t: (0, b, t, 0))
    row = pl.BlockSpec((1, W), lambda b, t: (0, 0))
    return pl.pallas_call(
        _rwkv_finish_kernel,
        grid=(B, N_TILES),
        in_specs=[tok2, tok, tok, tok, tok2, row, row, row, pl.BlockSpec((W, W), lambda b, t: (0, 0))],
        out_specs=tok,
        out_shape=jax.ShapeDtypeStruct((B, NT, W), BF16),
        compiler_params=_cp("parallel", "parallel"),
        name="rwkv_finish",
    )(o, r, v, g, kd, r_k.reshape(1, W), lnx_g.reshape(1, W), lnx_b.reshape(1, W), bd)


MLA_QW = MLA_HEADS * MLA_SLOT


def _mla_weights(w_uq, w_ukv, partner):
    dq = MLA_NOPE + MLA_ROPE
    wq = jnp.zeros((MLA_Q_RANK, 2 * MLA_QW), F32)
    wk = jnp.zeros((MLA_KV_RANK, MLA_QW), F32)
    wv = jnp.zeros((MLA_KV_RANK, MLA_HEADS * MLA_V), F32)
    place = np.zeros((128, MLA_QW), np.float32)
    for h in range(MLA_HEADS):
        qh = w_uq[:, h * dq:(h + 1) * dq]
        wq = wq.at[:, h * MLA_SLOT:h * MLA_SLOT + dq].set(qh)
        wq = wq.at[:, MLA_QW + h * MLA_SLOT + MLA_NOPE:MLA_QW + h * MLA_SLOT + dq].set(qh[:, MLA_NOPE + partner])
        kvh = w_ukv[:, h * (MLA_NOPE + MLA_V):(h + 1) * (MLA_NOPE + MLA_V)]
        wk = wk.at[:, h * MLA_SLOT:h * MLA_SLOT + MLA_NOPE].set(kvh[:, :MLA_NOPE])
        wv = wv.at[:, h * MLA_V:(h + 1) * MLA_V].set(kvh[:, MLA_NOPE:])
        place[np.arange(MLA_ROPE), h * MLA_SLOT + MLA_NOPE + np.arange(MLA_ROPE)] = 1.0
    return wq.astype(BF16), wk.astype(BF16), wv.astype(BF16), jnp.asarray(place).astype(BF16)


def _mla_proj_kernel(u_ref, qg_ref, kg_ref, wq_ref, wk_ref, wv_ref, pl_ref, cq_ref, sq_ref, ck_ref, sk_ref,
                     q_ref, k_ref, v_ref):
    u = u_ref[0]
    qn = _rmsnorm(u[:, :MLA_Q_RANK], qg_ref[...]).astype(BF16)
    kvn = _rmsnorm(u[:, MLA_Q_RANK:MLA_Q_RANK + MLA_KV_RANK], kg_ref[...]).astype(BF16)
    o = MLA_Q_RANK + MLA_KV_RANK
    kr = u[:, o:o + 128] * ck_ref[...] + u[:, o + 128:o + 256] * sk_ref[...]
    qq = _dot(qn, wq_ref[...])
    q_ref[0] = (qq[:, :MLA_QW] * cq_ref[...] + qq[:, MLA_QW:] * sq_ref[...]).astype(BF16)
    k_ref[0] = (_dot(kvn, wk_ref[...]) + _dot2(kr, pl_ref[...])).astype(BF16)
    v_ref[0] = _dot(kvn, wv_ref[...]).astype(BF16)


def _mla_proj(um, q_norm_g, kv_norm_g, weights, tables):
    B = um.shape[0]
    wq, wk, wv, place = weights
    cq, sq, ck, sk = tables
    tok = lambda n: pl.BlockSpec((1, TM, n), lambda b, t: (b, t, 0))
    tab = lambda n: pl.BlockSpec((TM, n), lambda b, t: (t, 0))
    full = lambda a: pl.BlockSpec(a.shape, lambda b, t: (0, 0))
    return pl.pallas_call(
        _mla_proj_kernel,
        grid=(B, N_TILES),
        in_specs=[tok(MLA_U), pl.BlockSpec((1, MLA_Q_RANK), lambda b, t: (0, 0)),
                  pl.BlockSpec((1, MLA_KV_RANK), lambda b, t: (0, 0)),
                  full(wq), full(wk), full(wv), full(place),
                  tab(MLA_QW), tab(MLA_QW), tab(MLA_SLOT), tab(MLA_SLOT)],
        out_specs=[tok(MLA_QW), tok(MLA_QW), tok(MLA_HEADS * MLA_V)],
        out_shape=[jax.ShapeDtypeStruct((B, NT, MLA_QW), BF16),
                   jax.ShapeDtypeStruct((B, NT, MLA_QW), BF16),
                   jax.ShapeDtypeStruct((B, NT, MLA_HEADS * MLA_V), BF16)],
        compiler_params=_cp("parallel", "parallel"),
        name="mla_proj",
    )(um, q_norm_g.reshape(1, -1), kv_norm_g.reshape(1, -1), wq, wk, wv, place, cq, sq, ck, sk)


def _mla_attn_kernel(q_ref, k_ref, v_ref, o_ref):
    t = pl.program_id(2)
    is_ctx = t == N_TILES - 1
    col = lax.broadcasted_iota(I32, (TM, NT), 1)
    valid = jnp.logical_not(is_ctx) | (col >= T_L)
    lo_lanes = lax.broadcasted_iota(I32, (TM, 128), 1) < MLA_V
    lo_rows = lax.broadcasted_iota(I32, (NT, 128), 1) < MLA_V
    vall = v_ref[0]
    one = jnp.ones((NT, 128), vall.dtype)
    vh = [jnp.where(lo_rows, vall, one), jnp.where(lo_rows, one, vall)]
    s = [_dot_nt(q_ref[0, :, hh * MLA_SLOT:(hh + 1) * MLA_SLOT], k_ref[0, :, hh * MLA_SLOT:(hh + 1) * MLA_SLOT])
         for hh in range(2)]
    e = []
    for hh in range(2):
        sm = jnp.where(valid, s[hh], -jnp.inf)
        e.append(jnp.exp(sm - jnp.max(sm, axis=-1, keepdims=True)).astype(BF16))
    r = [_dot(e[hh], vh[hh]) for hh in range(2)]
    num = jnp.where(lo_lanes, r[0], r[1])
    den = pltpu.roll(jnp.where(lo_lanes, r[1], r[0]), MLA_V, 1)
    o_ref[0] = (num * (1.0 / den)).astype(o_ref.dtype)


def _mla_attn(q, k, v):
    B = q.shape[0]
    return pl.pallas_call(
        _mla_attn_kernel,
        grid=(B, MLA_HEADS // 2, N_TILES),
        in_specs=[pl.BlockSpec((1, TM, 2 * MLA_SLOT), lambda b, p, t: (b, t, p)),
                  pl.BlockSpec((1, NT, 2 * MLA_SLOT), lambda b, p, t: (b, 0, p)),
                  pl.BlockSpec((1, NT, 2 * MLA_V), lambda b, p, t: (b, 0, p))],
        out_specs=pl.BlockSpec((1, TM, 2 * MLA_V), lambda b, p, t: (b, t, p)),
        out_shape=jax.ShapeDtypeStruct((B, NT, MLA_HEADS * MLA_V), BF16),
        compiler_params=_cp("parallel", "parallel", "parallel"),
        name="mla_attn",
    )(q, k, v)


def _final_norm_kernel(x_ref, g_ref, o_ref):
    o_ref[0] = _rmsnorm(x_ref[0], g_ref[...])


def _final_norm(x, g):
    B = x.shape[0]
    tok = pl.BlockSpec((1, TM, D), lambda b, t: (b, t, 0))
    return pl.pallas_call(
        _final_norm_kernel,
        grid=(B, N_LAT_TILES),
        in_specs=[tok, pl.BlockSpec((1, D), lambda b, t: (0, 0))],
        out_specs=tok,
        out_shape=jax.ShapeDtypeStruct((B, T_L, D), F32),
        compiler_params=_cp("parallel", "parallel"),
        name="final_norm",
    )(x, g.reshape(1, D))


def kernel(x, c, ctx, c_ctx, ada_w, ada_b, norm_mix_g, norm_ffn_g, router_w, exp_w_gate, exp_w_up, exp_w_down, ev_w_in, ev_w_out, ev_sink, ev_pool_w, ev_pool_scale, od_w_in, od_w_out, od_mu_prev, od_mu_next, od_w0, od_w2, od_a0, od_a2, od_g2, od_k_k, od_k_a, od_r_k, od_lnx_g, od_lnx_b, od_q_norm_g, od_w_uq, od_kv_norm_g, od_w_ukv, final_norm_g):
    depth = ada_w.shape[0]
    assert x.shape[1:] == (T_L, D) and ctx.shape[1:] == (T_C, D)
    mods = _modvec(c, c_ctx, ada_w, ada_b)
    ev_tables, ev_partner = _even_tables()
    mla_tables, mla_partner = _mla_tables()
    bd = _head_ones()
    xs = jnp.concatenate([x, ctx], axis=1)
    for l in range(depth):
        need_ctx = l < depth - 1
        i = l // 2
        mod = mods[l]
        if l % 2 == 0:
            q, kd, vd, pu = _inproj_even(xs, mod, norm_mix_g[l], _even_weights(ev_w_in[i], ev_partner), ev_tables)
            mix_a = _attn_even(q, kd, vd, ev_sink[i])
            mix_b = _pool(pu, ev_pool_w[i], ev_pool_scale[i])
            w_out = ev_w_out[i]
        else:
            ur, um = _inproj_odd(xs, mod, norm_mix_g[l], _odd_weights(od_w_in[i], mla_partner))
            r, v, kk, g, lw, bq, kdir = _rwkv_prep(ur, od_mu_prev[i], od_mu_next[i], od_w0[i], od_w2[i],
                                                   od_a0[i], od_a2[i], od_g2[i], od_k_k[i], od_k_a[i], bd)
            o = _wkv(r, v, kk, lw, bq, kdir)
            mix_a = _rwkv_finish(o, r, v, g, kdir, od_r_k[i], od_lnx_g[i], od_lnx_b[i], bd)
            mq, mk, mv = _mla_proj(um, od_q_norm_g[i], od_kv_norm_g[i],
                                   _mla_weights(od_w_uq[i], od_w_ukv[i], mla_partner), mla_tables)
            mix_b = _mla_attn(mq, mk, mv)
            w_out = od_w_out[i]
        xs, h, logits = _outproj(mix_a, mix_b, w_out, xs, mod, norm_ffn_g[l], router_w[l])
        xs = _moe(xs, h, logits, mod, exp_w_gate, exp_w_up, exp_w_down, l, need_ctx)
    return _final_norm(xs, final_norm_g)
```

```python
import functools

import numpy as np
import jax
import jax.numpy as jnp
from jax import lax
from jax.experimental import pallas as pl
from jax.experimental.pallas import tpu as pltpu

F32 = jnp.float32
BF16 = jnp.bfloat16
I32 = jnp.int32
HIGHEST = lax.Precision.HIGHEST

D = 1024
T_L = 2048
T_C = 256
NT = T_L + T_C
GRID_W = 64
EPS = 1e-6
ROPE_BASE = 10000.0

TM = 256
N_TILES = NT // TM
N_LAT_TILES = T_L // TM

A_HEADS = 8
A_KV_HEADS = 2
HEAD_DIM = 64
A_WINDOW = 128
A_BLOCK = 128
N_ABLK = NT // A_BLOCK
N_LAT_ABLK = T_L // A_BLOCK

POOL_WINDOWS = (2, 4, 8, 16)
POOL_GROUP = 128

RWKV_HEADS = 8
RWKV_HEAD = 64
RWKV_WIDTH = 512
RWKV_IN = 1920
LNX_EPS = 64e-5
WKV_CHUNK = 64
N_CHUNKS = NT // WKV_CHUNK
N_CTX_CHUNKS = T_C // WKV_CHUNK

MLA_HEADS = 8
MLA_Q_RANK = 256
MLA_KV_RANK = 128
MLA_NOPE = 64
MLA_ROPE = 32
MLA_V = 64
MLA_SLOT = 128
MLA_U = 640

N_EXPERTS = 16
EXPERT_FF = 2048
CAP_L = 2 * T_L // N_EXPERTS
CAP_C = 2 * T_C // N_EXPERTS
FF_CHUNK = 512

VMEM_LIMIT_BYTES = 56 * 1024 * 1024


def _cp(*sem):
    return pltpu.CompilerParams(dimension_semantics=sem, vmem_limit_bytes=VMEM_LIMIT_BYTES)


def _split_bf16(x):
    hi = x.astype(BF16)
    lo = (x - hi.astype(F32)).astype(BF16)
    return hi, lo


def _dot(a, b):
    return jnp.dot(a, b, preferred_element_type=F32)


def _dot_nt(a, b):
    return lax.dot_general(a, b, (((1,), (1,)), ((), ())), preferred_element_type=F32)


def _dot_tn(a, b):
    return lax.dot_general(a, b, (((0,), (0,)), ((), ())), preferred_element_type=F32)


def _dot2(a_f32, b_bf16):
    hi, lo = _split_bf16(a_f32)
    return _dot(hi, b_bf16) + _dot(lo, b_bf16)


def _modulate(x, g, shift, scale):
    ms = jnp.mean(x * x, axis=-1, keepdims=True)
    return x * lax.rsqrt(ms + EPS) * g * (1.0 + scale) + shift


def _rmsnorm(x, g):
    ms = jnp.mean(x * x, axis=-1, keepdims=True)
    return x * lax.rsqrt(ms + EPS) * g


def _modvec_kernel(a_ref, w_ref, b_ref, o_ref):
    a = a_ref[...]
    a = a * jax.nn.sigmoid(a)
    o_ref[0] = jnp.dot(a, w_ref[0], precision=HIGHEST, preferred_element_type=F32) + b_ref[0]


def _modvec(c, c_ctx, ada_w, ada_b):
    L = ada_w.shape[0]
    B = c.shape[0]
    assert B < 16
    rows = jnp.zeros((16, D), F32).at[:B].set(c).at[B].set(c_ctx)
    tn = 512
    out = pl.pallas_call(
        _modvec_kernel,
        grid=(L, 6 * D // tn),
        in_specs=[pl.BlockSpec((16, D), lambda l, n: (0, 0)),
                  pl.BlockSpec((1, D, tn), lambda l, n: (l, 0, n)),
                  pl.BlockSpec((1, 1, tn), lambda l, n: (l, 0, n))],
        out_specs=pl.BlockSpec((1, 16, tn), lambda l, n: (l, 0, n)),
        out_shape=jax.ShapeDtypeStruct((L, 16, 6 * D), F32),
        compiler_params=_cp("parallel", "parallel"),
        name="modvec",
    )(rows, ada_w, ada_b.reshape(L, 1, 6 * D))
    m = out.reshape(L, 16, 6, D)
    ml = m[:, :B]
    mc = jnp.broadcast_to(m[:, B:B + 1], (L, B, 6, D))
    mod = jnp.stack([ml, mc], axis=2)
    mod = jnp.pad(mod, ((0, 0), (0, 0), (0, 0), (0, 2), (0, 0)))
    return mod.reshape(L, B * 2, 8, D)


def _mod_spec():
    return pl.BlockSpec((1, 8, D), lambda b, t: (b * 2 + t // N_LAT_TILES, 0, 0))


def _rope_block(width, pos_row, pos_col):
    d = width // 2
    half = d // 2
    inv = ROPE_BASE ** (-np.arange(half, dtype=np.float32) / half)
    cos = np.zeros((pos_row.shape[0], width), np.float32)
    sin = np.zeros_like(cos)
    partner = np.zeros((width,), np.int64)
    for part, pos in ((0, pos_row), (1, pos_col)):
        ang = pos.astype(np.float32)[:, None] * inv[None, :]
        c, s = np.cos(ang), np.sin(ang)
        o = part * d
        cos[:, o:o + half] = c
        cos[:, o + half:o + d] = c
        sin[:, o:o + half] = -s
        sin[:, o + half:o + d] = s
        partner[o:o + half] = np.arange(o + half, o + d)
        partner[o + half:o + d] = np.arange(o, o + half)
    return cos, sin, partner


def _latent_positions():
    t = np.arange(T_L)
    return t // GRID_W, t % GRID_W


def _even_tables():
    cos, sin, partner = _rope_block(HEAD_DIM, *_latent_positions())
    scale = HEAD_DIM ** -0.5
    ones = np.ones((T_C, HEAD_DIM), np.float32)
    zeros = np.zeros((T_C, HEAD_DIM), np.float32)
    cos = np.concatenate([cos, ones], axis=0)
    sin = np.concatenate([sin, zeros], axis=0)
    cq = np.tile(cos, (1, A_HEADS)) * scale
    sq = np.tile(sin, (1, A_HEADS)) * scale
    ck = np.tile(cos, (1, 2 * A_KV_HEADS))
    sk = np.tile(sin, (1, 2 * A_KV_HEADS))
    return (jnp.asarray(cq), jnp.asarray(sq), jnp.asarray(ck), jnp.asarray(sk)), partner


def _mla_tables():
    cos, sin, partner = _rope_block(MLA_ROPE, *_latent_positions())
    cos = np.concatenate([cos, np.ones((T_C, MLA_ROPE), np.float32)], axis=0)
    sin = np.concatenate([sin, np.zeros((T_C, MLA_ROPE), np.float32)], axis=0)
    scale = (MLA_NOPE + MLA_ROPE) ** -0.5
    cq1 = np.zeros((NT, MLA_SLOT), np.float32)
    sq1 = np.zeros((NT, MLA_SLOT), np.float32)
    cq1[:, :MLA_NOPE] = 1.0
    cq1[:, MLA_NOPE:MLA_NOPE + MLA_ROPE] = cos
    sq1[:, MLA_NOPE:MLA_NOPE + MLA_ROPE] = sin
    cq = np.tile(cq1, (1, MLA_HEADS)) * scale
    sq = np.tile(sq1, (1, MLA_HEADS)) * scale
    ck = np.zeros((NT, MLA_SLOT), np.float32)
    sk = np.zeros((NT, MLA_SLOT), np.float32)
    ck[:, :MLA_ROPE] = cos
    sk[:, :MLA_ROPE] = sin
    return (jnp.asarray(cq), jnp.asarray(sq), jnp.asarray(ck), jnp.asarray(sk)), partner


EV_Q, EV_K, EV_V, EV_P = 512, 256, 256, 512
EV_N = 2 * EV_Q + 2 * EV_K + EV_V + EV_P


def _even_weights(w_in, partner):
    q = w_in[:, :512]
    k = w_in[:, 512:640]
    v = w_in[:, 640:768]
    p = w_in[:, 768:1280]
    pq = np.concatenate([h * HEAD_DIM + partner for h in range(A_HEADS)])
    pk = np.concatenate([h * HEAD_DIM + partner for h in range(A_KV_HEADS)])
    dup = np.concatenate([np.arange(HEAD_DIM), np.arange(HEAD_DIM),
                          HEAD_DIM + np.arange(HEAD_DIM), HEAD_DIM + np.arange(HEAD_DIM)])
    w = jnp.concatenate([q, q[:, pq], k[:, dup], k[:, pk][:, dup], v[:, dup], p], axis=1)
    return w.astype(BF16)


def _inproj_even_kernel(x_ref, mod_ref, g_ref, w_ref, cq_ref, sq_ref, ck_ref, sk_ref,
                        q_ref, k_ref, v_ref, p_ref):
    mod = mod_ref[0]
    h = _modulate(x_ref[0], g_ref[...], mod[0:1], mod[1:2])
    y = _dot(h.astype(BF16), w_ref[...])
    o = 0
    q = y[:, o:o + EV_Q] * cq_ref[...] + y[:, o + EV_Q:o + 2 * EV_Q] * sq_ref[...]
    o += 2 * EV_Q
    k = y[:, o:o + EV_K] * ck_ref[...] + y[:, o + EV_K:o + 2 * EV_K] * sk_ref[...]
    o += 2 * EV_K
    q_ref[0] = q.astype(BF16)
    k_ref[0] = k.astype(BF16)
    v_ref[0] = y[:, o:o + EV_V].astype(BF16)
    p_ref[0] = y[:, o + EV_V:o + EV_V + EV_P]


def _inproj_even(x, mod, g, w, tables):
    B = x.shape[0]
    cq, sq, ck, sk = tables
    tok = lambda n: pl.BlockSpec((1, TM, n), lambda b, t: (b, t, 0))
    tab = lambda n: pl.BlockSpec((TM, n), lambda b, t: (t, 0))
    return pl.pallas_call(
        _inproj_even_kernel,
        grid=(B, N_TILES),
        in_specs=[tok(D), _mod_spec(), pl.BlockSpec((1, D), lambda b, t: (0, 0)),
                  pl.BlockSpec((D, EV_N), lambda b, t: (0, 0)),
                  tab(EV_Q), tab(EV_Q), tab(EV_K), tab(EV_K)],
        out_specs=[tok(EV_Q), tok(EV_K), tok(EV_V), tok(EV_P)],
        out_shape=[jax.ShapeDtypeStruct((B, NT, EV_Q), BF16),
                   jax.ShapeDtypeStruct((B, NT, EV_K), BF16),
                   jax.ShapeDtypeStruct((B, NT, EV_V), BF16),
                   jax.ShapeDtypeStruct((B, NT, EV_P), F32)],
        compiler_params=_cp("parallel", "parallel"),
        name="inproj_even",
    )(x, mod, g.reshape(1, D), w, cq, sq, ck, sk)


def _attn_even_kernel(sink_ref, q_ref, kp_ref, kc_ref, kn_ref, kx_ref,
                      vp_ref, vc_ref, vn_ref, vx_ref, o_ref):
    j = pl.program_id(1)
    is_lat = j < N_LAT_ABLK
    q = q_ref[0]
    qpos = j * A_BLOCK + lax.broadcasted_iota(I32, (A_BLOCK, 3 * A_BLOCK), 0)
    kpos = (j - 1) * A_BLOCK + lax.broadcasted_iota(I32, (A_BLOCK, 3 * A_BLOCK), 1)
    valid = (jnp.abs(kpos - qpos) <= A_WINDOW) & (kpos >= 0) & (kpos < T_L) & is_lat
    lo_lanes = lax.broadcasted_iota(I32, (A_BLOCK, 128), 1) < HEAD_DIM
    kall, vall = [], []
    for kv in range(A_KV_HEADS):
        ks = slice(kv * 128, (kv + 1) * 128)
        kall.append(jnp.concatenate([kx_ref[0, :, ks], kp_ref[0, :, ks], kc_ref[0, :, ks], kn_ref[0, :, ks]], axis=0))
        vall.append(jnp.concatenate([vx_ref[0, :, ks], vp_ref[0, :, ks], vc_ref[0, :, ks], vn_ref[0, :, ks]], axis=0))
    heads = range(A_HEADS)
    group = A_HEADS // A_KV_HEADS
    qm = [jnp.where(lo_lanes if h % 2 == 0 else jnp.logical_not(lo_lanes), q[:, (h // 2) * 128:(h // 2 + 1) * 128],
                    jnp.zeros((A_BLOCK, 128), q.dtype)) for h in heads]
    s = [_dot_nt(qm[h], kall[h // group]) for h in heads]
    e, inv_den = [], []
    for h in heads:
        snk = sink_ref[h]
        s_ctx = s[h][:, :T_C]
        s_loc = jnp.where(valid, s[h][:, T_C:], -jnp.inf)
        m = jnp.maximum(jnp.max(s_ctx, axis=-1, keepdims=True), jnp.max(s_loc, axis=-1, keepdims=True))
        m = jnp.maximum(m, snk)
        e_ctx = jnp.exp(s_ctx - m)
        e_loc = jnp.exp(s_loc - m)
        den = (jnp.sum(e_ctx, axis=-1, keepdims=True) + jnp.sum(e_loc, axis=-1, keepdims=True)
               + jnp.exp(snk - m))
        e.append(jnp.concatenate([e_ctx, e_loc], axis=1).astype(BF16))
        inv_den.append(1.0 / den)
    o = [_dot(e[h], vall[h // group]) * inv_den[h] for h in heads]
    for p in range(A_HEADS // 2):
        o_ref[0, :, p * 128:(p + 1) * 128] = jnp.where(lo_lanes, o[2 * p], o[2 * p + 1]).astype(o_ref.dtype)


def _attn_even(q, kd, vd, sink):
    B = q.shape[0]
    last = N_LAT_ABLK - 1
    blk = lambda f: pl.BlockSpec((1, A_BLOCK, 256), f)
    prev = lambda b, j: (b, jnp.clip(j - 1, 0, last), 0)
    cur = lambda b, j: (b, j, 0)
    nxt = lambda b, j: (b, jnp.clip(j + 1, 0, last), 0)
    ctx = pl.BlockSpec((1, T_C, 256), lambda b, j: (b, T_L // T_C, 0))
    return pl.pallas_call(
        _attn_even_kernel,
        grid=(B, N_ABLK),
        in_specs=[pl.BlockSpec(memory_space=pltpu.SMEM),
                  pl.BlockSpec((1, A_BLOCK, EV_Q), cur),
                  blk(prev), blk(cur), blk(nxt), ctx,
                  blk(prev), blk(cur), blk(nxt), ctx],
        out_specs=pl.BlockSpec((1, A_BLOCK, 512), cur),
        out_shape=jax.ShapeDtypeStruct((B, NT, 512), BF16),
        compiler_params=_cp("parallel", "parallel"),
        name="attn_even",
    )(sink, q, kd, kd, kd, kd, vd, vd, vd, vd)


def _pool_kernel(up_ref, uc_ref, un_ref, w_ref, sc_ref, o_ref):
    j = pl.program_id(1)
    is_lat = j < N_LAT_ABLK
    seg_lo = jnp.where(is_lat, 0, T_L)
    seg_hi = jnp.where(is_lat, T_L, NT)
    u3 = jnp.concatenate([up_ref[0], uc_ref[0], un_ref[0]], axis=0)
    uc = uc_ref[0]
    r = j * A_BLOCK + lax.broadcasted_iota(I32, (A_BLOCK, 3 * A_BLOCK), 0)
    c = (j - 1) * A_BLOCK + lax.broadcasted_iota(I32, (A_BLOCK, 3 * A_BLOCK), 1)
    r1 = j * A_BLOCK + lax.broadcasted_iota(I32, (A_BLOCK, 1), 0)
    outs = []
    for gi, w in enumerate(POOL_WINDOWS):
        cs = slice(gi * POOL_GROUP, (gi + 1) * POOL_GROUP)
        lo = jnp.maximum(r - w // 2, seg_lo)
        hi = jnp.minimum(r + w // 2, seg_hi)
        band = ((c >= lo) & (c < hi)).astype(BF16)
        cnt = (jnp.minimum(r1 + w // 2, seg_hi) - jnp.maximum(r1 - w // 2, seg_lo)).astype(F32)
        hi_u, lo_u = _split_bf16(u3[:, cs])
        mean = (_dot(band, hi_u) + _dot(band, lo_u)) / cnt
        outs.append(_dot((mean - uc[:, cs]).astype(BF16), w_ref[gi]))
    o_ref[0] = (jnp.concatenate(outs, axis=1) * sc_ref[...]).astype(o_ref.dtype)


def _pool(u, pool_w, pool_scale):
    B = u.shape[0]
    blk = lambda f: pl.BlockSpec((1, A_BLOCK, 512), f)
    return pl.pallas_call(
        _pool_kernel,
        grid=(B, N_ABLK),
        in_specs=[blk(lambda b, j: (b, jnp.maximum(j - 1, 0), 0)),
                  blk(lambda b, j: (b, j, 0)),
                  blk(lambda b, j: (b, jnp.minimum(j + 1, N_ABLK - 1), 0)),
                  pl.BlockSpec((4, POOL_GROUP, POOL_GROUP), lambda b, j: (0, 0, 0)),
                  pl.BlockSpec((1, 512), lambda b, j: (0, 0))],
        out_specs=blk(lambda b, j: (b, j, 0)),
        out_shape=jax.ShapeDtypeStruct((B, NT, 512), BF16),
        compiler_params=_cp("parallel", "parallel"),
        name="pool",
    )(u, u, u, pool_w.astype(BF16), pool_scale.reshape(1, 512))


def _outproj_kernel(a_ref, p_ref, w_ref, x_ref, mod_ref, g_ref, wr_ref, xo_ref, h_ref, lg_ref):
    mod = mod_ref[0]
    y = _dot(a_ref[0], w_ref[0:512, :]) + _dot(p_ref[0], w_ref[512:1024, :])
    x = x_ref[0] + mod[2:3] * y
    xo_ref[0] = x
    h = _modulate(x, g_ref[...], mod[3:4], mod[4:5])
    w_hi, w_lo = _split_bf16(wr_ref[...])
    h_hi, h_lo = _split_bf16(h)
    h_ref[0] = h_hi
    lg_ref[0] = _dot_nt(w_hi, h_hi) + _dot_nt(w_lo, h_hi) + _dot_nt(w_hi, h_lo)


def _outproj(a, p, w_out, x, mod, g_ffn, w_router):
    B = x.shape[0]
    tok = lambda n: pl.BlockSpec((1, TM, n), lambda b, t: (b, t, 0))
    return pl.pallas_call(
        _outproj_kernel,
        grid=(B, N_TILES),
        in_specs=[tok(512), tok(512), pl.BlockSpec((D, D), lambda b, t: (0, 0)), tok(D), _mod_spec(),
                  pl.BlockSpec((1, D), lambda b, t: (0, 0)),
                  pl.BlockSpec((N_EXPERTS, D), lambda b, t: (0, 0))],
        out_specs=[tok(D), tok(D), pl.BlockSpec((1, N_EXPERTS, TM), lambda b, t: (b, 0, t))],
        out_shape=[jax.ShapeDtypeStruct((B, NT, D), F32),
                   jax.ShapeDtypeStruct((B, NT, D), BF16),
                   jax.ShapeDtypeStruct((B, N_EXPERTS, NT), F32)],
        compiler_params=_cp("parallel", "parallel"),
        name="outproj",
    )(a, p, w_out.astype(BF16), x, mod, g_ffn.reshape(1, D), w_router.T)


def _topk_kernel(lg_ref, sid_ref, gate_ref, *, cap):
    B, E, T = lg_ref.shape
    R = B * E
    lg = lg_ref[...]
    m = jnp.max(lg, axis=1, keepdims=True)
    e = jnp.exp(lg - m)
    aff = (e / jnp.sum(e, axis=1, keepdims=True)).reshape(R, T)
    bits = lax.bitcast_convert_type(aff, I32)

    def step(i, thr):
        cand = thr | jnp.left_shift(jnp.int32(1), 30 - i)
        cnt = jnp.sum((bits >= cand).astype(F32), axis=1, keepdims=True)
        return jnp.where(cnt >= cap, cand, thr)

    thr = lax.fori_loop(0, 31, step, jnp.zeros((R, 1), I32))
    gt = bits > thr
    eq = bits == thr
    need = cap - jnp.sum(gt.astype(F32), axis=1, keepdims=True)

    blk = 256
    tri = (lax.broadcasted_iota(I32, (blk, blk), 0) <= lax.broadcasted_iota(I32, (blk, blk), 1)).astype(BF16)

    def prefix(mask_f32):
        off = jnp.zeros((R, 1), F32)
        parts = []
        for k in range(T // blk):
            cb = _dot(mask_f32[:, k * blk:(k + 1) * blk].astype(BF16), tri) + off
            parts.append(cb)
            off = cb[:, blk - 1:blk]
        return parts

    eq_parts = prefix(eq.astype(F32))
    sel_parts = []
    for k in range(T // blk):
        sl = slice(k * blk, (k + 1) * blk)
        sel_parts.append(gt[:, sl] | (eq[:, sl] & (eq_parts[k] <= need)))
    sel = jnp.concatenate(sel_parts, axis=1)
    pos_parts = prefix(sel.astype(F32))
    for k in range(T // blk):
        sl = slice(k * blk, (k + 1) * blk)
        sid = jnp.where(sel_parts[k], pos_parts[k].astype(I32) - 1, -1)
        sid_ref[:, :, sl] = sid.reshape(B, E, blk)
        gate_ref[:, :, sl] = jnp.where(sel_parts[k], aff[:, sl], 0.0).reshape(B, E, blk)


def _topk(logits, cap):
    B, E, T = logits.shape
    full = pl.BlockSpec((B, E, T), lambda i: (0, 0, 0))
    return pl.pallas_call(
        functools.partial(_topk_kernel, cap=cap),
        grid=(1,),
        in_specs=[full],
        out_specs=[full, full],
        out_shape=[jax.ShapeDtypeStruct((B, E, T), I32), jax.ShapeDtypeStruct((B, E, T), F32)],
        compiler_params=_cp("arbitrary"),
        name=f"topk_{T}",
    )(logits)


def _gather_kernel(h_ref, sid_ref, gate_ref, xs_ref, gs_ref, *, cap, chunk):
    T = h_ref.shape[1]
    acc = jnp.zeros((cap, D), F32)
    gacc = jnp.zeros((cap, 1), F32)
    slot = lax.broadcasted_iota(I32, (cap, chunk), 0)
    for k in range(T // chunk):
        sl = slice(k * chunk, (k + 1) * chunk)
        hit = slot == sid_ref[0, 0, :, sl]
        acc = acc + _dot(hit.astype(BF16), h_ref[0, sl, :])
        gacc = gacc + jnp.sum(jnp.where(hit, gate_ref[0, 0, :, sl], 0.0), axis=1, keepdims=True)
    xs_ref[0] = acc.astype(BF16)
    gs_ref[0] = jnp.broadcast_to(gacc, (cap, 128))


def _gather(h, sid, gate, cap, row_block):
    B, E, T = sid.shape
    chunk = min(T, 512)
    idx = lambda b, e: (b, e, 0, 0)
    return pl.pallas_call(
        functools.partial(_gather_kernel, cap=cap, chunk=chunk),
        grid=(B, E),
        in_specs=[pl.BlockSpec((1, T, D), lambda b, e: (b, row_block, 0)),
                  pl.BlockSpec((1, 1, 1, T), idx), pl.BlockSpec((1, 1, 1, T), idx)],
        out_specs=[pl.BlockSpec((1, cap, D), lambda b, e: (e, b, 0)),
                   pl.BlockSpec((1, cap, 128), lambda b, e: (e, b, 0))],
        out_shape=[jax.ShapeDtypeStruct((E, B * cap, D), BF16),
                   jax.ShapeDtypeStruct((E, B * cap, 128), F32)],
        compiler_params=_cp("parallel", "arbitrary"),
        name=f"gather_{T}",
    )(h, sid.reshape(B, E, 1, T), gate.reshape(B, E, 1, T))


DOWN_CHUNK = 256


def _ffn_kernel(*refs, row_tiles):
    n_seg = len(row_tiles)
    xs_refs = refs[:n_seg]
    gs_refs = refs[n_seg:2 * n_seg]
    wg_ref, wu_ref, wd_ref = refs[2 * n_seg:2 * n_seg + 3]
    y_refs = refs[2 * n_seg + 3:3 * n_seg + 3]
    hid_refs = refs[3 * n_seg + 3:]
    s = pl.program_id(1)
    nf = EXPERT_FF // FF_CHUNK

    @pl.when(s < nf)
    def _():
        wg = wg_ref[0].astype(BF16)
        wu = wu_ref[0].astype(BF16)
        for xs_ref, hid_ref, rt in zip(xs_refs, hid_refs, row_tiles):
            for r0 in range(0, xs_ref.shape[1], rt):
                rows = pl.ds(r0, rt)
                x = xs_ref[0, rows, :]
                g = _dot(x, wg)
                u = _dot(x, wu)
                hid_ref[s, rows, :] = (g * jax.nn.sigmoid(g) * u).astype(BF16)

    @pl.when(s >= nf)
    def _():
        d = s - nf
        wd = wd_ref[0].astype(BF16)
        for gs_ref, y_ref, hid_ref, rt in zip(gs_refs, y_refs, hid_refs, row_tiles):
            for r0 in range(0, hid_ref.shape[1], rt):
                rows = pl.ds(r0, rt)
                acc = _dot(hid_ref[0, rows, :], wd[0:FF_CHUNK, :])
                for f in range(1, nf):
                    acc = acc + _dot(hid_ref[f, rows, :], wd[f * FF_CHUNK:(f + 1) * FF_CHUNK, :])
                gs = gs_ref[0, rows, :]
                y_ref[0, d, rows, :] = jnp.concatenate(
                    [acc[:, k * 128:(k + 1) * 128] * gs for k in range(DOWN_CHUNK // 128)], axis=1).astype(y_ref.dtype)


def _ffn(xs_list, gs_list, w_gate, w_up, w_down, layer):
    E = w_gate.shape[1]
    row_tiles = tuple(min(xs.shape[1], 512) for xs in xs_list)
    nf = EXPERT_FF // FF_CHUNK
    nd = D // DOWN_CHUNK
    once = dict(pipeline_mode=pl.Buffered(1))
    in_specs = [pl.BlockSpec((1, xs.shape[1], D), lambda e, s: (e, 0, 0), **once) for xs in xs_list]
    in_specs += [pl.BlockSpec((1, gs.shape[1], 128), lambda e, s: (e, 0, 0), **once) for gs in gs_list]
    up_map = lambda e, s: (layer, e, 0, jnp.minimum(s, nf - 1))
    in_specs += [pl.BlockSpec((None, 1, D, FF_CHUNK), up_map),
                 pl.BlockSpec((None, 1, D, FF_CHUNK), up_map),
                 pl.BlockSpec((None, 1, EXPERT_FF, DOWN_CHUNK), lambda e, s: (layer, e, 0, jnp.maximum(s - nf, 0)))]
    return pl.pallas_call(
        functools.partial(_ffn_kernel, row_tiles=row_tiles),
        grid=(E, nf + nd),
        in_specs=in_specs,
        out_specs=[pl.BlockSpec((1, nd, xs.shape[1], DOWN_CHUNK), lambda e, s: (e, 0, 0, 0)) for xs in xs_list],
        out_shape=[jax.ShapeDtypeStruct((E, nd, xs.shape[1], DOWN_CHUNK), BF16) for xs in xs_list],
        scratch_shapes=[pltpu.VMEM((nf, xs.shape[1], FF_CHUNK), BF16) for xs in xs_list],
        compiler_params=_cp("parallel", "arbitrary"),
        name="expert_ffn",
    )(*xs_list, *gs_list, w_gate, w_up, w_down)


def _combine_kernel(x_ref, sid_ref, y_ref, mod_ref, o_ref, *, cap):
    tbs = x_ref.shape[1]
    E = sid_ref.shape[1]
    slot = lax.broadcasted_iota(I32, (cap, tbs), 0)
    hit = jnp.concatenate([(slot == sid_ref[0, e]).astype(BF16) for e in range(E)], axis=0)
    gate = mod_ref[0][5:6]
    for d in range(D // DOWN_CHUNK):
        cs = slice(d * DOWN_CHUNK, (d + 1) * DOWN_CHUNK)
        z = _dot_tn(hit, y_ref[:, d].reshape(E * cap, DOWN_CHUNK))
        o_ref[0, :, cs] = x_ref[0, :, cs] + gate[:, cs] * z


def _combine(x, sid, y, mod, cap, tbs, row_block0, segment):
    B, E, T = sid.shape
    ntb = T // tbs
    xmap = lambda b, tb: (b, row_block0 + tb, 0)
    return pl.pallas_call(
        functools.partial(_combine_kernel, cap=cap),
        grid=(B, ntb),
        in_specs=[pl.BlockSpec((1, tbs, D), xmap),
                  pl.BlockSpec((1, E, 1, tbs), lambda b, tb: (b, 0, 0, tb)),
                  pl.BlockSpec((E, D // DOWN_CHUNK, cap, DOWN_CHUNK), lambda b, tb: (0, 0, b, 0)),
                  pl.BlockSpec((1, 8, D), lambda b, tb: (b * 2 + segment, 0, 0))],
        out_specs=pl.BlockSpec((1, tbs, D), xmap),
        out_shape=jax.ShapeDtypeStruct(x.shape, F32),
        input_output_aliases={0: 0},
        compiler_params=_cp("parallel", "parallel"),
        name=f"combine_{T}",
    )(x, sid.reshape(B, E, 1, T), y, mod)


def _moe(x, h, logits, mod, w_gate, w_up, w_down, layer, need_ctx):
    sid_l, gate_l = _topk(logits[:, :, :T_L], CAP_L)
    xs_l, gs_l = _gather(h, sid_l, gate_l, CAP_L, 0)
    if need_ctx:
        sid_c, gate_c = _topk(logits[:, :, T_L:], CAP_C)
        xs_c, gs_c = _gather(h, sid_c, gate_c, CAP_C, T_L // T_C)
        y_l, y_c = _ffn([xs_l, xs_c], [gs_l, gs_c], w_gate, w_up, w_down, layer)
    else:
        (y_l,) = _ffn([xs_l], [gs_l], w_gate, w_up, w_down, layer)
    x = _combine(x, sid_l, y_l, mod, CAP_L, 512, 0, 0)
    if need_ctx:
        x = _combine(x, sid_c, y_c, mod, CAP_C, T_C, T_L // T_C, 1)
    return x


OD_N = RWKV_IN + MLA_U


def _odd_weights(w_in, partner):
    rw = w_in[:, :RWKV_IN]
    qd = w_in[:, RWKV_IN:RWKV_IN + MLA_Q_RANK]
    kvd = w_in[:, RWKV_IN + MLA_Q_RANK:RWKV_IN + MLA_Q_RANK + MLA_KV_RANK]
    kr = w_in[:, RWKV_IN + MLA_Q_RANK + MLA_KV_RANK:]
    pad = jnp.zeros((D, 128 - MLA_ROPE), w_in.dtype)
    w = jnp.concatenate([rw, qd, kvd, kr, pad, kr[:, partner], pad], axis=1)
    return w.astype(BF16)


def _inproj_odd_kernel(x_ref, mod_ref, g_ref, w_ref, ur_ref, um_ref):
    mod = mod_ref[0]
    h = _modulate(x_ref[0], g_ref[...], mod[0:1], mod[1:2])
    y = _dot(h.astype(BF16), w_ref[...])
    ur_ref[0] = y[:, :RWKV_IN]
    um_ref[0] = y[:, RWKV_IN:]


def _inproj_odd(x, mod, g, w):
    B = x.shape[0]
    tok = lambda n: pl.BlockSpec((1, TM, n), lambda b, t: (b, t, 0))
    return pl.pallas_call(
        _inproj_odd_kernel,
        grid=(B, N_TILES),
        in_specs=[tok(D), _mod_spec(), pl.BlockSpec((1, D), lambda b, t: (0, 0)),
                  pl.BlockSpec((D, OD_N), lambda b, t: (0, 0))],
        out_specs=[tok(RWKV_IN), tok(MLA_U)],
        out_shape=[jax.ShapeDtypeStruct((B, NT, RWKV_IN), F32),
                   jax.ShapeDtypeStruct((B, NT, MLA_U), F32)],
        compiler_params=_cp("parallel", "parallel"),
        name="inproj_odd",
    )(x, mod, g.reshape(1, D), w)


def _head_ones():
    h = np.arange(RWKV_WIDTH) // RWKV_HEAD
    return jnp.asarray((h[:, None] == h[None, :]).astype(np.float32)).astype(BF16)


def _headsum(x, bd):
    return _dot2(x, bd)


def _rwkv_prep_kernel(u_ref, up_ref, un_ref, mup_ref, mun_ref, w0_ref, w2_ref, a0_ref, a2_ref, g2_ref,
                      kk_ref, ka_ref, bd_ref,
                      r_ref, v_ref, kkn_ref, g_ref, lw_ref, bq_ref, kd_ref):
    t = pl.program_id(1)
    u = u_ref[0]
    rows = lax.broadcasted_iota(I32, (TM, 1), 0)
    seg_start = (t == 0) | (t == N_LAT_TILES)
    seg_end = (t == N_LAT_TILES - 1) | (t == N_TILES - 1)
    prev_row = jnp.where(seg_start, 0.0, up_ref[0, 7:8, :])
    next_row = jnp.where(seg_end, 0.0, un_ref[0, 0:1, :])
    prev = jnp.where(rows == 0, prev_row, pltpu.roll(u, 1, 0))
    nxt = jnp.where(rows == TM - 1, next_row, pltpu.roll(u, TM - 1, 0))
    us = u + mup_ref[...] * (prev - u) + mun_ref[...] * (nxt - u)
    W = RWKV_WIDTH
    r = us[:, :W]
    k = us[:, W:2 * W]
    v = us[:, 2 * W:3 * W]
    wl = jnp.tanh(us[:, 3 * W:3 * W + 128]).astype(BF16)
    al = us[:, 3 * W + 128:3 * W + 256].astype(BF16)
    gl = jax.nn.sigmoid(us[:, 3 * W + 256:3 * W + 384]).astype(BF16)
    bd = bd_ref[...]
    kk = k * kk_ref[...]
    kk = kk * lax.rsqrt(jnp.maximum(_headsum(kk * kk, bd), 1e-24))
    r_ref[0] = r
    v_ref[0] = v
    kkn_ref[0] = kk
    g_ref[0] = _dot(gl, g2_ref[...])
    for d in range(2):
        z = w0_ref[d:d + 1, :] + _dot(wl, w2_ref[d])
        a = jax.nn.sigmoid(a0_ref[d:d + 1, :] + _dot(al, a2_ref[d]))
        lw_ref[d, 0] = jax.nn.sigmoid(z) * (-float(np.exp(-0.5)))
        bq_ref[d, 0] = kk * a
        kd_ref[d, 0] = k * (1.0 + (a - 1.0) * ka_ref[...])


def _rwkv_prep(ur, mu_prev, mu_next, w0, w2, a0, a2, g2, k_k, k_a, bd):
    B = ur.shape[0]
    W = RWKV_WIDTH
    z = jnp.zeros((64, W), F32)
    w2p = jnp.stack([jnp.concatenate([w2[0], z]), jnp.concatenate([z, w2[1]])]).astype(BF16)
    a2p = jnp.stack([jnp.concatenate([a2[0], z]), jnp.concatenate([z, a2[1]])]).astype(BF16)
    tok = lambda n: pl.BlockSpec((1, TM, n), lambda b, t: (b, t, 0))
    tok2 = pl.BlockSpec((2, 1, TM, W), lambda b, t: (0, b, t, 0))
    row = lambda n: pl.BlockSpec((1, n), lambda b, t: (0, 0))
    full = lambda *s: pl.BlockSpec(s, lambda b, t: (0,) * len(s))
    slab = TM // 8
    f32 = lambda: jax.ShapeDtypeStruct((B, NT, W), F32)
    f32d = lambda: jax.ShapeDtypeStruct((2, B, NT, W), F32)
    return pl.pallas_call(
        _rwkv_prep_kernel,
        grid=(B, N_TILES),
        in_specs=[tok(RWKV_IN),
                  pl.BlockSpec((1, 8, RWKV_IN), lambda b, t: (b, jnp.maximum(t * slab - 1, 0), 0)),
                  pl.BlockSpec((1, 8, RWKV_IN), lambda b, t: (b, jnp.minimum((t + 1) * slab, NT // 8 - 1), 0)),
                  row(RWKV_IN), row(RWKV_IN), full(2, W), full(2, 128, W), full(2, W), full(2, 128, W),
                  full(128, W), row(W), row(W), full(W, W)],
        out_specs=[tok(W), tok(W), tok(W), tok(W), tok2, tok2, tok2],
        out_shape=[f32(), f32(), f32(), f32(), f32d(), f32d(), f32d()],
        compiler_params=_cp("parallel", "parallel"),
        name="rwkv_prep",
    )(ur, ur, ur, mu_prev.reshape(1, RWKV_IN), mu_next.reshape(1, RWKV_IN), w0, w2p, a0, a2p,
      g2.astype(BF16), k_k.reshape(1, W), k_a.reshape(1, W), bd)


def _wkv_kernel(r_ref, v_ref, kk_ref, lw_ref, bq_ref, kd_ref, o_ref, s_ref, *, nb):
    d = pl.program_id(0)
    i = pl.program_id(2)
    C = WKV_CHUNK
    H = RWKV_HEADS
    G = nb * H

    @pl.when(i == 0)
    def _():
        s_ref[...] = jnp.zeros_like(s_ref)

    sgn = 1 - 2 * d
    rel = (lax.broadcasted_iota(I32, (C, C), 0) - lax.broadcasted_iota(I32, (C, C), 1)) * sgn
    incl = rel >= 0
    strict = rel > 0
    eye = (rel == 0).astype(F32)
    incl_b = incl.astype(BF16)
    row_i = lax.broadcasted_iota(I32, (C, C), 0)
    col_i = lax.broadcasted_iota(I32, (C, C), 1)

    def same_block(k):
        return (row_i >> k) == (col_i >> k)


    at, rt, bt, kt, a0, r0, bh, kh, v, p_end = ([] for _ in range(10))
    for b in range(nb):
        lw = lw_ref[0, b]
        lw_hi, lw_lo = _split_bf16(lw)
        ci = _dot(incl_b, lw_hi) + _dot(incl_b, lw_lo)
        ce = ci - lw
        tot = jnp.sum(lw, axis=0, keepdims=True)
        m = 0.5 * tot
        r = r_ref[b]
        kk = kk_ref[b]
        bq = bq_ref[0, b]
        kd = kd_ref[0, b]
        e_ci = jnp.exp(ci - m)
        e_mci = jnp.exp(m - ci)
        e_end = jnp.exp(tot - ci)
        full = ((at, (-kk * jnp.exp(ce - m)).astype(BF16)), (rt, (r * e_ci).astype(BF16)),
                (bt, (bq * e_mci).astype(BF16)), (kt, (kd * e_mci).astype(BF16)),
                (a0, (-kk * jnp.exp(ce)).astype(BF16)), (r0, (r * (e_ci * jnp.exp(m))).astype(BF16)),
                (bh, (bq * e_end).astype(BF16)), (kh, (kd * e_end).astype(BF16)),
                (v, v_ref[b].astype(BF16)), (p_end, jnp.exp(tot)))
        for dst, x in full:
            dst.extend(x[:, h * RWKV_HEAD:(h + 1) * RWKV_HEAD] for h in range(H))

    S = [s_ref[g] for g in range(G)]
    s_b = [s.astype(BF16) for s in S]
    n_ab = [jnp.where(strict, _dot_nt(at[g], bt[g]), 0.0) for g in range(G)]
    a_ak = [jnp.where(strict, _dot_nt(at[g], kt[g]), 0.0).astype(BF16) for g in range(G)]
    a_rb = [jnp.where(incl, _dot_nt(rt[g], bt[g]), 0.0).astype(BF16) for g in range(G)]
    a_rk = [jnp.where(incl, _dot_nt(rt[g], kt[g]), 0.0).astype(BF16) for g in range(G)]
    w_rhs = [_dot_nt(a0[g], s_b[g]) + _dot(a_ak[g], v[g]) for g in range(G)]
    inv = [eye + jnp.where(same_block(1), n, 0.0) for n in n_ab]
    for k in range(1, 6):
        off_mask = same_block(k + 1) & jnp.logical_not(same_block(k))
        inv_b = [x.astype(BF16) for x in inv]
        e = [_dot(jnp.where(off_mask, n, 0.0).astype(BF16), x) for n, x in zip(n_ab, inv_b)]
        inv = [x + _dot(xb, y.astype(BF16)) for x, xb, y in zip(inv, inv_b, e)]
    U = [_dot(inv[g].astype(BF16), w_rhs[g].astype(BF16)).astype(BF16) for g in range(G)]
    for g in range(G):
        o = _dot_nt(r0[g], s_b[g]) + _dot(a_rb[g], U[g]) + _dot(a_rk[g], v[g])
        o_ref[0, g // H, :, (g % H) * RWKV_HEAD:(g % H + 1) * RWKV_HEAD] = o
    for g in range(G):
        s_ref[g] = S[g] * p_end[g] + _dot_tn(U[g], bh[g]) + _dot_tn(v[g], kh[g])


WKV_BATCH = 8


def _wkv(r, v, kk, lw, bq, kd, nb=WKV_BATCH):
    B = r.shape[0]
    C = WKV_CHUNK
    n_lat = T_L // C

    def chunk(d, i):
        fwd = jnp.where(i < N_CTX_CHUNKS, n_lat + i, i - N_CTX_CHUNKS)
        return jnp.where(d == 0, fwd, N_CHUNKS - 1 - i)

    shared = pl.BlockSpec((nb, C, RWKV_WIDTH), lambda d, b, i: (b, chunk(d, i), 0))
    perdir = pl.BlockSpec((1, nb, C, RWKV_WIDTH), lambda d, b, i: (d, b, chunk(d, i), 0))
    return pl.pallas_call(
        functools.partial(_wkv_kernel, nb=nb),
        grid=(2, B // nb, N_CHUNKS),
        in_specs=[shared, shared, shared, perdir, perdir, perdir],
        out_specs=perdir,
        out_shape=jax.ShapeDtypeStruct((2, B, NT, RWKV_WIDTH), F32),
        scratch_shapes=[pltpu.VMEM((nb * RWKV_HEADS, RWKV_HEAD, RWKV_HEAD), F32)],
        compiler_params=_cp("parallel", "parallel", "arbitrary"),
        name="wkv7",
    )(r, v, kk, lw, bq, kd)


def _rwkv_finish_kernel(o_ref, r_ref, v_ref, g_ref, kd_ref, rk_ref, lg_ref, lb_ref, bd_ref, y_ref):
    bd = bd_ref[...]
    o = o_ref[0, 0] + o_ref[1, 0]
    mu = _headsum(o, bd) * (1.0 / RWKV_HEAD)
    xc = o - mu
    var = _headsum(xc * xc, bd) * (1.0 / RWKV_HEAD)
    on = xc * lax.rsqrt(var + LNX_EPS) * lg_ref[...] + lb_ref[...]
    bonus = _headsum(r_ref[0] * (kd_ref[0, 0] + kd_ref[1, 0]) * rk_ref[...], bd) * v_ref[0]
    y_ref[0] = ((on + bonus) * g_ref[0]).astype(y_ref.dtype)


def _rwkv_finish(o, r, v, g, kd, r_k, lnx_g, lnx_b, bd):
    B = r.shape[0]
    W = RWKV_WIDTH
    tok = pl.BlockSpec((1, TM, W), lambda b, t: (b, t, 0))
    tok2 = pl.BlockSpec((2, 1, TM, W), lambda b, t: (0, b, t, 0))
    row = pl.BlockSpec((1, W), lambda b, t: (0, 0))
    return pl.pallas_call(
        _rwkv_finish_kernel,
        grid=(B, N_TILES),
        in_specs=[tok2, tok, tok, tok, tok2, row, row, row, pl.BlockSpec((W, W), lambda b, t: (0, 0))],
        out_specs=tok,
        out_shape=jax.ShapeDtypeStruct((B, NT, W), BF16),
        compiler_params=_cp("parallel", "parallel"),
        name="rwkv_finish",
    )(o, r, v, g, kd, r_k.reshape(1, W), lnx_g.reshape(1, W), lnx_b.reshape(1, W), bd)


MLA_QW = MLA_HEADS * MLA_SLOT


def _mla_weights(w_uq, w_ukv, partner):
    dq = MLA_NOPE + MLA_ROPE
    wq = jnp.zeros((MLA_Q_RANK, 2 * MLA_QW), F32)
    wk = jnp.zeros((MLA_KV_RANK, MLA_QW), F32)
    wv = jnp.zeros((MLA_KV_RANK, MLA_HEADS * MLA_V), F32)
    place = np.zeros((128, MLA_QW), np.float32)
    for h in range(MLA_HEADS):
        qh = w_uq[:, h * dq:(h + 1) * dq]
        wq = wq.at[:, h * MLA_SLOT:h * MLA_SLOT + dq].set(qh)
        wq = wq.at[:, MLA_QW + h * MLA_SLOT + MLA_NOPE:MLA_QW + h * MLA_SLOT + dq].set(qh[:, MLA_NOPE + partner])
        kvh = w_ukv[:, h * (MLA_NOPE + MLA_V):(h + 1) * (MLA_NOPE + MLA_V)]
        wk = wk.at[:, h * MLA_SLOT:h * MLA_SLOT + MLA_NOPE].set(kvh[:, :MLA_NOPE])
        wv = wv.at[:, h * MLA_V:(h + 1) * MLA_V].set(kvh[:, MLA_NOPE:])
        place[np.arange(MLA_ROPE), h * MLA_SLOT + MLA_NOPE + np.arange(MLA_ROPE)] = 1.0
    return wq.astype(BF16), wk.astype(BF16), wv.astype(BF16), jnp.asarray(place).astype(BF16)


def _mla_proj_kernel(u_ref, qg_ref, kg_ref, wq_ref, wk_ref, wv_ref, pl_ref, cq_ref, sq_ref, ck_ref, sk_ref,
                     q_ref, k_ref, v_ref):
    u = u_ref[0]
    qn = _rmsnorm(u[:, :MLA_Q_RANK], qg_ref[...]).astype(BF16)
    kvn = _rmsnorm(u[:, MLA_Q_RANK:MLA_Q_RANK + MLA_KV_RANK], kg_ref[...]).astype(BF16)
    o = MLA_Q_RANK + MLA_KV_RANK
    kr = u[:, o:o + 128] * ck_ref[...] + u[:, o + 128:o + 256] * sk_ref[...]
    qq = _dot(qn, wq_ref[...])
    q_ref[0] = (qq[:, :MLA_QW] * cq_ref[...] + qq[:, MLA_QW:] * sq_ref[...]).astype(BF16)
    k_ref[0] = (_dot(kvn, wk_ref[...]) + _dot2(kr, pl_ref[...])).astype(BF16)
    v_ref[0] = _dot(kvn, wv_ref[...]).astype(BF16)


def _mla_proj(um, q_norm_g, kv_norm_g, weights, tables):
    B = um.shape[0]
    wq, wk, wv, place = weights
    cq, sq, ck, sk = tables
    tok = lambda n: pl.BlockSpec((1, TM, n), lambda b, t: (b, t, 0))
    tab = lambda n: pl.BlockSpec((TM, n), lambda b, t: (t, 0))
    full = lambda a: pl.BlockSpec(a.shape, lambda b, t: (0, 0))
    return pl.pallas_call(
        _mla_proj_kernel,
        grid=(B, N_TILES),
        in_specs=[tok(MLA_U), pl.BlockSpec((1, MLA_Q_RANK), lambda b, t: (0, 0)),
                  pl.BlockSpec((1, MLA_KV_RANK), lambda b, t: (0, 0)),
                  full(wq), full(wk), full(wv), full(place),
                  tab(MLA_QW), tab(MLA_QW), tab(MLA_SLOT), tab(MLA_SLOT)],
        out_specs=[tok(MLA_QW), tok(MLA_QW), tok(MLA_HEADS * MLA_V)],
        out_shape=[jax.ShapeDtypeStruct((B, NT, MLA_QW), BF16),
                   jax.ShapeDtypeStruct((B, NT, MLA_QW), BF16),
                   jax.ShapeDtypeStruct((B, NT, MLA_HEADS * MLA_V), BF16)],
        compiler_params=_cp("parallel", "parallel"),
        name="mla_proj",
    )(um, q_norm_g.reshape(1, -1), kv_norm_g.reshape(1, -1), wq, wk, wv, place, cq, sq, ck, sk)


def _mla_attn_kernel(q_ref, k_ref, v_ref, o_ref):
    t = pl.program_id(2)
    is_ctx = t == N_TILES - 1
    col = lax.broadcasted_iota(I32, (TM, NT), 1)
    valid = jnp.logical_not(is_ctx) | (col >= T_L)
    lo_lanes = lax.broadcasted_iota(I32, (TM, 128), 1) < MLA_V
    lo_rows = lax.broadcasted_iota(I32, (NT, 128), 1) < MLA_V
    vall = v_ref[0]
    one = jnp.ones((NT, 128), vall.dtype)
    vh = [jnp.where(lo_rows, vall, one), jnp.where(lo_rows, one, vall)]
    s = [_dot_nt(q_ref[0, :, hh * MLA_SLOT:(hh + 1) * MLA_SLOT], k_ref[0, :, hh * MLA_SLOT:(hh + 1) * MLA_SLOT])
         for hh in range(2)]
    e = []
    for hh in range(2):
        sm = jnp.where(valid, s[hh], -jnp.inf)
        e.append(jnp.exp(sm - jnp.max(sm, axis=-1, keepdims=True)).astype(BF16))
    r = [_dot(e[hh], vh[hh]) for hh in range(2)]
    num = jnp.where(lo_lanes, r[0], r[1])
    den = pltpu.roll(jnp.where(lo_lanes, r[1], r[0]), MLA_V, 1)
    o_ref[0] = (num * (1.0 / den)).astype(o_ref.dtype)


def _mla_attn(q, k, v):
    B = q.shape[0]
    return pl.pallas_call(
        _mla_attn_kernel,
        grid=(B, MLA_HEADS // 2, N_TILES),
        in_specs=[pl.BlockSpec((1, TM, 2 * MLA_SLOT), lambda b, p, t: (b, t, p)),
                  pl.BlockSpec((1, NT, 2 * MLA_SLOT), lambda b, p, t: (b, 0, p)),
                  pl.BlockSpec((1, NT, 2 * MLA_V), lambda b, p, t: (b, 0, p))],
        out_specs=pl.BlockSpec((1, TM, 2 * MLA_V), lambda b, p, t: (b, t, p)),
        out_shape=jax.ShapeDtypeStruct((B, NT, MLA_HEADS * MLA_V), BF16),
        compiler_params=_cp("parallel", "parallel", "parallel"),
        name="mla_attn",
    )(q, k, v)


def _final_norm_kernel(x_ref, g_ref, o_ref):
    o_ref[0] = _rmsnorm(x_ref[0], g_ref[...])


def _final_norm(x, g):
    B = x.shape[0]
    tok = pl.BlockSpec((1, TM, D), lambda b, t: (b, t, 0))
    return pl.pallas_call(
        _final_norm_kernel,
        grid=(B, N_LAT_TILES),
        in_specs=[tok, pl.BlockSpec((1, D), lambda b, t: (0, 0))],
        out_specs=tok,
        out_shape=jax.ShapeDtypeStruct((B, T_L, D), F32),
        compiler_params=_cp("parallel", "parallel"),
        name="final_norm",
    )(x, g.reshape(1, D))


def kernel(x, c, ctx, c_ctx, ada_w, ada_b, norm_mix_g, norm_ffn_g, router_w, exp_w_gate, exp_w_up, exp_w_down, ev_w_in, ev_w_out, ev_sink, ev_pool_w, ev_pool_scale, od_w_in, od_w_out, od_mu_prev, od_mu_next, od_w0, od_w2, od_a0, od_a2, od_g2, od_k_k, od_k_a, od_r_k, od_lnx_g, od_lnx_b, od_q_norm_g, od_w_uq, od_kv_norm_g, od_w_ukv, final_norm_g):
    depth = ada_w.shape[0]
    assert x.shape[1:] == (T_L, D) and ctx.shape[1:] == (T_C, D)
    mods = _modvec(c, c_ctx, ada_w, ada_b)
    ev_tables, ev_partner = _even_tables()
    mla_tables, mla_partner = _mla_tables()
    bd = _head_ones()
    xs = jnp.concatenate([x, ctx], axis=1)
    for l in range(depth):
        need_ctx = l < depth - 1
        i = l // 2
        mod = mods[l]
        if l % 2 == 0:
            q, kd, vd, pu = _inproj_even(xs, mod, norm_mix_g[l], _even_weights(ev_w_in[i], ev_partner), ev_tables)
            mix_a = _attn_even(q, kd, vd, ev_sink[i])
            mix_b = _pool(pu, ev_pool_w[i], ev_pool_scale[i])
            w_out = ev_w_out[i]
        else:
            ur, um = _inproj_odd(xs, mod, norm_mix_g[l], _odd_weights(od_w_in[i], mla_partner))
            r, v, kk, g, lw, bq, kdir = _rwkv_prep(ur, od_mu_prev[i], od_mu_next[i], od_w0[i], od_w2[i],
                                                   od_a0[i], od_a2[i], od_g2[i], od_k_k[i], od_k_a[i], bd)
            o = _wkv(r, v, kk, lw, bq, kdir)
            mix_a = _rwkv_finish(o, r, v, g, kdir, od_r_k[i], od_lnx_g[i], od_lnx_b[i], bd)
            mq, mk, mv = _mla_proj(um, od_q_norm_g[i], od_kv_norm_g[i],
                                   _mla_weights(od_w_uq[i], od_w_ukv[i], mla_partner), mla_tables)
            mix_b = _mla_attn(mq, mk, mv)
            w_out = od_w_out[i]
        xs, h, logits = _outproj(mix_a, mix_b, w_out, xs, mod, norm_ffn_g[l], router_w[l])
        xs = _moe(xs, h, logits, mod, exp_w_gate, exp_w_up, exp_w_down, l, need_ctx)
    return _final_norm(xs, final_norm_g)
```

```python
import functools

import numpy as np
import jax
import jax.numpy as jnp
from jax import lax
from jax.experimental import pallas as pl
from jax.experimental.pallas import tpu as pltpu

F32 = jnp.float32
BF16 = jnp.bfloat16
I32 = jnp.int32
HIGHEST = lax.Precision.HIGHEST

D = 1024
T_L = 2048
T_C = 256
NT = T_L + T_C
GRID_W = 64
EPS = 1e-6
ROPE_BASE = 10000.0

TM = 512
N_LAT_TILES = T_L // TM
N_TILES = N_LAT_TILES + 1
assert T_C <= TM and T_L % TM == 0

A_HEADS = 8
A_KV_HEADS = 2
HEAD_DIM = 64
A_WINDOW = 128
A_BLOCK = 128
N_ABLK = NT // A_BLOCK
N_LAT_ABLK = T_L // A_BLOCK

POOL_WINDOWS = (2, 4, 8, 16)
POOL_GROUP = 128

RWKV_HEADS = 8
RWKV_HEAD = 64
RWKV_WIDTH = 512
RWKV_IN = 1920
LNX_EPS = 64e-5
WKV_CHUNK = 64
N_CHUNKS = NT // WKV_CHUNK
N_CTX_CHUNKS = T_C // WKV_CHUNK

MLA_HEADS = 8
MLA_Q_RANK = 256
MLA_KV_RANK = 128
MLA_NOPE = 64
MLA_ROPE = 32
MLA_V = 64
MLA_SLOT = 128
MLA_U = 640

N_EXPERTS = 16
EXPERT_FF = 2048
CAP_L = 2 * T_L // N_EXPERTS
CAP_C = 2 * T_C // N_EXPERTS
FF_CHUNK = 512

VMEM_LIMIT_BYTES = 56 * 1024 * 1024


def _cp(*sem):
    return pltpu.CompilerParams(dimension_semantics=sem, vmem_limit_bytes=VMEM_LIMIT_BYTES)


def _split_bf16(x):
    hi = x.astype(BF16)
    lo = (x - hi.astype(F32)).astype(BF16)
    return hi, lo


def _dot(a, b):
    return jnp.dot(a, b, preferred_element_type=F32)


def _dot_nt(a, b):
    return lax.dot_general(a, b, (((1,), (1,)), ((), ())), preferred_element_type=F32)


def _dot_tn(a, b):
    return lax.dot_general(a, b, (((0,), (0,)), ((), ())), preferred_element_type=F32)


def _dot2(a_f32, b_bf16):
    hi, lo = _split_bf16(a_f32)
    return _dot(hi, b_bf16) + _dot(lo, b_bf16)


def _modulate(x, g, shift, scale):
    ms = jnp.mean(x * x, axis=-1, keepdims=True)
    return x * lax.rsqrt(ms + EPS) * g * (1.0 + scale) + shift


def _rmsnorm(x, g):
    ms = jnp.mean(x * x, axis=-1, keepdims=True)
    return x * lax.rsqrt(ms + EPS) * g


def _modvec_kernel(a_ref, w_ref, b_ref, o_ref):
    a = a_ref[...]
    a = a * jax.nn.sigmoid(a)
    o_ref[0] = jnp.dot(a, w_ref[0], precision=HIGHEST, preferred_element_type=F32) + b_ref[0]


def _modvec(c, c_ctx, ada_w, ada_b):
    L = ada_w.shape[0]
    B = c.shape[0]
    assert B < 16
    rows = jnp.zeros((16, D), F32).at[:B].set(c).at[B].set(c_ctx)
    tn = 512
    out = pl.pallas_call(
        _modvec_kernel,
        grid=(L, 6 * D // tn),
        in_specs=[pl.BlockSpec((16, D), lambda l, n: (0, 0)),
                  pl.BlockSpec((1, D, tn), lambda l, n: (l, 0, n)),
                  pl.BlockSpec((1, 1, tn), lambda l, n: (l, 0, n))],
        out_specs=pl.BlockSpec((1, 16, tn), lambda l, n: (l, 0, n)),
        out_shape=jax.ShapeDtypeStruct((L, 16, 6 * D), F32),
        compiler_params=_cp("parallel", "parallel"),
        name="modvec",
    )(rows, ada_w, ada_b.reshape(L, 1, 6 * D))
    m = out.reshape(L, 16, 6, D)
    ml = m[:, :B]
    mc = jnp.broadcast_to(m[:, B:B + 1], (L, B, 6, D))
    mod = jnp.stack([ml, mc], axis=2)
    mod = jnp.pad(mod, ((0, 0), (0, 0), (0, 0), (0, 2), (0, 0)))
    return mod.reshape(L, B * 2, 8, D)


def _mod_spec():
    return pl.BlockSpec((1, 8, D), lambda b, t: (b * 2 + t // N_LAT_TILES, 0, 0))


def _rope_block(width, pos_row, pos_col):
    d = width // 2
    half = d // 2
    inv = ROPE_BASE ** (-np.arange(half, dtype=np.float32) / half)
    cos = np.zeros((pos_row.shape[0], width), np.float32)
    sin = np.zeros_like(cos)
    partner = np.zeros((width,), np.int64)
    for part, pos in ((0, pos_row), (1, pos_col)):
        ang = pos.astype(np.float32)[:, None] * inv[None, :]
        c, s = np.cos(ang), np.sin(ang)
        o = part * d
        cos[:, o:o + half] = c
        cos[:, o + half:o + d] = c
        sin[:, o:o + half] = -s
        sin[:, o + half:o + d] = s
        partner[o:o + half] = np.arange(o + half, o + d)
        partner[o + half:o + d] = np.arange(o, o + half)
    return cos, sin, partner


def _latent_positions():
    t = np.arange(T_L)
    return t // GRID_W, t % GRID_W


def _even_tables():
    cos, sin, partner = _rope_block(HEAD_DIM, *_latent_positions())
    scale = HEAD_DIM ** -0.5
    ones = np.ones((T_C, HEAD_DIM), np.float32)
    zeros = np.zeros((T_C, HEAD_DIM), np.float32)
    cos = np.concatenate([cos, ones], axis=0)
    sin = np.concatenate([sin, zeros], axis=0)
    cq = np.tile(cos, (1, A_HEADS)) * scale
    sq = np.tile(sin, (1, A_HEADS)) * scale
    ck = np.tile(cos, (1, 2 * A_KV_HEADS))
    sk = np.tile(sin, (1, 2 * A_KV_HEADS))
    return (jnp.asarray(cq), jnp.asarray(sq), jnp.asarray(ck), jnp.asarray(sk)), partner


def _mla_tables():
    cos, sin, partner = _rope_block(MLA_ROPE, *_latent_positions())
    cos = np.concatenate([cos, np.ones((T_C, MLA_ROPE), np.float32)], axis=0)
    sin = np.concatenate([sin, np.zeros((T_C, MLA_ROPE), np.float32)], axis=0)
    scale = (MLA_NOPE + MLA_ROPE) ** -0.5
    cq1 = np.zeros((NT, MLA_SLOT), np.float32)
    sq1 = np.zeros((NT, MLA_SLOT), np.float32)
    cq1[:, :MLA_NOPE] = 1.0
    cq1[:, MLA_NOPE:MLA_NOPE + MLA_ROPE] = cos
    sq1[:, MLA_NOPE:MLA_NOPE + MLA_ROPE] = sin
    cq = np.tile(cq1, (1, MLA_HEADS)) * scale
    sq = np.tile(sq1, (1, MLA_HEADS)) * scale
    ck = np.zeros((NT, MLA_SLOT), np.float32)
    sk = np.zeros((NT, MLA_SLOT), np.float32)
    ck[:, :MLA_ROPE] = cos
    sk[:, :MLA_ROPE] = sin
    return (jnp.asarray(cq), jnp.asarray(sq), jnp.asarray(ck), jnp.asarray(sk)), partner


EV_Q, EV_K, EV_V, EV_P = 512, 256, 256, 512
EV_N = 2 * EV_Q + 2 * EV_K + EV_V + EV_P


def _even_weights(w_in, partner):
    q = w_in[:, :512]
    k = w_in[:, 512:640]
    v = w_in[:, 640:768]
    p = w_in[:, 768:1280]
    pq = np.concatenate([h * HEAD_DIM + partner for h in range(A_HEADS)])
    pk = np.concatenate([h * HEAD_DIM + partner for h in range(A_KV_HEADS)])
    dup = np.concatenate([np.arange(HEAD_DIM), np.arange(HEAD_DIM),
                          HEAD_DIM + np.arange(HEAD_DIM), HEAD_DIM + np.arange(HEAD_DIM)])
    w = jnp.concatenate([q, q[:, pq], k[:, dup], k[:, pk][:, dup], v[:, dup], p], axis=1)
    return w.astype(BF16)


def _inproj_even_kernel(x_ref, mod_ref, g_ref, w_ref, cq_ref, sq_ref, ck_ref, sk_ref,
                        q_ref, k_ref, v_ref, p_ref):
    mod = mod_ref[0]
    h = _modulate(x_ref[0], g_ref[...], mod[0:1], mod[1:2])
    y = _dot(h.astype(BF16), w_ref[...])
    o = 0
    q = y[:, o:o + EV_Q] * cq_ref[...] + y[:, o + EV_Q:o + 2 * EV_Q] * sq_ref[...]
    o += 2 * EV_Q
    k = y[:, o:o + EV_K] * ck_ref[...] + y[:, o + EV_K:o + 2 * EV_K] * sk_ref[...]
    o += 2 * EV_K
    q_ref[0] = q.astype(BF16)
    k_ref[0] = k.astype(BF16)
    v_ref[0] = y[:, o:o + EV_V].astype(BF16)
    p_ref[0] = y[:, o + EV_V:o + EV_V + EV_P]


def _inproj_even(x, mod, g, w, tables):
    B = x.shape[0]
    cq, sq, ck, sk = tables
    tok = lambda n: pl.BlockSpec((1, TM, n), lambda b, t: (b, t, 0))
    tab = lambda n: pl.BlockSpec((TM, n), lambda b, t: (t, 0))
    return pl.pallas_call(
        _inproj_even_kernel,
        grid=(B, N_TILES),
        in_specs=[tok(D), _mod_spec(), pl.BlockSpec((1, D), lambda b, t: (0, 0)),
                  pl.BlockSpec((D, EV_N), lambda b, t: (0, 0)),
                  tab(EV_Q), tab(EV_Q), tab(EV_K), tab(EV_K)],
        out_specs=[tok(EV_Q), tok(EV_K), tok(EV_V), tok(EV_P)],
        out_shape=[jax.ShapeDtypeStruct((B, NT, EV_Q), BF16),
                   jax.ShapeDtypeStruct((B, NT, EV_K), BF16),
                   jax.ShapeDtypeStruct((B, NT, EV_V), BF16),
                   jax.ShapeDtypeStruct((B, NT, EV_P), F32)],
        compiler_params=_cp("parallel", "parallel"),
        name="inproj_even",
    )(x, mod, g.reshape(1, D), w, cq, sq, ck, sk)


def _attn_even_kernel(sink_ref, q_ref, kp_ref, kc_ref, kn_ref, kx_ref,
                      vp_ref, vc_ref, vn_ref, vx_ref, o_ref):
    j = pl.program_id(1)
    is_lat = j < N_LAT_ABLK
    q = q_ref[0]
    qpos = j * A_BLOCK + lax.broadcasted_iota(I32, (A_BLOCK, 3 * A_BLOCK), 0)
    kpos = (j - 1) * A_BLOCK + lax.broadcasted_iota(I32, (A_BLOCK, 3 * A_BLOCK), 1)
    valid = (jnp.abs(kpos - qpos) <= A_WINDOW) & (kpos >= 0) & (kpos < T_L) & is_lat
    lo_lanes = lax.broadcasted_iota(I32, (A_BLOCK, 128), 1) < HEAD_DIM
    kall, vall = [], []
    for kv in range(A_KV_HEADS):
        ks = slice(kv * 128, (kv + 1) * 128)
        kall.append(jnp.concatenate([kx_ref[0, :, ks], kp_ref[0, :, ks], kc_ref[0, :, ks], kn_ref[0, :, ks]], axis=0))
        vall.append(jnp.concatenate([vx_ref[0, :, ks], vp_ref[0, :, ks], vc_ref[0, :, ks], vn_ref[0, :, ks]], axis=0))
    heads = range(A_HEADS)
    group = A_HEADS // A_KV_HEADS
    qm = [jnp.where(lo_lanes if h % 2 == 0 else jnp.logical_not(lo_lanes), q[:, (h // 2) * 128:(h // 2 + 1) * 128],
                    jnp.zeros((A_BLOCK, 128), q.dtype)) for h in heads]
    s = [_dot_nt(qm[h], kall[h // group]) for h in heads]
    e, inv_den = [], []
    for h in heads:
        snk = sink_ref[h]
        s_ctx = s[h][:, :T_C]
        s_loc = jnp.where(valid, s[h][:, T_C:], -jnp.inf)
        m = jnp.maximum(jnp.max(s_ctx, axis=-1, keepdims=True), jnp.max(s_loc, axis=-1, keepdims=True))
        m = jnp.maximum(m, snk)
        e_ctx = jnp.exp(s_ctx - m)
        e_loc = jnp.exp(s_loc - m)
        den = (jnp.sum(e_ctx, axis=-1, keepdims=True) + jnp.sum(e_loc, axis=-1, keepdims=True)
               + jnp.exp(snk - m))
        e.append(jnp.concatenate([e_ctx, e_loc], axis=1).astype(BF16))
        inv_den.append(1.0 / den)
    o = [_dot(e[h], vall[h // group]) * inv_den[h] for h in heads]
    for p in range(A_HEADS // 2):
        o_ref[0, :, p * 128:(p + 1) * 128] = jnp.where(lo_lanes, o[2 * p], o[2 * p + 1]).astype(o_ref.dtype)


def _attn_even(q, kd, vd, sink):
    B = q.shape[0]
    last = N_LAT_ABLK - 1
    blk = lambda f: pl.BlockSpec((1, A_BLOCK, 256), f)
    prev = lambda b, j: (b, jnp.clip(j - 1, 0, last), 0)
    cur = lambda b, j: (b, j, 0)
    nxt = lambda b, j: (b, jnp.clip(j + 1, 0, last), 0)
    ctx = pl.BlockSpec((1, T_C, 256), lambda b, j: (b, T_L // T_C, 0))
    return pl.pallas_call(
        _attn_even_kernel,
        grid=(B, N_ABLK),
        in_specs=[pl.BlockSpec(memory_space=pltpu.SMEM),
                  pl.BlockSpec((1, A_BLOCK, EV_Q), cur),
                  blk(prev), blk(cur), blk(nxt), ctx,
                  blk(prev), blk(cur), blk(nxt), ctx],
        out_specs=pl.BlockSpec((1, A_BLOCK, 512), cur),
        out_shape=jax.ShapeDtypeStruct((B, NT, 512), BF16),
        compiler_params=_cp("parallel", "parallel"),
        name="attn_even",
    )(sink, q, kd, kd, kd, kd, vd, vd, vd, vd)


def _pool_kernel(up_ref, uc_ref, un_ref, w_ref, sc_ref, o_ref):
    j = pl.program_id(1)
    is_lat = j < N_LAT_ABLK
    seg_lo = jnp.where(is_lat, 0, T_L)
    seg_hi = jnp.where(is_lat, T_L, NT)
    u3 = jnp.concatenate([up_ref[0], uc_ref[0], un_ref[0]], axis=0)
    uc = uc_ref[0]
    r = j * A_BLOCK + lax.broadcasted_iota(I32, (A_BLOCK, 3 * A_BLOCK), 0)
    c = (j - 1) * A_BLOCK + lax.broadcasted_iota(I32, (A_BLOCK, 3 * A_BLOCK), 1)
    r1 = j * A_BLOCK + lax.broadcasted_iota(I32, (A_BLOCK, 1), 0)
    outs = []
    for gi, w in enumerate(POOL_WINDOWS):
        cs = slice(gi * POOL_GROUP, (gi + 1) * POOL_GROUP)
        lo = jnp.maximum(r - w // 2, seg_lo)
        hi = jnp.minimum(r + w // 2, seg_hi)
        band = ((c >= lo) & (c < hi)).astype(BF16)
        cnt = (jnp.minimum(r1 + w // 2, seg_hi) - jnp.maximum(r1 - w // 2, seg_lo)).astype(F32)
        hi_u, lo_u = _split_bf16(u3[:, cs])
        mean = (_dot(band, hi_u) + _dot(band, lo_u)) / cnt
        outs.append(_dot((mean - uc[:, cs]).astype(BF16), w_ref[gi]))
    o_ref[0] = (jnp.concatenate(outs, axis=1) * sc_ref[...]).astype(o_ref.dtype)


def _pool(u, pool_w, pool_scale):
    B = u.shape[0]
    blk = lambda f: pl.BlockSpec((1, A_BLOCK, 512), f)
    return pl.pallas_call(
        _pool_kernel,
        grid=(B, N_ABLK),
        in_specs=[blk(lambda b, j: (b, jnp.maximum(j - 1, 0), 0)),
                  blk(lambda b, j: (b, j, 0)),
                  blk(lambda b, j: (b, jnp.minimum(j + 1, N_ABLK - 1), 0)),
                  pl.BlockSpec((4, POOL_GROUP, POOL_GROUP), lambda b, j: (0, 0, 0)),
                  pl.BlockSpec((1, 512), lambda b, j: (0, 0))],
        out_specs=blk(lambda b, j: (b, j, 0)),
        out_shape=jax.ShapeDtypeStruct((B, NT, 512), BF16),
        compiler_params=_cp("parallel", "parallel"),
        name="pool",
    )(u, u, u, pool_w.astype(BF16), pool_scale.reshape(1, 512))


def _outproj_kernel(a_ref, p_ref, w_ref, x_ref, mod_ref, g_ref, wr_ref, xo_ref, h_ref, lg_ref):
    mod = mod_ref[0]
    y = _dot(a_ref[0], w_ref[0:512, :]) + _dot(p_ref[0], w_ref[512:1024, :])
    x = x_ref[0] + mod[2:3] * y
    xo_ref[0] = x
    h = _modulate(x, g_ref[...], mod[3:4], mod[4:5])
    w_hi, w_lo = _split_bf16(wr_ref[...])
    h_hi, h_lo = _split_bf16(h)
    h_ref[0] = h_hi
    lg_ref[0] = _dot_nt(w_hi, h_hi) + _dot_nt(w_lo, h_hi) + _dot_nt(w_hi, h_lo)


def _outproj(a, p, w_out, x, mod, g_ffn, w_router):
    B = x.shape[0]
    tok = lambda n: pl.BlockSpec((1, TM, n), lambda b, t: (b, t, 0))
    return pl.pallas_call(
        _outproj_kernel,
        grid=(B, N_TILES),
        in_specs=[tok(512), tok(512), pl.BlockSpec((D, D), lambda b, t: (0, 0)), tok(D), _mod_spec(),
                  pl.BlockSpec((1, D), lambda b, t: (0, 0)),
                  pl.BlockSpec((N_EXPERTS, D), lambda b, t: (0, 0))],
        out_specs=[tok(D), tok(D), pl.BlockSpec((1, N_EXPERTS, TM), lambda b, t: (b, 0, t))],
        out_shape=[jax.ShapeDtypeStruct((B, NT, D), F32),
                   jax.ShapeDtypeStruct((B, NT, D), BF16),
                   jax.ShapeDtypeStruct((B, N_EXPERTS, NT), F32)],
        compiler_params=_cp("parallel", "parallel"),
        name="outproj",
    )(a, p, w_out.astype(BF16), x, mod, g_ffn.reshape(1, D), w_router.T)


def _topk_kernel(lg_ref, sid_ref, gate_ref, *, cap):
    B, E, T = lg_ref.shape
    R = B * E
    lg = lg_ref[...]
    m = jnp.max(lg, axis=1, keepdims=True)
    e = jnp.exp(lg - m)
    aff = (e / jnp.sum(e, axis=1, keepdims=True)).reshape(R, T)
    bits = lax.bitcast_convert_type(aff, I32)

    def step(i, thr):
        cand = thr | jnp.left_shift(jnp.int32(1), 30 - i)
        cnt = jnp.sum((bits >= cand).astype(F32), axis=1, keepdims=True)
        return jnp.where(cnt >= cap, cand, thr)

    thr = lax.fori_loop(0, 31, step, jnp.zeros((R, 1), I32))
    gt = bits > thr
    eq = bits == thr
    need = cap - jnp.sum(gt.astype(F32), axis=1, keepdims=True)

    blk = 256
    tri = (lax.broadcasted_iota(I32, (blk, blk), 0) <= lax.broadcasted_iota(I32, (blk, blk), 1)).astype(BF16)

    def prefix(mask_f32):
        off = jnp.zeros((R, 1), F32)
        parts = []
        for k in range(T // blk):
            cb = _dot(mask_f32[:, k * blk:(k + 1) * blk].astype(BF16), tri) + off
            parts.append(cb)
            off = cb[:, blk - 1:blk]
        return parts

    eq_parts = prefix(eq.astype(F32))
    sel_parts = []
    for k in range(T // blk):
        sl = slice(k * blk, (k + 1) * blk)
        sel_parts.append(gt[:, sl] | (eq[:, sl] & (eq_parts[k] <= need)))
    sel = jnp.concatenate(sel_parts, axis=1)
    pos_parts = prefix(sel.astype(F32))
    for k in range(T // blk):
        sl = slice(k * blk, (k + 1) * blk)
        sid = jnp.where(sel_parts[k], pos_parts[k].astype(I32) - 1, -1)
        sid_ref[:, :, sl] = sid.reshape(B, E, blk)
        gate_ref[:, :, sl] = jnp.where(sel_parts[k], aff[:, sl], 0.0).reshape(B, E, blk)


def _topk(logits, cap):
    B, E, T = logits.shape
    full = pl.BlockSpec((B, E, T), lambda i: (0, 0, 0))
    return pl.pallas_call(
        functools.partial(_topk_kernel, cap=cap),
        grid=(1,),
        in_specs=[full],
        out_specs=[full, full],
        out_shape=[jax.ShapeDtypeStruct((B, E, T), I32), jax.ShapeDtypeStruct((B, E, T), F32)],
        compiler_params=_cp("arbitrary"),
        name=f"topk_{T}",
    )(logits)


GATHER_COLS = 512


def _gather_kernel(h_ref, sid_ref, gate_ref, xs_ref, gs_ref, *, cap, chunk):
    T = h_ref.shape[1]
    E = sid_ref.shape[1]
    acc = jnp.zeros((E * cap, GATHER_COLS), F32)
    slot = lax.broadcasted_iota(I32, (cap, chunk), 0)
    for k in range(T // chunk):
        sl = slice(k * chunk, (k + 1) * chunk)
        hit = jnp.concatenate([(slot == sid_ref[0, e, :, sl]).astype(BF16) for e in range(E)], axis=0)
        acc = acc + _dot(hit, h_ref[0, sl, :])
    for e in range(E):
        xs_ref[e] = acc[e * cap:(e + 1) * cap].astype(BF16)

    @pl.when(pl.program_id(1) == 0)
    def _():
        slot_t = lax.broadcasted_iota(I32, (cap, T), 0)
        for e in range(E):
            picked = jnp.where(slot_t == sid_ref[0, e], gate_ref[0, e], 0.0)
            gs_ref[e] = jnp.broadcast_to(jnp.sum(picked, axis=1, keepdims=True), (cap, 128))


def _gather(h, sid, gate, cap, row_block):
    B, E, T = sid.shape
    chunk = min(T, 512)
    idx = lambda b, j: (b, 0, 0, 0)
    return pl.pallas_call(
        functools.partial(_gather_kernel, cap=cap, chunk=chunk),
        grid=(B, D // GATHER_COLS),
        in_specs=[pl.BlockSpec((1, T, GATHER_COLS), lambda b, j: (b, row_block, j)),
                  pl.BlockSpec((1, E, 1, T), idx), pl.BlockSpec((1, E, 1, T), idx)],
        out_specs=[pl.BlockSpec((E, cap, GATHER_COLS), lambda b, j: (0, b, j)),
                   pl.BlockSpec((E, cap, 128), lambda b, j: (0, b, 0))],
        out_shape=[jax.ShapeDtypeStruct((E, B * cap, D), BF16),
                   jax.ShapeDtypeStruct((E, B * cap, 128), F32)],
        compiler_params=_cp("parallel", "arbitrary"),
        name=f"gather_{T}",
    )(h, sid.reshape(B, E, 1, T), gate.reshape(B, E, 1, T))


DOWN_CHUNK = 256


def _ffn_kernel(*refs, row_tiles):
    n_seg = len(row_tiles)
    xs_refs = refs[:n_seg]
    gs_refs = refs[n_seg:2 * n_seg]
    wg_ref, wu_ref, wd_ref = refs[2 * n_seg:2 * n_seg + 3]
    y_refs = refs[2 * n_seg + 3:3 * n_seg + 3]
    hid_refs = refs[3 * n_seg + 3:]
    s = pl.program_id(1)
    nf = EXPERT_FF // FF_CHUNK

    @pl.when(s < nf)
    def _():
        wg = wg_ref[0].astype(BF16)
        wu = wu_ref[0].astype(BF16)
        for xs_ref, hid_ref, rt in zip(xs_refs, hid_refs, row_tiles):
            for r0 in range(0, xs_ref.shape[1], rt):
                rows = pl.ds(r0, rt)
                x = xs_ref[0, rows, :]
                g = _dot(x, wg)
                u = _dot(x, wu)
                hid_ref[s, rows, :] = (g * jax.nn.sigmoid(g) * u).astype(BF16)

    @pl.when(s >= nf)
    def _():
        d = s - nf
        wd = wd_ref[0].astype(BF16)
        for gs_ref, y_ref, hid_ref, rt in zip(gs_refs, y_refs, hid_refs, row_tiles):
            for r0 in range(0, hid_ref.shape[1], rt):
                rows = pl.ds(r0, rt)
                acc = _dot(hid_ref[0, rows, :], wd[0:FF_CHUNK, :])
                for f in range(1, nf):
                    acc = acc + _dot(hid_ref[f, rows, :], wd[f * FF_CHUNK:(f + 1) * FF_CHUNK, :])
                gs = gs_ref[0, rows, :]
                y_ref[0, d, rows, :] = jnp.concatenate(
                    [acc[:, k * 128:(k + 1) * 128] * gs for k in range(DOWN_CHUNK // 128)], axis=1).astype(y_ref.dtype)


def _ffn(xs_list, gs_list, w_gate, w_up, w_down, layer):
    E = w_gate.shape[1]
    row_tiles = tuple(min(xs.shape[1], 512) for xs in xs_list)
    nf = EXPERT_FF // FF_CHUNK
    nd = D // DOWN_CHUNK
    in_specs = [pl.BlockSpec((1, xs.shape[1], D), lambda e, s: (e, 0, 0)) for xs in xs_list]
    in_specs += [pl.BlockSpec((1, gs.shape[1], 128), lambda e, s: (e, 0, 0)) for gs in gs_list]
    up_map = lambda e, s: (layer, e, 0, jnp.minimum(s, nf - 1))
    in_specs += [pl.BlockSpec((None, 1, D, FF_CHUNK), up_map),
                 pl.BlockSpec((None, 1, D, FF_CHUNK), up_map),
                 pl.BlockSpec((None, 1, EXPERT_FF, DOWN_CHUNK), lambda e, s: (layer, e, 0, jnp.maximum(s - nf, 0)))]
    return pl.pallas_call(
        functools.partial(_ffn_kernel, row_tiles=row_tiles),
        grid=(E, nf + nd),
        in_specs=in_specs,
        out_specs=[pl.BlockSpec((1, nd, xs.shape[1], DOWN_CHUNK), lambda e, s: (e, 0, 0, 0)) for xs in xs_list],
        out_shape=[jax.ShapeDtypeStruct((E, nd, xs.shape[1], DOWN_CHUNK), BF16) for xs in xs_list],
        scratch_shapes=[pltpu.VMEM((nf, xs.shape[1], FF_CHUNK), BF16) for xs in xs_list],
        compiler_params=_cp("parallel", "arbitrary"),
        name="expert_ffn",
    )(*xs_list, *gs_list, w_gate, w_up, w_down)


def _combine_kernel(x_ref, sid_ref, y_ref, mod_ref, o_ref, *, cap):
    tbs = x_ref.shape[1]
    E = sid_ref.shape[1]
    slot = lax.broadcasted_iota(I32, (cap, tbs), 0)
    hit = jnp.concatenate([(slot == sid_ref[0, e]).astype(BF16) for e in range(E)], axis=0)
    gate = mod_ref[0][5:6]
    for d in range(D // DOWN_CHUNK):
        cs = slice(d * DOWN_CHUNK, (d + 1) * DOWN_CHUNK)
        z = _dot_tn(hit, y_ref[:, d].reshape(E * cap, DOWN_CHUNK))
        o_ref[0, :, cs] = x_ref[0, :, cs] + gate[:, cs] * z


def _combine(x, sid, y, mod, cap, tbs, row_block0, segment):
    B, E, T = sid.shape
    ntb = T // tbs
    xmap = lambda b, tb: (b, row_block0 + tb, 0)
    return pl.pallas_call(
        functools.partial(_combine_kernel, cap=cap),
        grid=(B, ntb),
        in_specs=[pl.BlockSpec((1, tbs, D), xmap),
                  pl.BlockSpec((1, E, 1, tbs), lambda b, tb: (b, 0, 0, tb)),
                  pl.BlockSpec((E, D // DOWN_CHUNK, cap, DOWN_CHUNK), lambda b, tb: (0, 0, b, 0)),
                  pl.BlockSpec((1, 8, D), lambda b, tb: (b * 2 + segment, 0, 0))],
        out_specs=pl.BlockSpec((1, tbs, D), xmap),
        out_shape=jax.ShapeDtypeStruct(x.shape, F32),
        input_output_aliases={0: 0},
        compiler_params=_cp("parallel", "parallel"),
        name=f"combine_{T}",
    )(x, sid.reshape(B, E, 1, T), y, mod)


def _moe(x, h, logits, mod, w_gate, w_up, w_down, layer, need_ctx):
    sid_l, gate_l = _topk(logits[:, :, :T_L], CAP_L)
    xs_l, gs_l = _gather(h, sid_l, gate_l, CAP_L, 0)
    if need_ctx:
        sid_c, gate_c = _topk(logits[:, :, T_L:], CAP_C)
        xs_c, gs_c = _gather(h, sid_c, gate_c, CAP_C, T_L // T_C)
        y_l, y_c = _ffn([xs_l, xs_c], [gs_l, gs_c], w_gate, w_up, w_down, layer)
    else:
        (y_l,) = _ffn([xs_l], [gs_l], w_gate, w_up, w_down, layer)
    x = _combine(x, sid_l, y_l, mod, CAP_L, 512, 0, 0)
    if need_ctx:
        x = _combine(x, sid_c, y_c, mod, CAP_C, T_C, T_L // T_C, 1)
    return x


OD_N = RWKV_IN + MLA_U


def _odd_weights(w_in, partner):
    rw = w_in[:, :RWKV_IN]
    qd = w_in[:, RWKV_IN:RWKV_IN + MLA_Q_RANK]
    kvd = w_in[:, RWKV_IN + MLA_Q_RANK:RWKV_IN + MLA_Q_RANK + MLA_KV_RANK]
    kr = w_in[:, RWKV_IN + MLA_Q_RANK + MLA_KV_RANK:]
    pad = jnp.zeros((D, 128 - MLA_ROPE), w_in.dtype)
    w = jnp.concatenate([rw, qd, kvd, kr, pad, kr[:, partner], pad], axis=1)
    return w.astype(BF16)


def _inproj_odd_kernel(x_ref, mod_ref, g_ref, w_ref, ur_ref, um_ref):
    mod = mod_ref[0]
    h = _modulate(x_ref[0], g_ref[...], mod[0:1], mod[1:2])
    y = _dot(h.astype(BF16), w_ref[...])
    ur_ref[0] = y[:, :RWKV_IN]
    um_ref[0] = y[:, RWKV_IN:]


def _inproj_odd(x, mod, g, w):
    B = x.shape[0]
    tok = lambda n: pl.BlockSpec((1, TM, n), lambda b, t: (b, t, 0))
    return pl.pallas_call(
        _inproj_odd_kernel,
        grid=(B, N_TILES),
        in_specs=[tok(D), _mod_spec(), pl.BlockSpec((1, D), lambda b, t: (0, 0)),
                  pl.BlockSpec((D, OD_N), lambda b, t: (0, 0))],
        out_specs=[tok(RWKV_IN), tok(MLA_U)],
        out_shape=[jax.ShapeDtypeStruct((B, NT, RWKV_IN), F32),
                   jax.ShapeDtypeStruct((B, NT, MLA_U), F32)],
        compiler_params=_cp("parallel", "parallel"),
        name="inproj_odd",
    )(x, mod, g.reshape(1, D), w)


def _head_ones():
    h = np.arange(RWKV_WIDTH) // RWKV_HEAD
    return jnp.asarray((h[:, None] == h[None, :]).astype(np.float32)).astype(BF16)


def _headsum(x, bd):
    return _dot2(x, bd)


def _rwkv_prep_kernel(u_ref, up_ref, un_ref, mup_ref, mun_ref, w0_ref, w2_ref, a0_ref, a2_ref, g2_ref,
                      kk_ref, ka_ref, bd_ref,
                      r_ref, v_ref, kkn_ref, g_ref, lw_ref, bq_ref, kd_ref):
    t = pl.program_id(1)
    u = u_ref[0]
    rows = lax.broadcasted_iota(I32, (TM, 1), 0)
    is_ctx = t == N_LAT_TILES
    seg_start = (t == 0) | is_ctx
    seg_end = (t == N_LAT_TILES - 1) | is_ctx
    last_row = jnp.where(is_ctx, T_C - 1, TM - 1)
    prev_row = jnp.where(seg_start, 0.0, up_ref[0, 7:8, :])
    next_row = jnp.where(seg_end, 0.0, un_ref[0, 0:1, :])
    prev = jnp.where(rows == 0, prev_row, pltpu.roll(u, 1, 0))
    nxt = jnp.where(rows == last_row, next_row, pltpu.roll(u, TM - 1, 0))
    us = u + mup_ref[...] * (prev - u) + mun_ref[...] * (nxt - u)
    W = RWKV_WIDTH
    r = us[:, :W]
    k = us[:, W:2 * W]
    v = us[:, 2 * W:3 * W]
    wl = jnp.tanh(us[:, 3 * W:3 * W + 128]).astype(BF16)
    al = us[:, 3 * W + 128:3 * W + 256].astype(BF16)
    gl = jax.nn.sigmoid(us[:, 3 * W + 256:3 * W + 384]).astype(BF16)
    bd = bd_ref[...]
    kk = k * kk_ref[...]
    kk = kk * lax.rsqrt(jnp.maximum(_headsum(kk * kk, bd), 1e-24))
    r_ref[0] = r
    v_ref[0] = v
    kkn_ref[0] = kk
    g_ref[0] = _dot(gl, g2_ref[...])
    for d in range(2):
        z = w0_ref[d:d + 1, :] + _dot(wl, w2_ref[d])
        a = jax.nn.sigmoid(a0_ref[d:d + 1, :] + _dot(al, a2_ref[d]))
        lw_ref[d, 0] = jax.nn.sigmoid(z) * (-float(np.exp(-0.5)))
        bq_ref[d, 0] = kk * a
        kd_ref[d, 0] = k * (1.0 + (a - 1.0) * ka_ref[...])


def _rwkv_prep(ur, mu_prev, mu_next, w0, w2, a0, a2, g2, k_k, k_a, bd):
    B = ur.shape[0]
    W = RWKV_WIDTH
    z = jnp.zeros((64, W), F32)
    w2p = jnp.stack([jnp.concatenate([w2[0], z]), jnp.concatenate([z, w2[1]])]).astype(BF16)
    a2p = jnp.stack([jnp.concatenate([a2[0], z]), jnp.concatenate([z, a2[1]])]).astype(BF16)
    tok = lambda n: pl.BlockSpec((1, TM, n), lambda b, t: (b, t, 0))
    tok2 = pl.BlockSpec((2, 1, TM, W), lambda b, t: (0, b, t, 0))
    row = lambda n: pl.BlockSpec((1, n), lambda b, t: (0, 0))
    full = lambda *s: pl.BlockSpec(s, lambda b, t: (0,) * len(s))
    slab = TM // 8
    f32 = lambda: jax.ShapeDtypeStruct((B, NT, W), F32)
    f32d = lambda: jax.ShapeDtypeStruct((2, B, NT, W), F32)
    return pl.pallas_call(
        _rwkv_prep_kernel,
        grid=(B, N_TILES),
        in_specs=[tok(RWKV_IN),
                  pl.BlockSpec((1, 8, RWKV_IN), lambda b, t: (b, jnp.maximum(t * slab - 1, 0), 0)),
                  pl.BlockSpec((1, 8, RWKV_IN), lambda b, t: (b, jnp.minimum((t + 1) * slab, NT // 8 - 1), 0)),
                  row(RWKV_IN), row(RWKV_IN), full(2, W), full(2, 128, W), full(2, W), full(2, 128, W),
                  full(128, W), row(W), row(W), full(W, W)],
        out_specs=[tok(W), tok(W), tok(W), tok(W), tok2, tok2, tok2],
        out_shape=[f32(), f32(), f32(), f32(), f32d(), f32d(), f32d()],
        compiler_params=_cp("parallel", "parallel"),
        name="rwkv_prep",
    )(ur, ur, ur, mu_prev.reshape(1, RWKV_IN), mu_next.reshape(1, RWKV_IN), w0, w2p, a0, a2p,
      g2.astype(BF16), k_k.reshape(1, W), k_a.reshape(1, W), bd)


def _wkv_kernel(r_ref, v_ref, kk_ref, lw_ref, bq_ref, kd_ref, o_ref, s_ref, *, nb):
    d = pl.program_id(0)
    i = pl.program_id(2)
    C = WKV_CHUNK
    P = RWKV_HEADS // 2
    G = nb * P
    HD = RWKV_HEAD

    @pl.when(i == 0)
    def _():
        s_ref[...] = jnp.zeros_like(s_ref)

    sgn = 1 - 2 * d
    rel = (lax.broadcasted_iota(I32, (C, C), 0) - lax.broadcasted_iota(I32, (C, C), 1)) * sgn
    incl_b = (rel >= 0).astype(BF16)
    row_w = lax.broadcasted_iota(I32, (C, 2 * HD), 0)
    col_w = lax.broadcasted_iota(I32, (C, 2 * HD), 1) & (HD - 1)
    rel_w = (row_w - col_w) * sgn
    incl = rel_w >= 0
    strict = rel_w > 0
    eye = (rel_w == 0).astype(F32)
    lo = lax.broadcasted_iota(I32, (C, 2 * HD), 1) < HD
    same_head = ((lax.broadcasted_iota(I32, (2 * HD, 2 * HD), 0) < HD)
                 == (lax.broadcasted_iota(I32, (2 * HD, 2 * HD), 1) < HD))

    def same_block(k):
        return (row_w >> k) == (col_w >> k)

    def pair(y):
        zero = jnp.zeros_like(y)
        return jnp.concatenate([jnp.where(lo, y, zero), jnp.where(lo, zero, y)], axis=0)

    at, rt, bt, kt, a0, r0, bh, kh, v, p_end = ([] for _ in range(10))
    for b in range(nb):
        lw = lw_ref[0, b]
        lw_hi, lw_lo = _split_bf16(lw)
        ci = _dot(incl_b, lw_hi) + _dot(incl_b, lw_lo)
        ce = ci - lw
        tot = jnp.sum(lw, axis=0, keepdims=True)
        m = 0.5 * tot
        r = r_ref[b]
        kk = kk_ref[b]
        bq = bq_ref[0, b]
        kd = kd_ref[0, b]
        e_ci = jnp.exp(ci - m)
        e_mci = jnp.exp(m - ci)
        e_end = jnp.exp(tot - ci)
        full = ((at, (-kk * jnp.exp(ce - m)).astype(BF16)), (rt, (r * e_ci).astype(BF16)),
                (bt, (bq * e_mci).astype(BF16)), (kt, (kd * e_mci).astype(BF16)),
                (a0, (-kk * jnp.exp(ce)).astype(BF16)), (r0, (r * (e_ci * jnp.exp(m))).astype(BF16)),
                (bh, (bq * e_end).astype(BF16)), (kh, (kd * e_end).astype(BF16)),
                (v, v_ref[b].astype(BF16)), (p_end, jnp.exp(tot)))
        for dst, x in full:
            dst.extend(x[:, p * 2 * HD:(p + 1) * 2 * HD] for p in range(P))

    S = [s_ref[g] for g in range(G)]
    s_b = [s.astype(BF16) for s in S]
    left = [jnp.concatenate([at[g], rt[g]], axis=0) for g in range(G)]
    nb_ = [_dot_nt(left[g], pair(bt[g])) for g in range(G)]
    nk_ = [_dot_nt(left[g], pair(kt[g])) for g in range(G)]
    n_ab = [jnp.where(strict, x[:C], 0.0) for x in nb_]
    a_rb = [jnp.where(incl, x[C:], 0.0).astype(BF16) for x in nb_]
    mask_k = jnp.concatenate([strict, incl], axis=0)
    a_k = [jnp.where(mask_k, x, 0.0).astype(BF16) for x in nk_]
    from_state = [_dot_nt(jnp.concatenate([a0[g], r0[g]], axis=0), s_b[g]) for g in range(G)]
    from_v = [_dot(a_k[g], pair(v[g])) for g in range(G)]
    w_rhs = [from_state[g][:C] + from_v[g][:C] for g in range(G)]
    inv = [eye + jnp.where(same_block(1), n, 0.0) for n in n_ab]
    for k in range(1, 6):
        off_mask = same_block(k + 1) & jnp.logical_not(same_block(k))
        inv_b = [x.astype(BF16) for x in inv]
        e = [_dot(jnp.where(off_mask, n, 0.0).astype(BF16), pair(x)) for n, x in zip(n_ab, inv_b)]
        inv = [x + _dot(xb, pair(y.astype(BF16))) for x, xb, y in zip(inv, inv_b, e)]
    U = [_dot(inv[g].astype(BF16), pair(w_rhs[g].astype(BF16))).astype(BF16) for g in range(G)]
    for g in range(G):
        o = from_state[g][C:] + _dot(a_rb[g], pair(U[g])) + from_v[g][C:]
        o_ref[0, g // P, :, (g % P) * 2 * HD:(g % P + 1) * 2 * HD] = o
    for g in range(G):
        upd = _dot_tn(jnp.concatenate([U[g], v[g]], axis=0), jnp.concatenate([bh[g], kh[g]], axis=0))
        s_ref[g] = S[g] * p_end[g] + jnp.where(same_head, upd, 0.0)


WKV_BATCH = 4


def _wkv(r, v, kk, lw, bq, kd, nb=WKV_BATCH):
    B = r.shape[0]
    C = WKV_CHUNK
    n_lat = T_L // C

    def chunk(d, i):
        fwd = jnp.where(i < N_CTX_CHUNKS, n_lat + i, i - N_CTX_CHUNKS)
        return jnp.where(d == 0, fwd, N_CHUNKS - 1 - i)

    shared = pl.BlockSpec((nb, C, RWKV_WIDTH), lambda d, b, i: (b, chunk(d, i), 0))
    perdir = pl.BlockSpec((1, nb, C, RWKV_WIDTH), lambda d, b, i: (d, b, chunk(d, i), 0))
    return pl.pallas_call(
        functools.partial(_wkv_kernel, nb=nb),
        grid=(2, B // nb, N_CHUNKS),
        in_specs=[shared, shared, shared, perdir, perdir, perdir],
        out_specs=perdir,
        out_shape=jax.ShapeDtypeStruct((2, B, NT, RWKV_WIDTH), F32),
        scratch_shapes=[pltpu.VMEM((nb * RWKV_HEADS // 2, 2 * RWKV_HEAD, 2 * RWKV_HEAD), F32)],
        compiler_params=_cp("parallel", "parallel", "arbitrary"),
        name="wkv7",
    )(r, v, kk, lw, bq, kd)


def _rwkv_finish_kernel(o_ref, r_ref, v_ref, g_ref, kd_ref, rk_ref, lg_ref, lb_ref, bd_ref, y_ref):
    bd = bd_ref[...]
    o = o_ref[0, 0] + o_ref[1, 0]
    mu = _headsum(o, bd) * (1.0 / RWKV_HEAD)
    xc = o - mu
    var = _headsum(xc * xc, bd) * (1.0 / RWKV_HEAD)
    on = xc * lax.rsqrt(var + LNX_EPS) * lg_ref[...] + lb_ref[...]
    bonus = _headsum(r_ref[0] * (kd_ref[0, 0] + kd_ref[1, 0]) * rk_ref[...], bd) * v_ref[0]
    y_ref[0] = ((on + bonus) * g_ref[0]).astype(y_ref.dtype)


def _rwkv_finish(o, r, v, g, kd, r_k, lnx_g, lnx_b, bd):
    B = r.shape[0]
    W = RWKV_WIDTH
    tok = pl.BlockSpec((1, TM, W), lambda b, t: (b, t, 0))
    tok2 = pl.BlockSpec((2, 1, TM, W), lambda b, t: (0, b, t, 0))
    row = pl.BlockSpec((1, W), lambda b, t: (0, 0))
    return pl.pallas_call(
        _rwkv_finish_kernel,
        grid=(B, N_TILES),
        in_specs=[tok2, tok, tok, tok, tok2, row, row, row, pl.BlockSpec((W, W), lambda b, t: (0, 0))],
        out_specs=tok,
        out_shape=jax.ShapeDtypeStruct((B, NT, W), BF16),
        compiler_params=_cp("parallel", "parallel"),
        name="rwkv_finish",
    )(o, r, v, g, kd, r_k.reshape(1, W), lnx_g.reshape(1, W), lnx_b.reshape(1, W), bd)


MLA_QW = MLA_HEADS * MLA_SLOT


def _mla_weights(w_uq, w_ukv, partner):
    dq = MLA_NOPE + MLA_ROPE
    wq = jnp.zeros((MLA_Q_RANK, 2 * MLA_QW), F32)
    wk = jnp.zeros((MLA_KV_RANK, MLA_QW), F32)
    wv = jnp.zeros((MLA_KV_RANK, MLA_HEADS * MLA_V), F32)
    place = np.zeros((128, MLA_QW), np.float32)
    for h in range(MLA_HEADS):
        qh = w_uq[:, h * dq:(h + 1) * dq]
        wq = wq.at[:, h * MLA_SLOT:h * MLA_SLOT + dq].set(qh)
        wq = wq.at[:, MLA_QW + h * MLA_SLOT + MLA_NOPE:MLA_QW + h * MLA_SLOT + dq].set(qh[:, MLA_NOPE + partner])
        kvh = w_ukv[:, h * (MLA_NOPE + MLA_V):(h + 1) * (MLA_NOPE + MLA_V)]
        wk = wk.at[:, h * MLA_SLOT:h * MLA_SLOT + MLA_NOPE].set(kvh[:, :MLA_NOPE])
        wv = wv.at[:, h * MLA_V:(h + 1) * MLA_V].set(kvh[:, MLA_NOPE:])
        place[np.arange(MLA_ROPE), h * MLA_SLOT + MLA_NOPE + np.arange(MLA_ROPE)] = 1.0
    return wq.astype(BF16), wk.astype(BF16), wv.astype(BF16), jnp.asarray(place).astype(BF16)


def _mla_proj_kernel(u_ref, qg_ref, kg_ref, wq_ref, wk_ref, wv_ref, pl_ref, cq_ref, sq_ref, ck_ref, sk_ref,
                     q_ref, k_ref, v_ref):
    u = u_ref[0]
    qn = _rmsnorm(u[:, :MLA_Q_RANK], qg_ref[...]).astype(BF16)
    kvn = _rmsnorm(u[:, MLA_Q_RANK:MLA_Q_RANK + MLA_KV_RANK], kg_ref[...]).astype(BF16)
    o = MLA_Q_RANK + MLA_KV_RANK
    kr = u[:, o:o + 128] * ck_ref[...] + u[:, o + 128:o + 256] * sk_ref[...]
    qq = _dot(qn, wq_ref[...])
    q_ref[0] = (qq[:, :MLA_QW] * cq_ref[...] + qq[:, MLA_QW:] * sq_ref[...]).astype(BF16)
    k_ref[0] = (_dot(kvn, wk_ref[...]) + _dot2(kr, pl_ref[...])).astype(BF16)
    v_ref[0] = _dot(kvn, wv_ref[...]).astype(BF16)


def _mla_proj(um, q_norm_g, kv_norm_g, weights, tables):
    B = um.shape[0]
    wq, wk, wv, place = weights
    cq, sq, ck, sk = tables
    tok = lambda n: pl.BlockSpec((1, TM, n), lambda b, t: (b, t, 0))
    tab = lambda n: pl.BlockSpec((TM, n), lambda b, t: (t, 0))
    full = lambda a: pl.BlockSpec(a.shape, lambda b, t: (0, 0))
    return pl.pallas_call(
        _mla_proj_kernel,
        grid=(B, N_TILES),
        in_specs=[tok(MLA_U), pl.BlockSpec((1, MLA_Q_RANK), lambda b, t: (0, 0)),
                  pl.BlockSpec((1, MLA_KV_RANK), lambda b, t: (0, 0)),
                  full(wq), full(wk), full(wv), full(place),
                  tab(MLA_QW), tab(MLA_QW), tab(MLA_SLOT), tab(MLA_SLOT)],
        out_specs=[tok(MLA_QW), tok(MLA_QW), tok(MLA_HEADS * MLA_V)],
        out_shape=[jax.ShapeDtypeStruct((B, NT, MLA_QW), BF16),
                   jax.ShapeDtypeStruct((B, NT, MLA_QW), BF16),
                   jax.ShapeDtypeStruct((B, NT, MLA_HEADS * MLA_V), BF16)],
        compiler_params=_cp("parallel", "parallel"),
        name="mla_proj",
    )(um, q_norm_g.reshape(1, -1), kv_norm_g.reshape(1, -1), wq, wk, wv, place, cq, sq, ck, sk)


MLA_TQ = 512


def _mla_attn_kernel(q_ref, k_ref, v_ref, o_ref):
    t = pl.program_id(2)
    is_ctx = t == T_L // MLA_TQ
    col = lax.broadcasted_iota(I32, (MLA_TQ, NT), 1)
    valid = jnp.logical_not(is_ctx) | (col >= T_L)
    lo_lanes = lax.broadcasted_iota(I32, (MLA_TQ, 128), 1) < MLA_V
    lo_rows = lax.broadcasted_iota(I32, (NT, 128), 1) < MLA_V
    vall = v_ref[0]
    one = jnp.ones((NT, 128), vall.dtype)
    vh = [jnp.where(lo_rows, vall, one), jnp.where(lo_rows, one, vall)]
    s = [_dot_nt(q_ref[0, :, hh * MLA_SLOT:(hh + 1) * MLA_SLOT], k_ref[0, :, hh * MLA_SLOT:(hh + 1) * MLA_SLOT])
         for hh in range(2)]
    e = []
    for hh in range(2):
        sm = jnp.where(valid, s[hh], -jnp.inf)
        e.append(jnp.exp(sm - jnp.max(sm, axis=-1, keepdims=True)).astype(BF16))
    r = [_dot(e[hh], vh[hh]) for hh in range(2)]
    num = jnp.where(lo_lanes, r[0], r[1])
    den = pltpu.roll(jnp.where(lo_lanes, r[1], r[0]), MLA_V, 1)
    o_ref[0] = (num * (1.0 / den)).astype(o_ref.dtype)


def _mla_attn(q, k, v):
    B = q.shape[0]
    return pl.pallas_call(
        _mla_attn_kernel,
        grid=(B, MLA_HEADS // 2, pl.cdiv(NT, MLA_TQ)),
        in_specs=[pl.BlockSpec((1, MLA_TQ, 2 * MLA_SLOT), lambda b, p, t: (b, t, p)),
                  pl.BlockSpec((1, NT, 2 * MLA_SLOT), lambda b, p, t: (b, 0, p)),
                  pl.BlockSpec((1, NT, 2 * MLA_V), lambda b, p, t: (b, 0, p))],
        out_specs=pl.BlockSpec((1, MLA_TQ, 2 * MLA_V), lambda b, p, t: (b, t, p)),
        out_shape=jax.ShapeDtypeStruct((B, NT, MLA_HEADS * MLA_V), BF16),
        compiler_params=_cp("parallel", "parallel", "parallel"),
        name="mla_attn",
    )(q, k, v)


def _final_norm_kernel(x_ref, g_ref, o_ref):
    o_ref[0] = _rmsnorm(x_ref[0], g_ref[...])


def _final_norm(x, g):
    B = x.shape[0]
    tok = pl.BlockSpec((1, TM, D), lambda b, t: (b, t, 0))
    return pl.pallas_call(
        _final_norm_kernel,
        grid=(B, N_LAT_TILES),
        in_specs=[tok, pl.BlockSpec((1, D), lambda b, t: (0, 0))],
        out_specs=tok,
        out_shape=jax.ShapeDtypeStruct((B, T_L, D), F32),
        compiler_params=_cp("parallel", "parallel"),
        name="final_norm",
    )(x, g.reshape(1, D))


def kernel(x, c, ctx, c_ctx, ada_w, ada_b, norm_mix_g, norm_ffn_g, router_w, exp_w_gate, exp_w_up, exp_w_down, ev_w_in, ev_w_out, ev_sink, ev_pool_w, ev_pool_scale, od_w_in, od_w_out, od_mu_prev, od_mu_next, od_w0, od_w2, od_a0, od_a2, od_g2, od_k_k, od_k_a, od_r_k, od_lnx_g, od_lnx_b, od_q_norm_g, od_w_uq, od_kv_norm_g, od_w_ukv, final_norm_g):
    depth = ada_w.shape[0]
    assert x.shape[1:] == (T_L, D) and ctx.shape[1:] == (T_C, D)
    mods = _modvec(c, c_ctx, ada_w, ada_b)
    ev_tables, ev_partner = _even_tables()
    mla_tables, mla_partner = _mla_tables()
    bd = _head_ones()
    xs = jnp.concatenate([x, ctx], axis=1)
    for l in range(depth):
        need_ctx = l < depth - 1
        i = l // 2
        mod = mods[l]
        if l % 2 == 0:
            q, kd, vd, pu = _inproj_even(xs, mod, norm_mix_g[l], _even_weights(ev_w_in[i], ev_partner), ev_tables)
            mix_a = _attn_even(q, kd, vd, ev_sink[i])
            mix_b = _pool(pu, ev_pool_w[i], ev_pool_scale[i])
            w_out = ev_w_out[i]
        else:
            ur, um = _inproj_odd(xs, mod, norm_mix_g[l], _odd_weights(od_w_in[i], mla_partner))
            r, v, kk, g, lw, bq, kdir = _rwkv_prep(ur, od_mu_prev[i], od_mu_next[i], od_w0[i], od_w2[i],
                                                   od_a0[i], od_a2[i], od_g2[i], od_k_k[i], od_k_a[i], bd)
            o = _wkv(r, v, kk, lw, bq, kdir)
            mix_a = _rwkv_finish(o, r, v, g, kdir, od_r_k[i], od_lnx_g[i], od_lnx_b[i], bd)
            mq, mk, mv = _mla_proj(um, od_q_norm_g[i], od_kv_norm_g[i],
                                   _mla_weights(od_w_uq[i], od_w_ukv[i], mla_partner), mla_tables)
            mix_b = _mla_attn(mq, mk, mv)
            w_out = od_w_out[i]
        xs, h, logits = _outproj(mix_a, mix_b, w_out, xs, mod, norm_ffn_g[l], router_w[l])
        xs = _moe(xs, h, logits, mod, exp_w_gate, exp_w_up, exp_w_down, l, need_ctx)
    return _final_norm(xs, final_norm_g)
```

```python
import functools

import numpy as np
import jax
import jax.numpy as jnp
from jax import lax
from jax.experimental import pallas as pl
from jax.experimental.pallas import tpu as pltpu

F32 = jnp.float32
BF16 = jnp.bfloat16
I32 = jnp.int32
HIGHEST = lax.Precision.HIGHEST

D = 1024
T_L = 2048
T_C = 256
NT = T_L + T_C
GRID_W = 64
EPS = 1e-6
ROPE_BASE = 10000.0

TM = 512
N_LAT_TILES = T_L // TM
N_TILES = N_LAT_TILES + 1
assert T_C <= TM and T_L % TM == 0

A_HEADS = 8
A_KV_HEADS = 2
HEAD_DIM = 64
A_WINDOW = 128
A_BLOCK = 128
N_ABLK = NT // A_BLOCK
N_LAT_ABLK = T_L // A_BLOCK

POOL_WINDOWS = (2, 4, 8, 16)
POOL_GROUP = 128

RWKV_HEADS = 8
RWKV_HEAD = 64
RWKV_WIDTH = 512
RWKV_IN = 1920
LNX_EPS = 64e-5
WKV_CHUNK = 64
N_CHUNKS = NT // WKV_CHUNK
N_CTX_CHUNKS = T_C // WKV_CHUNK

MLA_HEADS = 8
MLA_Q_RANK = 256
MLA_KV_RANK = 128
MLA_NOPE = 64
MLA_ROPE = 32
MLA_V = 64
MLA_SLOT = 128
MLA_U = 640

N_EXPERTS = 16
EXPERT_FF = 2048
CAP_L = 2 * T_L // N_EXPERTS
CAP_C = 2 * T_C // N_EXPERTS
FF_CHUNK = 512

VMEM_LIMIT_BYTES = 56 * 1024 * 1024


def _cp(*sem):
    return pltpu.CompilerParams(dimension_semantics=sem, vmem_limit_bytes=VMEM_LIMIT_BYTES)


def _split_bf16(x):
    hi = x.astype(BF16)
    lo = (x - hi.astype(F32)).astype(BF16)
    return hi, lo


def _dot(a, b):
    return jnp.dot(a, b, preferred_element_type=F32)


def _dot_nt(a, b):
    return lax.dot_general(a, b, (((1,), (1,)), ((), ())), preferred_element_type=F32)


def _dot_tn(a, b):
    return lax.dot_general(a, b, (((0,), (0,)), ((), ())), preferred_element_type=F32)


def _dot2(a_f32, b_bf16):
    hi, lo = _split_bf16(a_f32)
    return _dot(hi, b_bf16) + _dot(lo, b_bf16)


def _modulate(x, g, shift, scale):
    ms = jnp.mean(x * x, axis=-1, keepdims=True)
    return x * lax.rsqrt(ms + EPS) * g * (1.0 + scale) + shift


def _rmsnorm(x, g):
    ms = jnp.mean(x * x, axis=-1, keepdims=True)
    return x * lax.rsqrt(ms + EPS) * g


def _modvec_kernel(a_ref, w_ref, b_ref, o_ref):
    a = a_ref[...]
    a = a * jax.nn.sigmoid(a)
    o_ref[0] = jnp.dot(a, w_ref[0], precision=HIGHEST, preferred_element_type=F32) + b_ref[0]


def _modvec(c, c_ctx, ada_w, ada_b):
    L = ada_w.shape[0]
    B = c.shape[0]
    assert B < 16
    rows = jnp.zeros((16, D), F32).at[:B].set(c).at[B].set(c_ctx)
    tn = 512
    out = pl.pallas_call(
        _modvec_kernel,
        grid=(L, 6 * D // tn),
        in_specs=[pl.BlockSpec((16, D), lambda l, n: (0, 0)),
                  pl.BlockSpec((1, D, tn), lambda l, n: (l, 0, n)),
                  pl.BlockSpec((1, 1, tn), lambda l, n: (l, 0, n))],
        out_specs=pl.BlockSpec((1, 16, tn), lambda l, n: (l, 0, n)),
        out_shape=jax.ShapeDtypeStruct((L, 16, 6 * D), F32),
        compiler_params=_cp("parallel", "parallel"),
        name="modvec",
    )(rows, ada_w, ada_b.reshape(L, 1, 6 * D))
    m = out.reshape(L, 16, 6, D)
    ml = m[:, :B]
    mc = jnp.broadcast_to(m[:, B:B + 1], (L, B, 6, D))
    mod = jnp.stack([ml, mc], axis=2)
    mod = jnp.pad(mod, ((0, 0), (0, 0), (0, 0), (0, 2), (0, 0)))
    return mod.reshape(L, B * 2, 8, D)


def _mod_spec():
    return pl.BlockSpec((1, 8, D), lambda b, t: (b * 2 + t // N_LAT_TILES, 0, 0))


def _rope_block(width, pos_row, pos_col):
    d = width // 2
    half = d // 2
    inv = ROPE_BASE ** (-np.arange(half, dtype=np.float32) / half)
    cos = np.zeros((pos_row.shape[0], width), np.float32)
    sin = np.zeros_like(cos)
    partner = np.zeros((width,), np.int64)
    for part, pos in ((0, pos_row), (1, pos_col)):
        ang = pos.astype(np.float32)[:, None] * inv[None, :]
        c, s = np.cos(ang), np.sin(ang)
        o = part * d
        cos[:, o:o + half] = c
        cos[:, o + half:o + d] = c
        sin[:, o:o + half] = -s
        sin[:, o + half:o + d] = s
        partner[o:o + half] = np.arange(o + half, o + d)
        partner[o + half:o + d] = np.arange(o, o + half)
    return cos, sin, partner


def _latent_positions():
    t = np.arange(T_L)
    return t // GRID_W, t % GRID_W


def _even_tables():
    cos, sin, partner = _rope_block(HEAD_DIM, *_latent_positions())
    scale = HEAD_DIM ** -0.5
    ones = np.ones((T_C, HEAD_DIM), np.float32)
    zeros = np.zeros((T_C, HEAD_DIM), np.float32)
    cos = np.concatenate([cos, ones], axis=0)
    sin = np.concatenate([sin, zeros], axis=0)
    cq = np.tile(cos, (1, A_HEADS)) * scale
    sq = np.tile(sin, (1, A_HEADS)) * scale
    ck = np.tile(cos, (1, 2 * A_KV_HEADS))
    sk = np.tile(sin, (1, 2 * A_KV_HEADS))
    return (jnp.asarray(cq), jnp.asarray(sq), jnp.asarray(ck), jnp.asarray(sk)), partner


def _mla_tables():
    cos, sin, partner = _rope_block(MLA_ROPE, *_latent_positions())
    cos = np.concatenate([cos, np.ones((T_C, MLA_ROPE), np.float32)], axis=0)
    sin = np.concatenate([sin, np.zeros((T_C, MLA_ROPE), np.float32)], axis=0)
    scale = (MLA_NOPE + MLA_ROPE) ** -0.5
    cq1 = np.zeros((NT, MLA_SLOT), np.float32)
    sq1 = np.zeros((NT, MLA_SLOT), np.float32)
    cq1[:, :MLA_NOPE] = 1.0
    cq1[:, MLA_NOPE:MLA_NOPE + MLA_ROPE] = cos
    sq1[:, MLA_NOPE:MLA_NOPE + MLA_ROPE] = sin
    cq = np.tile(cq1, (1, MLA_HEADS)) * scale
    sq = np.tile(sq1, (1, MLA_HEADS)) * scale
    ck = np.zeros((NT, MLA_SLOT), np.float32)
    sk = np.zeros((NT, MLA_SLOT), np.float32)
    ck[:, :MLA_ROPE] = cos
    sk[:, :MLA_ROPE] = sin
    return (jnp.asarray(cq), jnp.asarray(sq), jnp.asarray(ck), jnp.asarray(sk)), partner


EV_Q, EV_K, EV_V, EV_P = 512, 256, 256, 512
EV_N = 2 * EV_Q + 2 * EV_K + EV_V + EV_P


def _even_weights(w_in, partner):
    q = w_in[:, :512]
    k = w_in[:, 512:640]
    v = w_in[:, 640:768]
    p = w_in[:, 768:1280]
    pq = np.concatenate([h * HEAD_DIM + partner for h in range(A_HEADS)])
    pk = np.concatenate([h * HEAD_DIM + partner for h in range(A_KV_HEADS)])
    dup = np.concatenate([np.arange(HEAD_DIM), np.arange(HEAD_DIM),
                          HEAD_DIM + np.arange(HEAD_DIM), HEAD_DIM + np.arange(HEAD_DIM)])
    w = jnp.concatenate([q, q[:, pq], k[:, dup], k[:, pk][:, dup], v[:, dup], p], axis=1)
    return w.astype(BF16)


def _inproj_even_kernel(x_ref, mod_ref, g_ref, w_ref, cq_ref, sq_ref, ck_ref, sk_ref,
                        q_ref, k_ref, v_ref, p_ref):
    mod = mod_ref[0]
    h = _modulate(x_ref[0], g_ref[...], mod[0:1], mod[1:2])
    y = _dot(h.astype(BF16), w_ref[...])
    o = 0
    q = y[:, o:o + EV_Q] * cq_ref[...] + y[:, o + EV_Q:o + 2 * EV_Q] * sq_ref[...]
    o += 2 * EV_Q
    k = y[:, o:o + EV_K] * ck_ref[...] + y[:, o + EV_K:o + 2 * EV_K] * sk_ref[...]
    o += 2 * EV_K
    q_ref[0] = q.astype(BF16)
    k_ref[0] = k.astype(BF16)
    v_ref[0] = y[:, o:o + EV_V].astype(BF16)
    p_ref[0] = y[:, o + EV_V:o + EV_V + EV_P]


def _inproj_even(x, mod, g, w, tables):
    B = x.shape[0]
    cq, sq, ck, sk = tables
    tok = lambda n: pl.BlockSpec((1, TM, n), lambda b, t: (b, t, 0))
    tab = lambda n: pl.BlockSpec((TM, n), lambda b, t: (t, 0))
    return pl.pallas_call(
        _inproj_even_kernel,
        grid=(B, N_TILES),
        in_specs=[tok(D), _mod_spec(), pl.BlockSpec((1, D), lambda b, t: (0, 0)),
                  pl.BlockSpec((D, EV_N), lambda b, t: (0, 0)),
                  tab(EV_Q), tab(EV_Q), tab(EV_K), tab(EV_K)],
        out_specs=[tok(EV_Q), tok(EV_K), tok(EV_V), tok(EV_P)],
        out_shape=[jax.ShapeDtypeStruct((B, NT, EV_Q), BF16),
                   jax.ShapeDtypeStruct((B, NT, EV_K), BF16),
                   jax.ShapeDtypeStruct((B, NT, EV_V), BF16),
                   jax.ShapeDtypeStruct((B, NT, EV_P), F32)],
        compiler_params=_cp("parallel", "parallel"),
        name="inproj_even",
    )(x, mod, g.reshape(1, D), w, cq, sq, ck, sk)


def _attn_even_kernel(sink_ref, q_ref, kp_ref, kc_ref, kn_ref, kx_ref,
                      vp_ref, vc_ref, vn_ref, vx_ref, o_ref):
    j = pl.program_id(1)
    is_lat = j < N_LAT_ABLK
    q = q_ref[0]
    qpos = j * A_BLOCK + lax.broadcasted_iota(I32, (A_BLOCK, 3 * A_BLOCK), 0)
    kpos = (j - 1) * A_BLOCK + lax.broadcasted_iota(I32, (A_BLOCK, 3 * A_BLOCK), 1)
    valid = (jnp.abs(kpos - qpos) <= A_WINDOW) & (kpos >= 0) & (kpos < T_L) & is_lat
    lo_lanes = lax.broadcasted_iota(I32, (A_BLOCK, 128), 1) < HEAD_DIM
    kall, vall = [], []
    for kv in range(A_KV_HEADS):
        ks = slice(kv * 128, (kv + 1) * 128)
        kall.append(jnp.concatenate([kx_ref[0, :, ks], kp_ref[0, :, ks], kc_ref[0, :, ks], kn_ref[0, :, ks]], axis=0))
        vall.append(jnp.concatenate([vx_ref[0, :, ks], vp_ref[0, :, ks], vc_ref[0, :, ks], vn_ref[0, :, ks]], axis=0))
    heads = range(A_HEADS)
    group = A_HEADS // A_KV_HEADS
    qm = [jnp.where(lo_lanes if h % 2 == 0 else jnp.logical_not(lo_lanes), q[:, (h // 2) * 128:(h // 2 + 1) * 128],
                    jnp.zeros((A_BLOCK, 128), q.dtype)) for h in heads]
    s = [_dot_nt(qm[h], kall[h // group]) for h in heads]
    e, inv_den = [], []
    for h in heads:
        snk = sink_ref[h]
        s_ctx = s[h][:, :T_C]
        s_loc = jnp.where(valid, s[h][:, T_C:], -jnp.inf)
        m = jnp.maximum(jnp.max(s_ctx, axis=-1, keepdims=True), jnp.max(s_loc, axis=-1, keepdims=True))
        m = jnp.maximum(m, snk)
        e_ctx = jnp.exp(s_ctx - m)
        e_loc = jnp.exp(s_loc - m)
        den = (jnp.sum(e_ctx, axis=-1, keepdims=True) + jnp.sum(e_loc, axis=-1, keepdims=True)
               + jnp.exp(snk - m))
        e.append(jnp.concatenate([e_ctx, e_loc], axis=1).astype(BF16))
        inv_den.append(1.0 / den)
    o = [_dot(e[h], vall[h // group]) * inv_den[h] for h in heads]
    for p in range(A_HEADS // 2):
        o_ref[0, :, p * 128:(p + 1) * 128] = jnp.where(lo_lanes, o[2 * p], o[2 * p + 1]).astype(o_ref.dtype)


def _attn_even(q, kd, vd, sink):
    B = q.shape[0]
    last = N_LAT_ABLK - 1
    blk = lambda f: pl.BlockSpec((1, A_BLOCK, 256), f)
    prev = lambda b, j: (b, jnp.clip(j - 1, 0, last), 0)
    cur = lambda b, j: (b, j, 0)
    nxt = lambda b, j: (b, jnp.clip(j + 1, 0, last), 0)
    ctx = pl.BlockSpec((1, T_C, 256), lambda b, j: (b, T_L // T_C, 0))
    return pl.pallas_call(
        _attn_even_kernel,
        grid=(B, N_ABLK),
        in_specs=[pl.BlockSpec(memory_space=pltpu.SMEM),
                  pl.BlockSpec((1, A_BLOCK, EV_Q), cur),
                  blk(prev), blk(cur), blk(nxt), ctx,
                  blk(prev), blk(cur), blk(nxt), ctx],
        out_specs=pl.BlockSpec((1, A_BLOCK, 512), cur),
        out_shape=jax.ShapeDtypeStruct((B, NT, 512), BF16),
        compiler_params=_cp("parallel", "parallel"),
        name="attn_even",
    )(sink, q, kd, kd, kd, kd, vd, vd, vd, vd)


def _pool_kernel(up_ref, uc_ref, un_ref, w_ref, sc_ref, o_ref):
    j = pl.program_id(1)
    is_lat = j < N_LAT_ABLK
    seg_lo = jnp.where(is_lat, 0, T_L)
    seg_hi = jnp.where(is_lat, T_L, NT)
    u3 = jnp.concatenate([up_ref[0], uc_ref[0], un_ref[0]], axis=0)
    uc = uc_ref[0]
    r = j * A_BLOCK + lax.broadcasted_iota(I32, (A_BLOCK, 3 * A_BLOCK), 0)
    c = (j - 1) * A_BLOCK + lax.broadcasted_iota(I32, (A_BLOCK, 3 * A_BLOCK), 1)
    r1 = j * A_BLOCK + lax.broadcasted_iota(I32, (A_BLOCK, 1), 0)
    hi_u, lo_u = _split_bf16(u3)
    groups = range(len(POOL_WINDOWS))
    cols = [slice(gi * POOL_GROUP, (gi + 1) * POOL_GROUP) for gi in groups]
    bands, inv_cnt = [], []
    for w in POOL_WINDOWS:
        lo = jnp.maximum(r - w // 2, seg_lo)
        hi = jnp.minimum(r + w // 2, seg_hi)
        bands.append(((c >= lo) & (c < hi)).astype(BF16))
        cnt = (jnp.minimum(r1 + w // 2, seg_hi) - jnp.maximum(r1 - w // 2, seg_lo)).astype(F32)
        inv_cnt.append(1.0 / cnt)
    sums = [_dot(bands[gi], jnp.concatenate([hi_u[:, cols[gi]], lo_u[:, cols[gi]]], axis=1)) for gi in groups]
    dev = [((s[:, :POOL_GROUP] + s[:, POOL_GROUP:]) * inv_cnt[gi] - uc[:, cols[gi]]).astype(BF16)
           for gi, s in enumerate(sums)]
    outs = [_dot(dev[gi], w_ref[gi]) for gi in groups]
    o_ref[0] = (jnp.concatenate(outs, axis=1) * sc_ref[...]).astype(o_ref.dtype)


def _pool(u, pool_w, pool_scale):
    B = u.shape[0]
    blk = lambda f: pl.BlockSpec((1, A_BLOCK, 512), f)
    return pl.pallas_call(
        _pool_kernel,
        grid=(B, N_ABLK),
        in_specs=[blk(lambda b, j: (b, jnp.maximum(j - 1, 0), 0)),
                  blk(lambda b, j: (b, j, 0)),
                  blk(lambda b, j: (b, jnp.minimum(j + 1, N_ABLK - 1), 0)),
                  pl.BlockSpec((4, POOL_GROUP, POOL_GROUP), lambda b, j: (0, 0, 0)),
                  pl.BlockSpec((1, 512), lambda b, j: (0, 0))],
        out_specs=blk(lambda b, j: (b, j, 0)),
        out_shape=jax.ShapeDtypeStruct((B, NT, 512), BF16),
        compiler_params=_cp("parallel", "parallel"),
        name="pool",
    )(u, u, u, pool_w.astype(BF16), pool_scale.reshape(1, 512))


def _outproj_kernel(a_ref, p_ref, w_ref, x_ref, mod_ref, g_ref, wr_ref, xo_ref, h_ref, lg_ref):
    mod = mod_ref[0]
    y = _dot(a_ref[0], w_ref[0:512, :]) + _dot(p_ref[0], w_ref[512:1024, :])
    x = x_ref[0] + mod[2:3] * y
    xo_ref[0] = x
    h = _modulate(x, g_ref[...], mod[3:4], mod[4:5])
    w_hi, w_lo = _split_bf16(wr_ref[...])
    h_hi, h_lo = _split_bf16(h)
    h_ref[0] = h_hi
    lg_ref[0] = _dot_nt(w_hi, h_hi) + _dot_nt(w_lo, h_hi) + _dot_nt(w_hi, h_lo)


def _outproj(a, p, w_out, x, mod, g_ffn, w_router, need_ctx):
    B = x.shape[0]
    tok = lambda n: pl.BlockSpec((1, TM, n), lambda b, t: (b, t, 0))
    return pl.pallas_call(
        _outproj_kernel,
        grid=(B, N_TILES if need_ctx else N_LAT_TILES),
        in_specs=[tok(512), tok(512), pl.BlockSpec((D, D), lambda b, t: (0, 0)), tok(D), _mod_spec(),
                  pl.BlockSpec((1, D), lambda b, t: (0, 0)),
                  pl.BlockSpec((N_EXPERTS, D), lambda b, t: (0, 0))],
        out_specs=[tok(D), tok(D), pl.BlockSpec((1, N_EXPERTS, TM), lambda b, t: (b, 0, t))],
        out_shape=[jax.ShapeDtypeStruct((B, NT, D), F32),
                   jax.ShapeDtypeStruct((B, NT, D), BF16),
                   jax.ShapeDtypeStruct((B, N_EXPERTS, NT), F32)],
        compiler_params=_cp("parallel", "parallel"),
        name="outproj",
    )(a, p, w_out.astype(BF16), x, mod, g_ffn.reshape(1, D), w_router.T)


def _topk_kernel(lg_ref, sid_ref, gate_ref, *, cap):
    B, E, T = lg_ref.shape
    R = B * E
    lg = lg_ref[...]
    m = jnp.max(lg, axis=1, keepdims=True)
    e = jnp.exp(lg - m)
    aff = (e / jnp.sum(e, axis=1, keepdims=True)).reshape(R, T)
    bits = lax.bitcast_convert_type(aff, I32)

    def step(i, thr):
        cand = thr | jnp.left_shift(jnp.int32(1), 30 - i)
        cnt = jnp.sum((bits >= cand).astype(F32), axis=1, keepdims=True)
        return jnp.where(cnt >= cap, cand, thr)

    thr = lax.fori_loop(0, 31, step, jnp.zeros((R, 1), I32))
    gt = bits > thr
    eq = bits == thr
    need = cap - jnp.sum(gt.astype(F32), axis=1, keepdims=True)

    blk = 256
    tri = (lax.broadcasted_iota(I32, (blk, blk), 0) <= lax.broadcasted_iota(I32, (blk, blk), 1)).astype(BF16)

    def prefix(mask_f32):
        off = jnp.zeros((R, 1), F32)
        parts = []
        for k in range(T // blk):
            cb = _dot(mask_f32[:, k * blk:(k + 1) * blk].astype(BF16), tri) + off
            parts.append(cb)
            off = cb[:, blk - 1:blk]
        return parts

    eq_parts = prefix(eq.astype(F32))
    sel_parts = []
    for k in range(T // blk):
        sl = slice(k * blk, (k + 1) * blk)
        sel_parts.append(gt[:, sl] | (eq[:, sl] & (eq_parts[k] <= need)))
    sel = jnp.concatenate(sel_parts, axis=1)
    pos_parts = prefix(sel.astype(F32))
    for k in range(T // blk):
        sl = slice(k * blk, (k + 1) * blk)
        sid = jnp.where(sel_parts[k], pos_parts[k].astype(I32) - 1, -1)
        sid_ref[:, :, sl] = sid.reshape(B, E, blk)
        gate_ref[:, :, sl] = jnp.where(sel_parts[k], aff[:, sl], 0.0).reshape(B, E, blk)


def _topk(logits, cap):
    B, E, T = logits.shape
    full = pl.BlockSpec((B, E, T), lambda i: (0, 0, 0))
    return pl.pallas_call(
        functools.partial(_topk_kernel, cap=cap),
        grid=(1,),
        in_specs=[full],
        out_specs=[full, full],
        out_shape=[jax.ShapeDtypeStruct((B, E, T), I32), jax.ShapeDtypeStruct((B, E, T), F32)],
        compiler_params=_cp("arbitrary"),
        name=f"topk_{T}",
    )(logits)


GATHER_COLS = 512


def _gather_kernel(h_ref, sid_ref, gate_ref, xs_ref, gs_ref, *, cap, chunk):
    T = h_ref.shape[1]
    E = sid_ref.shape[1]
    acc = jnp.zeros((E * cap, GATHER_COLS), F32)
    slot = lax.broadcasted_iota(I32, (cap, chunk), 0)
    for k in range(T // chunk):
        sl = slice(k * chunk, (k + 1) * chunk)
        hit = jnp.concatenate([(slot == sid_ref[0, e, :, sl]).astype(BF16) for e in range(E)], axis=0)
        acc = acc + _dot(hit, h_ref[0, sl, :])
    for e in range(E):
        xs_ref[e] = acc[e * cap:(e + 1) * cap].astype(BF16)

    @pl.when(pl.program_id(1) == 0)
    def _():
        slot_t = lax.broadcasted_iota(I32, (cap, T), 0)
        for e in range(E):
            picked = jnp.where(slot_t == sid_ref[0, e], gate_ref[0, e], 0.0)
            gs_ref[e] = jnp.broadcast_to(jnp.sum(picked, axis=1, keepdims=True), (cap, 128))


def _gather(h, sid, gate, cap, row_block):
    B, E, T = sid.shape
    chunk = min(T, 512)
    idx = lambda b, j: (b, 0, 0, 0)
    return pl.pallas_call(
        functools.partial(_gather_kernel, cap=cap, chunk=chunk),
        grid=(B, D // GATHER_COLS),
        in_specs=[pl.BlockSpec((1, T, GATHER_COLS), lambda b, j: (b, row_block, j)),
                  pl.BlockSpec((1, E, 1, T), idx), pl.BlockSpec((1, E, 1, T), idx)],
        out_specs=[pl.BlockSpec((E, cap, GATHER_COLS), lambda b, j: (0, b, j)),
                   pl.BlockSpec((E, cap, 128), lambda b, j: (0, b, 0))],
        out_shape=[jax.ShapeDtypeStruct((E, B * cap, D), BF16),
                   jax.ShapeDtypeStruct((E, B * cap, 128), F32)],
        compiler_params=_cp("parallel", "arbitrary"),
        name=f"gather_{T}",
    )(h, sid.reshape(B, E, 1, T), gate.reshape(B, E, 1, T))


DOWN_CHUNK = 256


def _ffn_kernel(*refs, row_tiles):
    n_seg = len(row_tiles)
    xs_refs = refs[:n_seg]
    gs_refs = refs[n_seg:2 * n_seg]
    wg_ref, wu_ref, wd_ref = refs[2 * n_seg:2 * n_seg + 3]
    y_refs = refs[2 * n_seg + 3:3 * n_seg + 3]
    hid_refs = refs[3 * n_seg + 3:]
    s = pl.program_id(1)
    nf = EXPERT_FF // FF_CHUNK

    @pl.when(s < nf)
    def _():
        wg = wg_ref[0].astype(BF16)
        wu = wu_ref[0].astype(BF16)
        for xs_ref, hid_ref, rt in zip(xs_refs, hid_refs, row_tiles):
            for r0 in range(0, xs_ref.shape[1], rt):
                rows = pl.ds(r0, rt)
                x = xs_ref[0, rows, :]
                g = _dot(x, wg)
                u = _dot(x, wu)
                hid_ref[s, rows, :] = (g * jax.nn.sigmoid(g) * u).astype(BF16)

    @pl.when(s >= nf)
    def _():
        d = s - nf
        wd = wd_ref[0].astype(BF16)
        for gs_ref, y_ref, hid_ref, rt in zip(gs_refs, y_refs, hid_refs, row_tiles):
            for r0 in range(0, hid_ref.shape[1], rt):
                rows = pl.ds(r0, rt)
                acc = _dot(hid_ref[0, rows, :], wd[0:FF_CHUNK, :])
                for f in range(1, nf):
                    acc = acc + _dot(hid_ref[f, rows, :], wd[f * FF_CHUNK:(f + 1) * FF_CHUNK, :])
                gs = gs_ref[0, rows, :]
                y_ref[0, d, rows, :] = jnp.concatenate(
                    [acc[:, k * 128:(k + 1) * 128] * gs for k in range(DOWN_CHUNK // 128)], axis=1).astype(y_ref.dtype)


def _ffn(xs_list, gs_list, w_gate, w_up, w_down, layer):
    E = w_gate.shape[1]
    row_tiles = tuple(min(xs.shape[1], 512) for xs in xs_list)
    nf = EXPERT_FF // FF_CHUNK
    nd = D // DOWN_CHUNK
    in_specs = [pl.BlockSpec((1, xs.shape[1], D), lambda e, s: (e, 0, 0)) for xs in xs_list]
    in_specs += [pl.BlockSpec((1, gs.shape[1], 128), lambda e, s: (e, 0, 0)) for gs in gs_list]
    up_map = lambda e, s: (layer, e, 0, jnp.minimum(s, nf - 1))
    in_specs += [pl.BlockSpec((None, 1, D, FF_CHUNK), up_map),
                 pl.BlockSpec((None, 1, D, FF_CHUNK), up_map),
                 pl.BlockSpec((None, 1, EXPERT_FF, DOWN_CHUNK), lambda e, s: (layer, e, 0, jnp.maximum(s - nf, 0)))]
    return pl.pallas_call(
        functools.partial(_ffn_kernel, row_tiles=row_tiles),
        grid=(E, nf + nd),
        in_specs=in_specs,
        out_specs=[pl.BlockSpec((1, nd, xs.shape[1], DOWN_CHUNK), lambda e, s: (e, 0, 0, 0)) for xs in xs_list],
        out_shape=[jax.ShapeDtypeStruct((E, nd, xs.shape[1], DOWN_CHUNK), BF16) for xs in xs_list],
        scratch_shapes=[pltpu.VMEM((nf, xs.shape[1], FF_CHUNK), BF16) for xs in xs_list],
        compiler_params=_cp("parallel", "arbitrary"),
        name="expert_ffn",
    )(*xs_list, *gs_list, w_gate, w_up, w_down)


def _combine_kernel(x_ref, sid_ref, y_ref, mod_ref, o_ref, *, cap):
    tbs = x_ref.shape[1]
    E = sid_ref.shape[1]
    slot = lax.broadcasted_iota(I32, (cap, tbs), 0)
    hit = jnp.concatenate([(slot == sid_ref[0, e]).astype(BF16) for e in range(E)], axis=0)
    gate = mod_ref[0][5:6]
    for d in range(D // DOWN_CHUNK):
        cs = slice(d * DOWN_CHUNK, (d + 1) * DOWN_CHUNK)
        z = _dot_tn(hit, y_ref[:, d].reshape(E * cap, DOWN_CHUNK))
        o_ref[0, :, cs] = x_ref[0, :, cs] + gate[:, cs] * z


def _combine(x, sid, y, mod, cap, tbs, row_block0, segment):
    B, E, T = sid.shape
    ntb = T // tbs
    xmap = lambda b, tb: (b, row_block0 + tb, 0)
    return pl.pallas_call(
        functools.partial(_combine_kernel, cap=cap),
        grid=(B, ntb),
        in_specs=[pl.BlockSpec((1, tbs, D), xmap),
                  pl.BlockSpec((1, E, 1, tbs), lambda b, tb: (b, 0, 0, tb)),
                  pl.BlockSpec((E, D // DOWN_CHUNK, cap, DOWN_CHUNK), lambda b, tb: (0, 0, b, 0)),
                  pl.BlockSpec((1, 8, D), lambda b, tb: (b * 2 + segment, 0, 0))],
        out_specs=pl.BlockSpec((1, tbs, D), xmap),
        out_shape=jax.ShapeDtypeStruct(x.shape, F32),
        input_output_aliases={0: 0},
        compiler_params=_cp("parallel", "parallel"),
        name=f"combine_{T}",
    )(x, sid.reshape(B, E, 1, T), y, mod)


def _moe(x, h, logits, mod, w_gate, w_up, w_down, layer, need_ctx):
    sid_l, gate_l = _topk(logits[:, :, :T_L], CAP_L)
    xs_l, gs_l = _gather(h, sid_l, gate_l, CAP_L, 0)
    if need_ctx:
        sid_c, gate_c = _topk(logits[:, :, T_L:], CAP_C)
        xs_c, gs_c = _gather(h, sid_c, gate_c, CAP_C, T_L // T_C)
        y_l, y_c = _ffn([xs_l, xs_c], [gs_l, gs_c], w_gate, w_up, w_down, layer)
    else:
        (y_l,) = _ffn([xs_l], [gs_l], w_gate, w_up, w_down, layer)
    x = _combine(x, sid_l, y_l, mod, CAP_L, 512, 0, 0)
    if need_ctx:
        x = _combine(x, sid_c, y_c, mod, CAP_C, T_C, T_L // T_C, 1)
    return x


OD_N = RWKV_IN + MLA_U


def _odd_weights(w_in, partner):
    rw = w_in[:, :RWKV_IN]
    qd = w_in[:, RWKV_IN:RWKV_IN + MLA_Q_RANK]
    kvd = w_in[:, RWKV_IN + MLA_Q_RANK:RWKV_IN + MLA_Q_RANK + MLA_KV_RANK]
    kr = w_in[:, RWKV_IN + MLA_Q_RANK + MLA_KV_RANK:]
    pad = jnp.zeros((D, 128 - MLA_ROPE), w_in.dtype)
    w = jnp.concatenate([rw, qd, kvd, kr, pad, kr[:, partner], pad], axis=1)
    return w.astype(BF16)


def _inproj_odd_kernel(x_ref, mod_ref, g_ref, w_ref, ur_ref, um_ref):
    mod = mod_ref[0]
    h = _modulate(x_ref[0], g_ref[...], mod[0:1], mod[1:2])
    y = _dot(h.astype(BF16), w_ref[...])
    ur_ref[0] = y[:, :RWKV_IN]
    um_ref[0] = y[:, RWKV_IN:]


def _inproj_odd(x, mod, g, w):
    B = x.shape[0]
    tok = lambda n: pl.BlockSpec((1, TM, n), lambda b, t: (b, t, 0))
    return pl.pallas_call(
        _inproj_odd_kernel,
        grid=(B, N_TILES),
        in_specs=[tok(D), _mod_spec(), pl.BlockSpec((1, D), lambda b, t: (0, 0)),
                  pl.BlockSpec((D, OD_N), lambda b, t: (0, 0))],
        out_specs=[tok(RWKV_IN), tok(MLA_U)],
        out_shape=[jax.ShapeDtypeStruct((B, NT, RWKV_IN), F32),
                   jax.ShapeDtypeStruct((B, NT, MLA_U), F32)],
        compiler_params=_cp("parallel", "parallel"),
        name="inproj_odd",
    )(x, mod, g.reshape(1, D), w)


def _head_ones():
    h = np.arange(RWKV_WIDTH) // RWKV_HEAD
    return jnp.asarray((h[:, None] == h[None, :]).astype(np.float32)).astype(BF16)


def _headsum(x, bd):
    return _dot2(x, bd)


def _rwkv_prep_kernel(u_ref, up_ref, un_ref, mup_ref, mun_ref, w0_ref, w2_ref, a0_ref, a2_ref, g2_ref,
                      kk_ref, ka_ref, bd_ref,
                      r_ref, v_ref, kkn_ref, g_ref, lw_ref, bq_ref, kd_ref):
    t = pl.program_id(1)
    u = u_ref[0]
    rows = lax.broadcasted_iota(I32, (TM, 1), 0)
    is_ctx = t == N_LAT_TILES
    seg_start = (t == 0) | is_ctx
    seg_end = (t == N_LAT_TILES - 1) | is_ctx
    last_row = jnp.where(is_ctx, T_C - 1, TM - 1)
    prev_row = jnp.where(seg_start, 0.0, up_ref[0, 7:8, :])
    next_row = jnp.where(seg_end, 0.0, un_ref[0, 0:1, :])
    prev = jnp.where(rows == 0, prev_row, pltpu.roll(u, 1, 0))
    nxt = jnp.where(rows == last_row, next_row, pltpu.roll(u, TM - 1, 0))
    us = u + mup_ref[...] * (prev - u) + mun_ref[...] * (nxt - u)
    W = RWKV_WIDTH
    r = us[:, :W]
    k = us[:, W:2 * W]
    v = us[:, 2 * W:3 * W]
    wl = jnp.tanh(us[:, 3 * W:3 * W + 128]).astype(BF16)
    al = us[:, 3 * W + 128:3 * W + 256].astype(BF16)
    gl = jax.nn.sigmoid(us[:, 3 * W + 256:3 * W + 384]).astype(BF16)
    bd = bd_ref[...]
    kk = k * kk_ref[...]
    kk = kk * lax.rsqrt(jnp.maximum(_headsum(kk * kk, bd), 1e-24))
    r_ref[0] = r
    v_ref[0] = v
    kkn_ref[0] = kk
    g_ref[0] = _dot(gl, g2_ref[...])
    for d in range(2):
        z = w0_ref[d:d + 1, :] + _dot(wl, w2_ref[d])
        a = jax.nn.sigmoid(a0_ref[d:d + 1, :] + _dot(al, a2_ref[d]))
        lw_ref[d, 0] = jax.nn.sigmoid(z) * (-float(np.exp(-0.5)))
        bq_ref[d, 0] = kk * a
        kd_ref[d, 0] = k * (1.0 + (a - 1.0) * ka_ref[...])


def _rwkv_prep(ur, mu_prev, mu_next, w0, w2, a0, a2, g2, k_k, k_a, bd):
    B = ur.shape[0]
    W = RWKV_WIDTH
    z = jnp.zeros((64, W), F32)
    w2p = jnp.stack([jnp.concatenate([w2[0], z]), jnp.concatenate([z, w2[1]])]).astype(BF16)
    a2p = jnp.stack([jnp.concatenate([a2[0], z]), jnp.concatenate([z, a2[1]])]).astype(BF16)
    tok = lambda n: pl.BlockSpec((1, TM, n), lambda b, t: (b, t, 0))
    tok2 = pl.BlockSpec((2, 1, TM, W), lambda b, t: (0, b, t, 0))
    row = lambda n: pl.BlockSpec((1, n), lambda b, t: (0, 0))
    full = lambda *s: pl.BlockSpec(s, lambda b, t: (0,) * len(s))
    slab = TM // 8
    f32 = lambda: jax.ShapeDtypeStruct((B, NT, W), F32)
    f32d = lambda: jax.ShapeDtypeStruct((2, B, NT, W), F32)
    return pl.pallas_call(
        _rwkv_prep_kernel,
        grid=(B, N_TILES),
        in_specs=[tok(RWKV_IN),
                  pl.BlockSpec((1, 8, RWKV_IN), lambda b, t: (b, jnp.maximum(t * slab - 1, 0), 0)),
                  pl.BlockSpec((1, 8, RWKV_IN), lambda b, t: (b, jnp.minimum((t + 1) * slab, NT // 8 - 1), 0)),
                  row(RWKV_IN), row(RWKV_IN), full(2, W), full(2, 128, W), full(2, W), full(2, 128, W),
                  full(128, W), row(W), row(W), full(W, W)],
        out_specs=[tok(W), tok(W), tok(W), tok(W), tok2, tok2, tok2],
        out_shape=[f32(), f32(), f32(), f32(), f32d(), f32d(), f32d()],
        compiler_params=_cp("parallel", "parallel"),
        name="rwkv_prep",
    )(ur, ur, ur, mu_prev.reshape(1, RWKV_IN), mu_next.reshape(1, RWKV_IN), w0, w2p, a0, a2p,
      g2.astype(BF16), k_k.reshape(1, W), k_a.reshape(1, W), bd)


def _wkv_kernel(r_ref, v_ref, kk_ref, lw_ref, bq_ref, kd_ref, o_ref, s_ref, *, nb):
    d = pl.program_id(0)
    i = pl.program_id(2)
    C = WKV_CHUNK
    P = RWKV_HEADS // 2
    G = nb * P
    HD = RWKV_HEAD

    @pl.when(i == 0)
    def _():
        s_ref[...] = jnp.zeros_like(s_ref)

    sgn = 1 - 2 * d
    rel = (lax.broadcasted_iota(I32, (C, C), 0) - lax.broadcasted_iota(I32, (C, C), 1)) * sgn
    incl_b = (rel >= 0).astype(BF16)
    row_w = lax.broadcasted_iota(I32, (C, 2 * HD), 0)
    col_w = lax.broadcasted_iota(I32, (C, 2 * HD), 1) & (HD - 1)
    rel_w = (row_w - col_w) * sgn
    incl = rel_w >= 0
    strict = rel_w > 0
    eye = (rel_w == 0).astype(F32)
    lo = lax.broadcasted_iota(I32, (C, 2 * HD), 1) < HD
    same_head = ((lax.broadcasted_iota(I32, (2 * HD, 2 * HD), 0) < HD)
                 == (lax.broadcasted_iota(I32, (2 * HD, 2 * HD), 1) < HD))

    def same_block(k):
        return (row_w >> k) == (col_w >> k)

    def pair(y):
        zero = jnp.zeros_like(y)
        return jnp.concatenate([jnp.where(lo, y, zero), jnp.where(lo, zero, y)], axis=0)

    at, rt, bt, kt, a0, r0, bh, kh, v, p_end = ([] for _ in range(10))
    for b in range(nb):
        lw = lw_ref[0, b]
        lw_hi, lw_lo = _split_bf16(lw)
        ci = _dot(incl_b, lw_hi) + _dot(incl_b, lw_lo)
        ce = ci - lw
        tot = jnp.sum(lw, axis=0, keepdims=True)
        m = 0.5 * tot
        r = r_ref[b]
        kk = kk_ref[b]
        bq = bq_ref[0, b]
        kd = kd_ref[0, b]
        e_ci = jnp.exp(ci - m)
        e_mci = jnp.exp(m - ci)
        e_end = jnp.exp(tot - ci)
        full = ((at, (-kk * jnp.exp(ce - m)).astype(BF16)), (rt, (r * e_ci).astype(BF16)),
                (bt, (bq * e_mci).astype(BF16)), (kt, (kd * e_mci).astype(BF16)),
                (a0, (-kk * jnp.exp(ce)).astype(BF16)), (r0, (r * (e_ci * jnp.exp(m))).astype(BF16)),
                (bh, (bq * e_end).astype(BF16)), (kh, (kd * e_end).astype(BF16)),
                (v, v_ref[b].astype(BF16)), (p_end, jnp.exp(tot)))
        for dst, x in full:
            dst.extend(x[:, p * 2 * HD:(p + 1) * 2 * HD] for p in range(P))

    S = [s_ref[g] for g in range(G)]
    s_b = [s.astype(BF16) for s in S]
    left = [jnp.concatenate([at[g], rt[g]], axis=0) for g in range(G)]
    nb_ = [_dot_nt(left[g], pair(bt[g])) for g in range(G)]
    nk_ = [_dot_nt(left[g], pair(kt[g])) for g in range(G)]
    n_ab = [jnp.where(strict, x[:C], 0.0) for x in nb_]
    a_rb = [jnp.where(incl, x[C:], 0.0).astype(BF16) for x in nb_]
    mask_k = jnp.concatenate([strict, incl], axis=0)
    a_k = [jnp.where(mask_k, x, 0.0).astype(BF16) for x in nk_]
    from_state = [_dot_nt(jnp.concatenate([a0[g], r0[g]], axis=0), s_b[g]) for g in range(G)]
    from_v = [_dot(a_k[g], pair(v[g])) for g in range(G)]
    w_rhs = [from_state[g][:C] + from_v[g][:C] for g in range(G)]
    inv = [eye + jnp.where(same_block(1), n, 0.0) for n in n_ab]
    for k in range(1, 6):
        off_mask = same_block(k + 1) & jnp.logical_not(same_block(k))
        inv_b = [x.astype(BF16) for x in inv]
        e = [_dot(jnp.where(off_mask, n, 0.0).astype(BF16), pair(x)) for n, x in zip(n_ab, inv_b)]
        inv = [x + _dot(xb, pair(y.astype(BF16))) for x, xb, y in zip(inv, inv_b, e)]
    U = [_dot(inv[g].astype(BF16), pair(w_rhs[g].astype(BF16))).astype(BF16) for g in range(G)]
    for g in range(G):
        o = from_state[g][C:] + _dot(a_rb[g], pair(U[g])) + from_v[g][C:]
        o_ref[0, g // P, :, (g % P) * 2 * HD:(g % P + 1) * 2 * HD] = o
    for g in range(G):
        upd = _dot_tn(jnp.concatenate([U[g], v[g]], axis=0), jnp.concatenate([bh[g], kh[g]], axis=0))
        s_ref[g] = S[g] * p_end[g] + jnp.where(same_head, upd, 0.0)


WKV_BATCH = 4


def _wkv(r, v, kk, lw, bq, kd, nb=WKV_BATCH):
    B = r.shape[0]
    C = WKV_CHUNK
    n_lat = T_L // C

    def chunk(d, i):
        fwd = jnp.where(i < N_CTX_CHUNKS, n_lat + i, i - N_CTX_CHUNKS)
        return jnp.where(d == 0, fwd, N_CHUNKS - 1 - i)

    shared = pl.BlockSpec((nb, C, RWKV_WIDTH), lambda d, b, i: (b, chunk(d, i), 0))
    perdir = pl.BlockSpec((1, nb, C, RWKV_WIDTH), lambda d, b, i: (d, b, chunk(d, i), 0))
    return pl.pallas_call(
        functools.partial(_wkv_kernel, nb=nb),
        grid=(2, B // nb, N_CHUNKS),
        in_specs=[shared, shared, shared, perdir, perdir, perdir],
        out_specs=perdir,
        out_shape=jax.ShapeDtypeStruct((2, B, NT, RWKV_WIDTH), F32),
        scratch_shapes=[pltpu.VMEM((nb * RWKV_HEADS // 2, 2 * RWKV_HEAD, 2 * RWKV_HEAD), F32)],
        compiler_params=_cp("parallel", "parallel", "arbitrary"),
        name="wkv7",
    )(r, v, kk, lw, bq, kd)


def _rwkv_finish_kernel(o_ref, r_ref, v_ref, g_ref, kd_ref, rk_ref, lg_ref, lb_ref, bd_ref, y_ref):
    bd = bd_ref[...]
    o = o_ref[0, 0] + o_ref[1, 0]
    mu = _headsum(o, bd) * (1.0 / RWKV_HEAD)
    xc = o - mu
    var = _headsum(xc * xc, bd) * (1.0 / RWKV_HEAD)
    on = xc * lax.rsqrt(var + LNX_EPS) * lg_ref[...] + lb_ref[...]
    bonus = _headsum(r_ref[0] * (kd_ref[0, 0] + kd_ref[1, 0]) * rk_ref[...], bd) * v_ref[0]
    y_ref[0] = ((on + bonus) * g_ref[0]).astype(y_ref.dtype)


def _rwkv_finish(o, r, v, g, kd, r_k, lnx_g, lnx_b, bd, need_ctx):
    B = r.shape[0]
    W = RWKV_WIDTH
    tok = pl.BlockSpec((1, TM, W), lambda b, t: (b, t, 0))
    tok2 = pl.BlockSpec((2, 1, TM, W), lambda b, t: (0, b, t, 0))
    row = pl.BlockSpec((1, W), lambda b, t: (0, 0))
    return pl.pallas_call(
        _rwkv_finish_kernel,
        grid=(B, N_TILES if need_ctx else N_LAT_TILES),
        in_specs=[tok2, tok, tok, tok, tok2, row, row, row, pl.BlockSpec((W, W), lambda b, t: (0, 0))],
        out_specs=tok,
        out_shape=jax.ShapeDtypeStruct((B, NT, W), BF16),
        compiler_params=_cp("parallel", "parallel"),
        name="rwkv_finish",
    )(o, r, v, g, kd, r_k.reshape(1, W), lnx_g.reshape(1, W), lnx_b.reshape(1, W), bd)


MLA_QW = MLA_HEADS * MLA_SLOT


def _mla_weights(w_uq, w_ukv, partner):
    dq = MLA_NOPE + MLA_ROPE
    wq = jnp.zeros((MLA_Q_RANK, 2 * MLA_QW), F32)
    wk = jnp.zeros((MLA_KV_RANK, MLA_QW), F32)
    wv = jnp.zeros((MLA_KV_RANK, MLA_HEADS * MLA_V), F32)
    place = np.zeros((128, MLA_QW), np.float32)
    for h in range(MLA_HEADS):
        qh = w_uq[:, h * dq:(h + 1) * dq]
        wq = wq.at[:, h * MLA_SLOT:h * MLA_SLOT + dq].set(qh)
        wq = wq.at[:, MLA_QW + h * MLA_SLOT + MLA_NOPE:MLA_QW + h * MLA_SLOT + dq].set(qh[:, MLA_NOPE + partner])
        kvh = w_ukv[:, h * (MLA_NOPE + MLA_V):(h + 1) * (MLA_NOPE + MLA_V)]
        wk = wk.at[:, h * MLA_SLOT:h * MLA_SLOT + MLA_NOPE].set(kvh[:, :MLA_NOPE])
        wv = wv.at[:, h * MLA_V:(h + 1) * MLA_V].set(kvh[:, MLA_NOPE:])
        place[np.arange(MLA_ROPE), h * MLA_SLOT + MLA_NOPE + np.arange(MLA_ROPE)] = 1.0
    return wq.astype(BF16), wk.astype(BF16), wv.astype(BF16), jnp.asarray(place).astype(BF16)


def _mla_proj_kernel(u_ref, qg_ref, kg_ref, wq_ref, wk_ref, wv_ref, pl_ref, cq_ref, sq_ref, ck_ref, sk_ref,
                     q_ref, k_ref, v_ref):
    u = u_ref[0]
    qn = _rmsnorm(u[:, :MLA_Q_RANK], qg_ref[...]).astype(BF16)
    kvn = _rmsnorm(u[:, MLA_Q_RANK:MLA_Q_RANK + MLA_KV_RANK], kg_ref[...]).astype(BF16)
    o = MLA_Q_RANK + MLA_KV_RANK
    kr = u[:, o:o + 128] * ck_ref[...] + u[:, o + 128:o + 256] * sk_ref[...]
    qq = _dot(qn, wq_ref[...])
    q_ref[0] = (qq[:, :MLA_QW] * cq_ref[...] + qq[:, MLA_QW:] * sq_ref[...]).astype(BF16)
    k_ref[0] = (_dot(kvn, wk_ref[...]) + _dot2(kr, pl_ref[...])).astype(BF16)
    v_ref[0] = _dot(kvn, wv_ref[...]).astype(BF16)


def _mla_proj(um, q_norm_g, kv_norm_g, weights, tables):
    B = um.shape[0]
    wq, wk, wv, place = weights
    cq, sq, ck, sk = tables
    tok = lambda n: pl.BlockSpec((1, TM, n), lambda b, t: (b, t, 0))
    tab = lambda n: pl.BlockSpec((TM, n), lambda b, t: (t, 0))
    full = lambda a: pl.BlockSpec(a.shape, lambda b, t: (0, 0))
    return pl.pallas_call(
        _mla_proj_kernel,
        grid=(B, N_TILES),
        in_specs=[tok(MLA_U), pl.BlockSpec((1, MLA_Q_RANK), lambda b, t: (0, 0)),
                  pl.BlockSpec((1, MLA_KV_RANK), lambda b, t: (0, 0)),
                  full(wq), full(wk), full(wv), full(place),
                  tab(MLA_QW), tab(MLA_QW), tab(MLA_SLOT), tab(MLA_SLOT)],
        out_specs=[tok(MLA_QW), tok(MLA_QW), tok(MLA_HEADS * MLA_V)],
        out_shape=[jax.ShapeDtypeStruct((B, NT, MLA_QW), BF16),
                   jax.ShapeDtypeStruct((B, NT, MLA_QW), BF16),
                   jax.ShapeDtypeStruct((B, NT, MLA_HEADS * MLA_V), BF16)],
        compiler_params=_cp("parallel", "parallel"),
        name="mla_proj",
    )(um, q_norm_g.reshape(1, -1), kv_norm_g.reshape(1, -1), wq, wk, wv, place, cq, sq, ck, sk)


MLA_TQ = 512


def _mla_attn_kernel(q_ref, k_ref, v_ref, o_ref):
    t = pl.program_id(2)
    is_ctx = t == T_L // MLA_TQ
    lo_lanes = lax.broadcasted_iota(I32, (MLA_TQ, 128), 1) < MLA_V

    def attend(keys):
        n_keys = keys.stop - keys.start
        lo_rows = lax.broadcasted_iota(I32, (n_keys, 128), 1) < MLA_V
        vall = v_ref[0, keys, :]
        one = jnp.ones((n_keys, 128), vall.dtype)
        vh = [jnp.where(lo_rows, vall, one), jnp.where(lo_rows, one, vall)]
        s = [_dot_nt(q_ref[0, :, hh * MLA_SLOT:(hh + 1) * MLA_SLOT], k_ref[0, keys, hh * MLA_SLOT:(hh + 1) * MLA_SLOT])
             for hh in range(2)]
        e = [jnp.exp(x - jnp.max(x, axis=-1, keepdims=True)).astype(BF16) for x in s]
        r = [_dot(e[hh], vh[hh]) for hh in range(2)]
        num = jnp.where(lo_lanes, r[0], r[1])
        den = pltpu.roll(jnp.where(lo_lanes, r[1], r[0]), MLA_V, 1)
        o_ref[0] = (num * (1.0 / den)).astype(o_ref.dtype)

    @pl.when(jnp.logical_not(is_ctx))
    def _():
        attend(slice(0, NT))

    @pl.when(is_ctx)
    def _():
        attend(slice(T_L, NT))


def _mla_attn(q, k, v, need_ctx):
    B = q.shape[0]
    n_tiles = pl.cdiv(NT, MLA_TQ) if need_ctx else T_L // MLA_TQ
    return pl.pallas_call(
        _mla_attn_kernel,
        grid=(B, MLA_HEADS // 2, n_tiles),
        in_specs=[pl.BlockSpec((1, MLA_TQ, 2 * MLA_SLOT), lambda b, p, t: (b, t, p)),
                  pl.BlockSpec((1, NT, 2 * MLA_SLOT), lambda b, p, t: (b, 0, p)),
                  pl.BlockSpec((1, NT, 2 * MLA_V), lambda b, p, t: (b, 0, p))],
        out_specs=pl.BlockSpec((1, MLA_TQ, 2 * MLA_V), lambda b, p, t: (b, t, p)),
        out_shape=jax.ShapeDtypeStruct((B, NT, MLA_HEADS * MLA_V), BF16),
        compiler_params=_cp("parallel", "parallel", "parallel"),
        name="mla_attn",
    )(q, k, v)


def _final_norm_kernel(x_ref, g_ref, o_ref):
    o_ref[0] = _rmsnorm(x_ref[0], g_ref[...])


def _final_norm(x, g):
    B = x.shape[0]
    tok = pl.BlockSpec((1, TM, D), lambda b, t: (b, t, 0))
    return pl.pallas_call(
        _final_norm_kernel,
        grid=(B, N_LAT_TILES),
        in_specs=[tok, pl.BlockSpec((1, D), lambda b, t: (0, 0))],
        out_specs=tok,
        out_shape=jax.ShapeDtypeStruct((B, T_L, D), F32),
        compiler_params=_cp("parallel", "parallel"),
        name="final_norm",
    )(x, g.reshape(1, D))


def kernel(x, c, ctx, c_ctx, ada_w, ada_b, norm_mix_g, norm_ffn_g, router_w, exp_w_gate, exp_w_up, exp_w_down, ev_w_in, ev_w_out, ev_sink, ev_pool_w, ev_pool_scale, od_w_in, od_w_out, od_mu_prev, od_mu_next, od_w0, od_w2, od_a0, od_a2, od_g2, od_k_k, od_k_a, od_r_k, od_lnx_g, od_lnx_b, od_q_norm_g, od_w_uq, od_kv_norm_g, od_w_ukv, final_norm_g):
    depth = ada_w.shape[0]
    assert x.shape[1:] == (T_L, D) and ctx.shape[1:] == (T_C, D)
    mods = _modvec(c, c_ctx, ada_w, ada_b)
    ev_tables, ev_partner = _even_tables()
    mla_tables, mla_partner = _mla_tables()
    bd = _head_ones()
    xs = jnp.concatenate([x, ctx], axis=1)
    for l in range(depth):
        need_ctx = l < depth - 1
        i = l // 2
        mod = mods[l]
        if l % 2 == 0:
            q, kd, vd, pu = _inproj_even(xs, mod, norm_mix_g[l], _even_weights(ev_w_in[i], ev_partner), ev_tables)
            mix_a = _attn_even(q, kd, vd, ev_sink[i])
            mix_b = _pool(pu, ev_pool_w[i], ev_pool_scale[i])
            w_out = ev_w_out[i]
        else:
            ur, um = _inproj_odd(xs, mod, norm_mix_g[l], _odd_weights(od_w_in[i], mla_partner))
            r, v, kk, g, lw, bq, kdir = _rwkv_prep(ur, od_mu_prev[i], od_mu_next[i], od_w0[i], od_w2[i],
                                                   od_a0[i], od_a2[i], od_g2[i], od_k_k[i], od_k_a[i], bd)
            o = _wkv(r, v, kk, lw, bq, kdir)
            mix_a = _rwkv_finish(o, r, v, g, kdir, od_r_k[i], od_lnx_g[i], od_lnx_b[i], bd, need_ctx)
            mq, mk, mv = _mla_proj(um, od_q_norm_g[i], od_kv_norm_g[i],
                                   _mla_weights(od_w_uq[i], od_w_ukv[i], mla_partner), mla_tables)
            mix_b = _mla_attn(mq, mk, mv, need_ctx)
            w_out = od_w_out[i]
        xs, h, logits = _outproj(mix_a, mix_b, w_out, xs, mod, norm_ffn_g[l], router_w[l], need_ctx)
        xs = _moe(xs, h, logits, mod, exp_w_gate, exp_w_up, exp_w_down, l, need_ctx)
    return _final_norm(xs, final_norm_g)
```

```python
import functools

import numpy as np
import jax
import jax.numpy as jnp
from jax import lax
from jax.experimental import pallas as pl
from jax.experimental.pallas import tpu as pltpu

F32 = jnp.float32
BF16 = jnp.bfloat16
I32 = jnp.int32

D = 1024
T_L = 2048
T_C = 256
NT = T_L + T_C
GRID_W = 64
EPS = 1e-6
ROPE_BASE = 10000.0

TM = 512
N_LAT_TILES = T_L // TM
N_TILES = N_LAT_TILES + 1
assert T_C <= TM and T_L % TM == 0

A_HEADS = 8
A_KV_HEADS = 2
HEAD_DIM = 64
A_WINDOW = 128
A_BLOCK = 128
N_ABLK = NT // A_BLOCK
N_LAT_ABLK = T_L // A_BLOCK

POOL_WINDOWS = (2, 4, 8, 16)
POOL_GROUP = 128

RWKV_HEADS = 8
RWKV_HEAD = 64
RWKV_WIDTH = 512
RWKV_IN = 1920
LNX_EPS = 64e-5
WKV_CHUNK = 64
N_CHUNKS = NT // WKV_CHUNK
N_CTX_CHUNKS = T_C // WKV_CHUNK

MLA_HEADS = 8
MLA_Q_RANK = 256
MLA_KV_RANK = 128
MLA_NOPE = 64
MLA_ROPE = 32
MLA_V = 64
MLA_SLOT = 128
MLA_U = 640

N_EXPERTS = 16
EXPERT_FF = 2048
CAP_L = 2 * T_L // N_EXPERTS
CAP_C = 2 * T_C // N_EXPERTS
FF_CHUNK = 512

VMEM_LIMIT_BYTES = 56 * 1024 * 1024


def _cp(*sem):
    return pltpu.CompilerParams(dimension_semantics=sem, vmem_limit_bytes=VMEM_LIMIT_BYTES)


def _split_bf16(x):
    hi = x.astype(BF16)
    lo = (x - hi.astype(F32)).astype(BF16)
    return hi, lo


def _dot(a, b):
    return jnp.dot(a, b, preferred_element_type=F32)


def _dot_nt(a, b):
    return lax.dot_general(a, b, (((1,), (1,)), ((), ())), preferred_element_type=F32)


def _dot_tn(a, b):
    return lax.dot_general(a, b, (((0,), (0,)), ((), ())), preferred_element_type=F32)


def _dot2(a_f32, b_bf16):
    hi, lo = _split_bf16(a_f32)
    return _dot(hi, b_bf16) + _dot(lo, b_bf16)


def _modulate(x, g, shift, scale):
    ms = jnp.mean(x * x, axis=-1, keepdims=True)
    return x * lax.rsqrt(ms + EPS) * g * (1.0 + scale) + shift


def _rmsnorm(x, g):
    ms = jnp.mean(x * x, axis=-1, keepdims=True)
    return x * lax.rsqrt(ms + EPS) * g


def _modvec_kernel(a_ref, w_ref, b_ref, o_ref):
    a = a_ref[...]
    a = a * jax.nn.sigmoid(a)
    a_hi, a_lo = _split_bf16(a)
    w_hi, w_lo = _split_bf16(w_ref[0])
    o_ref[0] = _dot(a_hi, w_hi) + _dot(a_lo, w_hi) + _dot(a_hi, w_lo) + b_ref[0]


def _modvec(c, c_ctx, ada_w, ada_b):
    L = ada_w.shape[0]
    B = c.shape[0]
    assert B < 16
    rows = jnp.zeros((16, D), F32).at[:B].set(c).at[B].set(c_ctx)
    tn = 512
    out = pl.pallas_call(
        _modvec_kernel,
        grid=(L, 6 * D // tn),
        in_specs=[pl.BlockSpec((16, D), lambda l, n: (0, 0)),
                  pl.BlockSpec((1, D, tn), lambda l, n: (l, 0, n)),
                  pl.BlockSpec((1, 1, tn), lambda l, n: (l, 0, n))],
        out_specs=pl.BlockSpec((1, 16, tn), lambda l, n: (l, 0, n)),
        out_shape=jax.ShapeDtypeStruct((L, 16, 6 * D), F32),
        compiler_params=_cp("parallel", "parallel"),
        name="modvec",
    )(rows, ada_w, ada_b.reshape(L, 1, 6 * D))
    m = out.reshape(L, 16, 6, D)
    ml = m[:, :B]
    mc = jnp.broadcast_to(m[:, B:B + 1], (L, B, 6, D))
    mod = jnp.stack([ml, mc], axis=2)
    mod = jnp.pad(mod, ((0, 0), (0, 0), (0, 0), (0, 2), (0, 0)))
    return mod.reshape(L, B * 2, 8, D)


def _mod_spec():
    return pl.BlockSpec((1, 8, D), lambda b, t: (b * 2 + t // N_LAT_TILES, 0, 0))


def _rope_block(width, pos_row, pos_col):
    d = width // 2
    half = d // 2
    inv = ROPE_BASE ** (-np.arange(half, dtype=np.float32) / half)
    cos = np.zeros((pos_row.shape[0], width), np.float32)
    sin = np.zeros_like(cos)
    partner = np.zeros((width,), np.int64)
    for part, pos in ((0, pos_row), (1, pos_col)):
        ang = pos.astype(np.float32)[:, None] * inv[None, :]
        c, s = np.cos(ang), np.sin(ang)
        o = part * d
        cos[:, o:o + half] = c
        cos[:, o + half:o + d] = c
        sin[:, o:o + half] = -s
        sin[:, o + half:o + d] = s
        partner[o:o + half] = np.arange(o + half, o + d)
        partner[o + half:o + d] = np.arange(o, o + half)
    return cos, sin, partner


def _latent_positions():
    t = np.arange(T_L)
    return t // GRID_W, t % GRID_W


def _even_tables():
    cos, sin, partner = _rope_block(HEAD_DIM, *_latent_positions())
    scale = HEAD_DIM ** -0.5
    ones = np.ones((T_C, HEAD_DIM), np.float32)
    zeros = np.zeros((T_C, HEAD_DIM), np.float32)
    cos = np.concatenate([cos, ones], axis=0)
    sin = np.concatenate([sin, zeros], axis=0)
    cq = np.tile(cos, (1, A_HEADS)) * scale
    sq = np.tile(sin, (1, A_HEADS)) * scale
    ck = np.tile(cos, (1, 2 * A_KV_HEADS))
    sk = np.tile(sin, (1, 2 * A_KV_HEADS))
    return (jnp.asarray(cq), jnp.asarray(sq), jnp.asarray(ck), jnp.asarray(sk)), partner


def _mla_tables():
    cos, sin, partner = _rope_block(MLA_ROPE, *_latent_positions())
    cos = np.concatenate([cos, np.ones((T_C, MLA_ROPE), np.float32)], axis=0)
    sin = np.concatenate([sin, np.zeros((T_C, MLA_ROPE), np.float32)], axis=0)
    scale = (MLA_NOPE + MLA_ROPE) ** -0.5
    cq1 = np.zeros((NT, MLA_SLOT), np.float32)
    sq1 = np.zeros((NT, MLA_SLOT), np.float32)
    cq1[:, :MLA_NOPE] = 1.0
    cq1[:, MLA_NOPE:MLA_NOPE + MLA_ROPE] = cos
    sq1[:, MLA_NOPE:MLA_NOPE + MLA_ROPE] = sin
    cq = np.tile(cq1, (1, MLA_HEADS)) * scale
    sq = np.tile(sq1, (1, MLA_HEADS)) * scale
    ck = np.zeros((NT, MLA_SLOT), np.float32)
    sk = np.zeros((NT, MLA_SLOT), np.float32)
    ck[:, :MLA_ROPE] = cos
    sk[:, :MLA_ROPE] = sin
    return (jnp.asarray(cq), jnp.asarray(sq), jnp.asarray(ck), jnp.asarray(sk)), partner


EV_Q, EV_K, EV_V, EV_P = 512, 256, 256, 512
EV_N = 2 * EV_Q + 2 * EV_K + EV_V + EV_P


def _even_weights(w_in, partner):
    q = w_in[:, :512]
    k = w_in[:, 512:640]
    v = w_in[:, 640:768]
    p = w_in[:, 768:1280]
    pq = np.concatenate([h * HEAD_DIM + partner for h in range(A_HEADS)])
    pk = np.concatenate([h * HEAD_DIM + partner for h in range(A_KV_HEADS)])
    dup = np.concatenate([np.arange(HEAD_DIM), np.arange(HEAD_DIM),
                          HEAD_DIM + np.arange(HEAD_DIM), HEAD_DIM + np.arange(HEAD_DIM)])
    w = jnp.concatenate([q, q[:, pq], k[:, dup], k[:, pk][:, dup], v[:, dup], p], axis=1)
    return w.astype(BF16)


def _inproj_even_kernel(x_ref, mod_ref, g_ref, w_ref, cq_ref, sq_ref, ck_ref, sk_ref,
                        q_ref, k_ref, v_ref, p_ref):
    mod = mod_ref[0]
    h = _modulate(x_ref[0], g_ref[...], mod[0:1], mod[1:2])
    y = _dot(h.astype(BF16), w_ref[...])
    o = 0
    q = y[:, o:o + EV_Q] * cq_ref[...] + y[:, o + EV_Q:o + 2 * EV_Q] * sq_ref[...]
    o += 2 * EV_Q
    k = y[:, o:o + EV_K] * ck_ref[...] + y[:, o + EV_K:o + 2 * EV_K] * sk_ref[...]
    o += 2 * EV_K
    q_ref[0] = q.astype(BF16)
    k_ref[0] = k.astype(BF16)
    v_ref[0] = y[:, o:o + EV_V].astype(BF16)
    p_ref[0] = y[:, o + EV_V:o + EV_V + EV_P]


def _inproj_even(x, mod, g, w, tables):
    B = x.shape[0]
    cq, sq, ck, sk = tables
    tok = lambda n: pl.BlockSpec((1, TM, n), lambda b, t: (b, t, 0))
    tab = lambda n: pl.BlockSpec((TM, n), lambda b, t: (t, 0))
    return pl.pallas_call(
        _inproj_even_kernel,
        grid=(B, N_TILES),
        in_specs=[tok(D), _mod_spec(), pl.BlockSpec((1, D), lambda b, t: (0, 0)),
                  pl.BlockSpec((D, EV_N), lambda b, t: (0, 0)),
                  tab(EV_Q), tab(EV_Q), tab(EV_K), tab(EV_K)],
        out_specs=[tok(EV_Q), tok(EV_K), tok(EV_V), tok(EV_P)],
        out_shape=[jax.ShapeDtypeStruct((B, NT, EV_Q), BF16),
                   jax.ShapeDtypeStruct((B, NT, EV_K), BF16),
                   jax.ShapeDtypeStruct((B, NT, EV_V), BF16),
                   jax.ShapeDtypeStruct((B, NT, EV_P), F32)],
        compiler_params=_cp("parallel", "parallel"),
        name="inproj_even",
    )(x, mod, g.reshape(1, D), w, cq, sq, ck, sk)


def _attn_even_kernel(sink_ref, q_ref, kp_ref, kc_ref, kn_ref, kx_ref,
                      vp_ref, vc_ref, vn_ref, vx_ref, o_ref):
    j = pl.program_id(1)
    is_lat = j < N_LAT_ABLK
    q = q_ref[0]
    qpos = j * A_BLOCK + lax.broadcasted_iota(I32, (A_BLOCK, 3 * A_BLOCK), 0)
    kpos = (j - 1) * A_BLOCK + lax.broadcasted_iota(I32, (A_BLOCK, 3 * A_BLOCK), 1)
    valid = (jnp.abs(kpos - qpos) <= A_WINDOW) & (kpos >= 0) & (kpos < T_L) & is_lat
    lo_lanes = lax.broadcasted_iota(I32, (A_BLOCK, 128), 1) < HEAD_DIM
    kall, vall = [], []
    for kv in range(A_KV_HEADS):
        ks = slice(kv * 128, (kv + 1) * 128)
        kall.append(jnp.concatenate([kx_ref[0, :, ks], kp_ref[0, :, ks], kc_ref[0, :, ks], kn_ref[0, :, ks]], axis=0))
        vall.append(jnp.concatenate([vx_ref[0, :, ks], vp_ref[0, :, ks], vc_ref[0, :, ks], vn_ref[0, :, ks]], axis=0))
    heads = range(A_HEADS)
    group = A_HEADS // A_KV_HEADS
    qm = [jnp.where(lo_lanes if h % 2 == 0 else jnp.logical_not(lo_lanes), q[:, (h // 2) * 128:(h // 2 + 1) * 128],
                    jnp.zeros((A_BLOCK, 128), q.dtype)) for h in heads]
    s = [_dot_nt(qm[h], kall[h // group]) for h in heads]
    e, inv_den = [], []
    for h in heads:
        snk = sink_ref[h]
        s_ctx = s[h][:, :T_C]
        s_loc = jnp.where(valid, s[h][:, T_C:], -jnp.inf)
        m = jnp.maximum(jnp.max(s_ctx, axis=-1, keepdims=True), jnp.max(s_loc, axis=-1, keepdims=True))
        m = jnp.maximum(m, snk)
        e_ctx = jnp.exp(s_ctx - m)
        e_loc = jnp.exp(s_loc - m)
        den = (jnp.sum(e_ctx, axis=-1, keepdims=True) + jnp.sum(e_loc, axis=-1, keepdims=True)
               + jnp.exp(snk - m))
        e.append(jnp.concatenate([e_ctx, e_loc], axis=1).astype(BF16))
        inv_den.append(1.0 / den)
    o = [_dot(e[h], vall[h // group]) * inv_den[h] for h in heads]
    for p in range(A_HEADS // 2):
        o_ref[0, :, p * 128:(p + 1) * 128] = jnp.where(lo_lanes, o[2 * p], o[2 * p + 1]).astype(o_ref.dtype)


def _attn_even(q, kd, vd, sink):
    B = q.shape[0]
    last = N_LAT_ABLK - 1
    blk = lambda f: pl.BlockSpec((1, A_BLOCK, 256), f)
    prev = lambda b, j: (b, jnp.clip(j - 1, 0, last), 0)
    cur = lambda b, j: (b, j, 0)
    nxt = lambda b, j: (b, jnp.clip(j + 1, 0, last), 0)
    ctx = pl.BlockSpec((1, T_C, 256), lambda b, j: (b, T_L // T_C, 0))
    return pl.pallas_call(
        _attn_even_kernel,
        grid=(B, N_ABLK),
        in_specs=[pl.BlockSpec(memory_space=pltpu.SMEM),
                  pl.BlockSpec((1, A_BLOCK, EV_Q), cur),
                  blk(prev), blk(cur), blk(nxt), ctx,
                  blk(prev), blk(cur), blk(nxt), ctx],
        out_specs=pl.BlockSpec((1, A_BLOCK, 512), cur),
        out_shape=jax.ShapeDtypeStruct((B, NT, 512), BF16),
        compiler_params=_cp("parallel", "parallel"),
        name="attn_even",
    )(sink, q, kd, kd, kd, kd, vd, vd, vd, vd)


def _pool_kernel(up_ref, uc_ref, un_ref, w_ref, sc_ref, o_ref):
    j = pl.program_id(1)
    is_lat = j < N_LAT_ABLK
    seg_lo = jnp.where(is_lat, 0, T_L)
    seg_hi = jnp.where(is_lat, T_L, NT)
    u3 = jnp.concatenate([up_ref[0], uc_ref[0], un_ref[0]], axis=0)
    uc = uc_ref[0]
    r = j * A_BLOCK + lax.broadcasted_iota(I32, (A_BLOCK, 3 * A_BLOCK), 0)
    c = (j - 1) * A_BLOCK + lax.broadcasted_iota(I32, (A_BLOCK, 3 * A_BLOCK), 1)
    r1 = j * A_BLOCK + lax.broadcasted_iota(I32, (A_BLOCK, 1), 0)
    hi_u, lo_u = _split_bf16(u3)
    groups = range(len(POOL_WINDOWS))
    cols = [slice(gi * POOL_GROUP, (gi + 1) * POOL_GROUP) for gi in groups]
    bands, inv_cnt = [], []
    for w in POOL_WINDOWS:
        lo = jnp.maximum(r - w // 2, seg_lo)
        hi = jnp.minimum(r + w // 2, seg_hi)
        bands.append(((c >= lo) & (c < hi)).astype(BF16))
        cnt = (jnp.minimum(r1 + w // 2, seg_hi) - jnp.maximum(r1 - w // 2, seg_lo)).astype(F32)
        inv_cnt.append(1.0 / cnt)
    sums = [_dot(bands[gi], jnp.concatenate([hi_u[:, cols[gi]], lo_u[:, cols[gi]]], axis=1)) for gi in groups]
    dev = [((s[:, :POOL_GROUP] + s[:, POOL_GROUP:]) * inv_cnt[gi] - uc[:, cols[gi]]).astype(BF16)
           for gi, s in enumerate(sums)]
    outs = [_dot(dev[gi], w_ref[gi]) for gi in groups]
    o_ref[0] = (jnp.concatenate(outs, axis=1) * sc_ref[...]).astype(o_ref.dtype)


def _pool(u, pool_w, pool_scale):
    B = u.shape[0]
    blk = lambda f: pl.BlockSpec((1, A_BLOCK, 512), f)
    return pl.pallas_call(
        _pool_kernel,
        grid=(B, N_ABLK),
        in_specs=[blk(lambda b, j: (b, jnp.maximum(j - 1, 0), 0)),
                  blk(lambda b, j: (b, j, 0)),
                  blk(lambda b, j: (b, jnp.minimum(j + 1, N_ABLK - 1), 0)),
                  pl.BlockSpec((4, POOL_GROUP, POOL_GROUP), lambda b, j: (0, 0, 0)),
                  pl.BlockSpec((1, 512), lambda b, j: (0, 0))],
        out_specs=blk(lambda b, j: (b, j, 0)),
        out_shape=jax.ShapeDtypeStruct((B, NT, 512), BF16),
        compiler_params=_cp("parallel", "parallel"),
        name="pool",
    )(u, u, u, pool_w.astype(BF16), pool_scale.reshape(1, 512))


def _outproj_kernel(a_ref, p_ref, w_ref, x_ref, mod_ref, g_ref, wr_ref, xo_ref, h_ref, lg_ref):
    mod = mod_ref[0]
    y = _dot(a_ref[0], w_ref[0:512, :]) + _dot(p_ref[0], w_ref[512:1024, :])
    x = x_ref[0] + mod[2:3] * y
    xo_ref[0] = x
    h = _modulate(x, g_ref[...], mod[3:4], mod[4:5])
    w_hi, w_lo = _split_bf16(wr_ref[...])
    h_hi, h_lo = _split_bf16(h)
    h_ref[0] = h_hi
    lg_ref[0] = _dot_nt(w_hi, h_hi) + _dot_nt(w_lo, h_hi) + _dot_nt(w_hi, h_lo)


def _outproj(a, p, w_out, x, mod, g_ffn, w_router, need_ctx):
    B = x.shape[0]
    tok = lambda n: pl.BlockSpec((1, TM, n), lambda b, t: (b, t, 0))
    return pl.pallas_call(
        _outproj_kernel,
        grid=(B, N_TILES if need_ctx else N_LAT_TILES),
        in_specs=[tok(512), tok(512), pl.BlockSpec((D, D), lambda b, t: (0, 0)), tok(D), _mod_spec(),
                  pl.BlockSpec((1, D), lambda b, t: (0, 0)),
                  pl.BlockSpec((N_EXPERTS, D), lambda b, t: (0, 0))],
        out_specs=[tok(D), tok(D), pl.BlockSpec((1, N_EXPERTS, TM), lambda b, t: (b, 0, t))],
        out_shape=[jax.ShapeDtypeStruct((B, NT, D), F32),
                   jax.ShapeDtypeStruct((B, NT, D), BF16),
                   jax.ShapeDtypeStruct((B, N_EXPERTS, NT), F32)],
        compiler_params=_cp("parallel", "parallel"),
        name="outproj",
    )(a, p, w_out.astype(BF16), x, mod, g_ffn.reshape(1, D), w_router.T)


def _topk_kernel(lg_ref, sid_ref, gate_ref, *, cap):
    B, E, T = lg_ref.shape
    R = B * E
    lg = lg_ref[...]
    m = jnp.max(lg, axis=1, keepdims=True)
    e = jnp.exp(lg - m)
    aff = (e / jnp.sum(e, axis=1, keepdims=True)).reshape(R, T)
    bits = lax.bitcast_convert_type(aff, I32)

    def step(i, thr):
        cand = thr | jnp.left_shift(jnp.int32(1), 30 - i)
        cnt = jnp.sum((bits >= cand).astype(F32), axis=1, keepdims=True)
        return jnp.where(cnt >= cap, cand, thr)

    thr = lax.fori_loop(0, 31, step, jnp.zeros((R, 1), I32))
    gt = bits > thr
    eq = bits == thr
    need = cap - jnp.sum(gt.astype(F32), axis=1, keepdims=True)

    blk = 256
    tri = (lax.broadcasted_iota(I32, (blk, blk), 0) <= lax.broadcasted_iota(I32, (blk, blk), 1)).astype(BF16)

    def prefix(mask_f32):
        off = jnp.zeros((R, 1), F32)
        parts = []
        for k in range(T // blk):
            cb = _dot(mask_f32[:, k * blk:(k + 1) * blk].astype(BF16), tri) + off
            parts.append(cb)
            off = cb[:, blk - 1:blk]
        return parts

    eq_parts = prefix(eq.astype(F32))
    sel_parts = []
    for k in range(T // blk):
        sl = slice(k * blk, (k + 1) * blk)
        sel_parts.append(gt[:, sl] | (eq[:, sl] & (eq_parts[k] <= need)))
    sel = jnp.concatenate(sel_parts, axis=1)
    pos_parts = prefix(sel.astype(F32))
    for k in range(T // blk):
        sl = slice(k * blk, (k + 1) * blk)
        sid = jnp.where(sel_parts[k], pos_parts[k].astype(I32) - 1, -1)
        sid_ref[:, :, sl] = sid.reshape(B, E, blk)
        gate_ref[:, :, sl] = jnp.where(sel_parts[k], aff[:, sl], 0.0).reshape(B, E, blk)


def _topk(logits, cap):
    B, E, T = logits.shape
    full = pl.BlockSpec((B, E, T), lambda i: (0, 0, 0))
    return pl.pallas_call(
        functools.partial(_topk_kernel, cap=cap),
        grid=(1,),
        in_specs=[full],
        out_specs=[full, full],
        out_shape=[jax.ShapeDtypeStruct((B, E, T), I32), jax.ShapeDtypeStruct((B, E, T), F32)],
        compiler_params=_cp("arbitrary"),
        name=f"topk_{T}",
    )(logits)


GATHER_COLS = 512


def _gather_kernel(h_ref, sid_ref, gate_ref, xs_ref, gs_ref, *, cap, chunk):
    T = h_ref.shape[1]
    E = sid_ref.shape[1]
    acc = jnp.zeros((E * cap, GATHER_COLS), F32)
    slot = lax.broadcasted_iota(I32, (cap, chunk), 0)
    for k in range(T // chunk):
        sl = slice(k * chunk, (k + 1) * chunk)
        hit = jnp.concatenate([(slot == sid_ref[0, e, :, sl]).astype(BF16) for e in range(E)], axis=0)
        acc = acc + _dot(hit, h_ref[0, sl, :])
    for e in range(E):
        xs_ref[e] = acc[e * cap:(e + 1) * cap].astype(BF16)

    @pl.when(pl.program_id(1) == 0)
    def _():
        slot_t = lax.broadcasted_iota(I32, (cap, T), 0)
        for e in range(E):
            picked = jnp.where(slot_t == sid_ref[0, e], gate_ref[0, e], 0.0)
            gs_ref[e] = jnp.broadcast_to(jnp.sum(picked, axis=1, keepdims=True), (cap, 128))


def _gather(h, sid, gate, cap, row_block):
    B, E, T = sid.shape
    chunk = min(T, 512)
    idx = lambda b, j: (b, 0, 0, 0)
    return pl.pallas_call(
        functools.partial(_gather_kernel, cap=cap, chunk=chunk),
        grid=(B, D // GATHER_COLS),
        in_specs=[pl.BlockSpec((1, T, GATHER_COLS), lambda b, j: (b, row_block, j)),
                  pl.BlockSpec((1, E, 1, T), idx), pl.BlockSpec((1, E, 1, T), idx)],
        out_specs=[pl.BlockSpec((E, cap, GATHER_COLS), lambda b, j: (0, b, j)),
                   pl.BlockSpec((E, cap, 128), lambda b, j: (0, b, 0))],
        out_shape=[jax.ShapeDtypeStruct((E, B * cap, D), BF16),
                   jax.ShapeDtypeStruct((E, B * cap, 128), F32)],
        compiler_params=_cp("parallel", "arbitrary"),
        name=f"gather_{T}",
    )(h, sid.reshape(B, E, 1, T), gate.reshape(B, E, 1, T))


DOWN_CHUNK = 256


def _ffn_kernel(*refs, row_tiles):
    n_seg = len(row_tiles)
    xs_refs = refs[:n_seg]
    gs_refs = refs[n_seg:2 * n_seg]
    wg_ref, wu_ref, wd_ref = refs[2 * n_seg:2 * n_seg + 3]
    y_refs = refs[2 * n_seg + 3:3 * n_seg + 3]
    hid_refs = refs[3 * n_seg + 3:]
    s = pl.program_id(1)
    nf = EXPERT_FF // FF_CHUNK

    @pl.when(s < nf)
    def _():
        wg = wg_ref[0].astype(BF16)
        wu = wu_ref[0].astype(BF16)
        for xs_ref, hid_ref, rt in zip(xs_refs, hid_refs, row_tiles):
            for r0 in range(0, xs_ref.shape[1], rt):
                rows = pl.ds(r0, rt)
                x = xs_ref[0, rows, :]
                g = _dot(x, wg)
                u = _dot(x, wu)
                hid_ref[s, rows, :] = (g * jax.nn.sigmoid(g) * u).astype(BF16)

    @pl.when(s >= nf)
    def _():
        d = s - nf
        wd = wd_ref[0].astype(BF16)
        for gs_ref, y_ref, hid_ref, rt in zip(gs_refs, y_refs, hid_refs, row_tiles):
            for r0 in range(0, hid_ref.shape[1], rt):
                rows = pl.ds(r0, rt)
                acc = _dot(hid_ref[0, rows, :], wd[0:FF_CHUNK, :])
                for f in range(1, nf):
                    acc = acc + _dot(hid_ref[f, rows, :], wd[f * FF_CHUNK:(f + 1) * FF_CHUNK, :])
                gs = gs_ref[0, rows, :]
                y_ref[0, d, rows, :] = jnp.concatenate(
                    [acc[:, k * 128:(k + 1) * 128] * gs for k in range(DOWN_CHUNK // 128)], axis=1).astype(y_ref.dtype)


def _ffn(xs_list, gs_list, w_gate, w_up, w_down, layer):
    E = w_gate.shape[1]
    row_tiles = tuple(min(xs.shape[1], 512) for xs in xs_list)
    nf = EXPERT_FF // FF_CHUNK
    nd = D // DOWN_CHUNK
    in_specs = [pl.BlockSpec((1, xs.shape[1], D), lambda e, s: (e, 0, 0)) for xs in xs_list]
    in_specs += [pl.BlockSpec((1, gs.shape[1], 128), lambda e, s: (e, 0, 0)) for gs in gs_list]
    up_map = lambda e, s: (layer, e, 0, jnp.minimum(s, nf - 1))
    in_specs += [pl.BlockSpec((None, 1, D, FF_CHUNK), up_map),
                 pl.BlockSpec((None, 1, D, FF_CHUNK), up_map),
                 pl.BlockSpec((None, 1, EXPERT_FF, DOWN_CHUNK), lambda e, s: (layer, e, 0, jnp.maximum(s - nf, 0)))]
    return pl.pallas_call(
        functools.partial(_ffn_kernel, row_tiles=row_tiles),
        grid=(E, nf + nd),
        in_specs=in_specs,
        out_specs=[pl.BlockSpec((1, nd, xs.shape[1], DOWN_CHUNK), lambda e, s: (e, 0, 0, 0)) for xs in xs_list],
        out_shape=[jax.ShapeDtypeStruct((E, nd, xs.shape[1], DOWN_CHUNK), BF16) for xs in xs_list],
        scratch_shapes=[pltpu.VMEM((nf, xs.shape[1], FF_CHUNK), BF16) for xs in xs_list],
        compiler_params=_cp("parallel", "arbitrary"),
        name="expert_ffn",
    )(*xs_list, *gs_list, w_gate, w_up, w_down)


def _combine_kernel(x_ref, sid_ref, y_ref, mod_ref, *rest, cap):
    o_ref = rest[-1]
    tbs = x_ref.shape[1]
    E = sid_ref.shape[1]
    slot = lax.broadcasted_iota(I32, (cap, tbs), 0)
    hit = jnp.concatenate([(slot == sid_ref[0, e]).astype(BF16) for e in range(E)], axis=0)
    gate = mod_ref[0][5:6]
    for d in range(D // DOWN_CHUNK):
        cs = slice(d * DOWN_CHUNK, (d + 1) * DOWN_CHUNK)
        z = _dot_tn(hit, y_ref[:, d].reshape(E * cap, DOWN_CHUNK))
        o_ref[0, :, cs] = x_ref[0, :, cs] + gate[:, cs] * z
    if len(rest) == 2:
        o_ref[0] = _rmsnorm(o_ref[0], rest[0][...])


def _combine(x, sid, y, mod, cap, tbs, row_block0, segment, final_g=None):
    B, E, T = sid.shape
    ntb = T // tbs
    xmap = lambda b, tb: (b, row_block0 + tb, 0)
    in_specs = [pl.BlockSpec((1, tbs, D), xmap),
                pl.BlockSpec((1, E, 1, tbs), lambda b, tb: (b, 0, 0, tb)),
                pl.BlockSpec((E, D // DOWN_CHUNK, cap, DOWN_CHUNK), lambda b, tb: (0, 0, b, 0)),
                pl.BlockSpec((1, 8, D), lambda b, tb: (b * 2 + segment, 0, 0))]
    args = [x, sid.reshape(B, E, 1, T), y, mod]
    if final_g is None:
        out_spec, out_shape, alias = pl.BlockSpec((1, tbs, D), xmap), x.shape, {0: 0}
    else:
        assert row_block0 == 0
        in_specs.append(pl.BlockSpec((1, D), lambda b, tb: (0, 0)))
        args.append(final_g.reshape(1, D))
        out_spec, out_shape, alias = pl.BlockSpec((1, tbs, D), lambda b, tb: (b, tb, 0)), (B, T, D), {}
    return pl.pallas_call(
        functools.partial(_combine_kernel, cap=cap),
        grid=(B, ntb),
        in_specs=in_specs,
        out_specs=out_spec,
        out_shape=jax.ShapeDtypeStruct(out_shape, F32),
        input_output_aliases=alias,
        compiler_params=_cp("parallel", "parallel"),
        name=f"combine_{T}" + ("" if final_g is None else "_final"),
    )(*args)


def _moe(x, h, logits, mod, w_gate, w_up, w_down, layer, need_ctx, final_g):
    sid_l, gate_l = _topk(logits[:, :, :T_L], CAP_L)
    xs_l, gs_l = _gather(h, sid_l, gate_l, CAP_L, 0)
    if need_ctx:
        sid_c, gate_c = _topk(logits[:, :, T_L:], CAP_C)
        xs_c, gs_c = _gather(h, sid_c, gate_c, CAP_C, T_L // T_C)
        y_l, y_c = _ffn([xs_l, xs_c], [gs_l, gs_c], w_gate, w_up, w_down, layer)
    else:
        (y_l,) = _ffn([xs_l], [gs_l], w_gate, w_up, w_down, layer)
    x = _combine(x, sid_l, y_l, mod, CAP_L, 512, 0, 0, final_g)
    if need_ctx:
        x = _combine(x, sid_c, y_c, mod, CAP_C, T_C, T_L // T_C, 1)
    return x


OD_N = RWKV_IN + MLA_U


def _odd_weights(w_in, partner):
    rw = w_in[:, :RWKV_IN]
    qd = w_in[:, RWKV_IN:RWKV_IN + MLA_Q_RANK]
    kvd = w_in[:, RWKV_IN + MLA_Q_RANK:RWKV_IN + MLA_Q_RANK + MLA_KV_RANK]
    kr = w_in[:, RWKV_IN + MLA_Q_RANK + MLA_KV_RANK:]
    pad = jnp.zeros((D, 128 - MLA_ROPE), w_in.dtype)
    w = jnp.concatenate([rw, qd, kvd, kr, pad, kr[:, partner], pad], axis=1)
    return w.astype(BF16)


def _inproj_odd_kernel(x_ref, mod_ref, g_ref, w_ref, ur_ref, um_ref):
    mod = mod_ref[0]
    h = _modulate(x_ref[0], g_ref[...], mod[0:1], mod[1:2])
    y = _dot(h.astype(BF16), w_ref[...])
    ur_ref[0] = y[:, :RWKV_IN]
    um_ref[0] = y[:, RWKV_IN:]


def _inproj_odd(x, mod, g, w):
    B = x.shape[0]
    tok = lambda n: pl.BlockSpec((1, TM, n), lambda b, t: (b, t, 0))
    return pl.pallas_call(
        _inproj_odd_kernel,
        grid=(B, N_TILES),
        in_specs=[tok(D), _mod_spec(), pl.BlockSpec((1, D), lambda b, t: (0, 0)),
                  pl.BlockSpec((D, OD_N), lambda b, t: (0, 0))],
        out_specs=[tok(RWKV_IN), tok(MLA_U)],
        out_shape=[jax.ShapeDtypeStruct((B, NT, RWKV_IN), F32),
                   jax.ShapeDtypeStruct((B, NT, MLA_U), F32)],
        compiler_params=_cp("parallel", "parallel"),
        name="inproj_odd",
    )(x, mod, g.reshape(1, D), w)


def _head_ones():
    h = np.arange(RWKV_WIDTH) // RWKV_HEAD
    return jnp.asarray((h[:, None] == h[None, :]).astype(np.float32)).astype(BF16)


def _headsum(x, bd):
    return _dot2(x, bd)


def _rwkv_prep_kernel(u_ref, up_ref, un_ref, mup_ref, mun_ref, w0_ref, w2_ref, a0_ref, a2_ref, g2_ref,
                      kk_ref, ka_ref, bd_ref,
                      r_ref, v_ref, kkn_ref, g_ref, lw_ref, bq_ref, kd_ref):
    t = pl.program_id(1)
    u = u_ref[0]
    rows = lax.broadcasted_iota(I32, (TM, 1), 0)
    is_ctx = t == N_LAT_TILES
    seg_start = (t == 0) | is_ctx
    seg_end = (t == N_LAT_TILES - 1) | is_ctx
    last_row = jnp.where(is_ctx, T_C - 1, TM - 1)
    prev_row = jnp.where(seg_start, 0.0, up_ref[0, 7:8, :])
    next_row = jnp.where(seg_end, 0.0, un_ref[0, 0:1, :])
    prev = jnp.where(rows == 0, prev_row, pltpu.roll(u, 1, 0))
    nxt = jnp.where(rows == last_row, next_row, pltpu.roll(u, TM - 1, 0))
    us = u + mup_ref[...] * (prev - u) + mun_ref[...] * (nxt - u)
    W = RWKV_WIDTH
    r = us[:, :W]
    k = us[:, W:2 * W]
    v = us[:, 2 * W:3 * W]
    wl = jnp.tanh(us[:, 3 * W:3 * W + 128]).astype(BF16)
    al = us[:, 3 * W + 128:3 * W + 256].astype(BF16)
    gl = jax.nn.sigmoid(us[:, 3 * W + 256:3 * W + 384]).astype(BF16)
    bd = bd_ref[...]
    kk = k * kk_ref[...]
    kk = kk * lax.rsqrt(jnp.maximum(_headsum(kk * kk, bd), 1e-24))
    r_ref[0] = r.astype(BF16)
    v_ref[0] = v.astype(BF16)
    kkn_ref[0] = kk.astype(BF16)
    g_ref[0] = _dot(gl, g2_ref[...]).astype(BF16)
    for d in range(2):
        z = w0_ref[d:d + 1, :] + _dot(wl, w2_ref[d])
        a = jax.nn.sigmoid(a0_ref[d:d + 1, :] + _dot(al, a2_ref[d]))
        lw_ref[d, 0] = jax.nn.sigmoid(z) * (-float(np.exp(-0.5)))
        bq_ref[d, 0] = (kk * a).astype(BF16)
        kd_ref[d, 0] = (k * (1.0 + (a - 1.0) * ka_ref[...])).astype(BF16)


def _rwkv_prep(ur, mu_prev, mu_next, w0, w2, a0, a2, g2, k_k, k_a, bd):
    B = ur.shape[0]
    W = RWKV_WIDTH
    z = jnp.zeros((64, W), F32)
    w2p = jnp.stack([jnp.concatenate([w2[0], z]), jnp.concatenate([z, w2[1]])]).astype(BF16)
    a2p = jnp.stack([jnp.concatenate([a2[0], z]), jnp.concatenate([z, a2[1]])]).astype(BF16)
    tok = lambda n: pl.BlockSpec((1, TM, n), lambda b, t: (b, t, 0))
    tok2 = pl.BlockSpec((2, 1, TM, W), lambda b, t: (0, b, t, 0))
    row = lambda n: pl.BlockSpec((1, n), lambda b, t: (0, 0))
    full = lambda *s: pl.BlockSpec(s, lambda b, t: (0,) * len(s))
    slab = TM // 8
    one = lambda: jax.ShapeDtypeStruct((B, NT, W), BF16)
    per_dir = lambda dt: jax.ShapeDtypeStruct((2, B, NT, W), dt)
    return pl.pallas_call(
        _rwkv_prep_kernel,
        grid=(B, N_TILES),
        in_specs=[tok(RWKV_IN),
                  pl.BlockSpec((1, 8, RWKV_IN), lambda b, t: (b, jnp.maximum(t * slab - 1, 0), 0)),
                  pl.BlockSpec((1, 8, RWKV_IN), lambda b, t: (b, jnp.minimum((t + 1) * slab, NT // 8 - 1), 0)),
                  row(RWKV_IN), row(RWKV_IN), full(2, W), full(2, 128, W), full(2, W), full(2, 128, W),
                  full(128, W), row(W), row(W), full(W, W)],
        out_specs=[tok(W), tok(W), tok(W), tok(W), tok2, tok2, tok2],
        out_shape=[one(), one(), one(), one(), per_dir(F32), per_dir(BF16), per_dir(BF16)],
        compiler_params=_cp("parallel", "parallel"),
        name="rwkv_prep",
    )(ur, ur, ur, mu_prev.reshape(1, RWKV_IN), mu_next.reshape(1, RWKV_IN), w0, w2p, a0, a2p,
      g2.astype(BF16), k_k.reshape(1, W), k_a.reshape(1, W), bd)


def _wkv_kernel(r_ref, v_ref, kk_ref, lw_ref, bq_ref, kd_ref, o_ref, s_ref, *, nb):
    d = pl.program_id(0)
    i = pl.program_id(2)
    C = WKV_CHUNK
    P = RWKV_HEADS // 2
    G = nb * P
    HD = RWKV_HEAD

    @pl.when(i == 0)
    def _():
        s_ref[...] = jnp.zeros_like(s_ref)

    sgn = 1 - 2 * d
    rel = (lax.broadcasted_iota(I32, (C, C), 0) - lax.broadcasted_iota(I32, (C, C), 1)) * sgn
    incl_b = (rel >= 0).astype(BF16)
    row_w = lax.broadcasted_iota(I32, (C, 2 * HD), 0)
    col_w = lax.broadcasted_iota(I32, (C, 2 * HD), 1) & (HD - 1)
    rel_w = (row_w - col_w) * sgn
    incl = rel_w >= 0
    strict = rel_w > 0
    eye = (rel_w == 0).astype(F32)
    lo = lax.broadcasted_iota(I32, (C, 2 * HD), 1) < HD
    same_head = ((lax.broadcasted_iota(I32, (2 * HD, 2 * HD), 0) < HD)
                 == (lax.broadcasted_iota(I32, (2 * HD, 2 * HD), 1) < HD))

    def same_block(k):
        return (row_w >> k) == (col_w >> k)

    def pair(y):
        zero = jnp.zeros_like(y)
        return jnp.concatenate([jnp.where(lo, y, zero), jnp.where(lo, zero, y)], axis=0)

    at, rt, bt, kt, a0, r0, bh, kh, v, p_end = ([] for _ in range(10))
    for b in range(nb):
        lw = lw_ref[0, b]
        lw_hi, lw_lo = _split_bf16(lw)
        ci = _dot(incl_b, lw_hi) + _dot(incl_b, lw_lo)
        ce = ci - lw
        tot = jnp.sum(lw, axis=0, keepdims=True)
        m = 0.5 * tot
        r = r_ref[b].astype(F32)
        kk = kk_ref[b].astype(F32)
        bq = bq_ref[0, b].astype(F32)
        kd = kd_ref[0, b].astype(F32)
        e_ci = jnp.exp(ci - m)
        e_mci = jnp.exp(m - ci)
        e_end = jnp.exp(tot - ci)
        full = ((at, (-kk * jnp.exp(ce - m)).astype(BF16)), (rt, (r * e_ci).astype(BF16)),
                (bt, (bq * e_mci).astype(BF16)), (kt, (kd * e_mci).astype(BF16)),
                (a0, (-kk * jnp.exp(ce)).astype(BF16)), (r0, (r * (e_ci * jnp.exp(m))).astype(BF16)),
                (bh, (bq * e_end).astype(BF16)), (kh, (kd * e_end).astype(BF16)),
                (v, v_ref[b]), (p_end, jnp.exp(tot)))
        for dst, x in full:
            dst.extend(x[:, p * 2 * HD:(p + 1) * 2 * HD] for p in range(P))

    S = [s_ref[g] for g in range(G)]
    s_b = [s.astype(BF16) for s in S]
    left = [jnp.concatenate([at[g], rt[g]], axis=0) for g in range(G)]
    nb_ = [_dot_nt(left[g], pair(bt[g])) for g in range(G)]
    nk_ = [_dot_nt(left[g], pair(kt[g])) for g in range(G)]
    n_ab = [jnp.where(strict, x[:C], 0.0) for x in nb_]
    a_rb = [jnp.where(incl, x[C:], 0.0).astype(BF16) for x in nb_]
    mask_k = jnp.concatenate([strict, incl], axis=0)
    a_k = [jnp.where(mask_k, x, 0.0).astype(BF16) for x in nk_]
    from_state = [_dot_nt(jnp.concatenate([a0[g], r0[g]], axis=0), s_b[g]) for g in range(G)]
    from_v = [_dot(a_k[g], pair(v[g])) for g in range(G)]
    w_rhs = [from_state[g][:C] + from_v[g][:C] for g in range(G)]
    inv = [eye + jnp.where(same_block(1), n, 0.0) for n in n_ab]
    for k in range(1, 6):
        off_mask = same_block(k + 1) & jnp.logical_not(same_block(k))
        inv_b = [x.astype(BF16) for x in inv]
        e = [_dot(jnp.where(off_mask, n, 0.0).astype(BF16), pair(x)) for n, x in zip(n_ab, inv_b)]
        inv = [x + _dot(xb, pair(y.astype(BF16))) for x, xb, y in zip(inv, inv_b, e)]
    U = [_dot(inv[g].astype(BF16), pair(w_rhs[g].astype(BF16))).astype(BF16) for g in range(G)]
    for g in range(G):
        o = from_state[g][C:] + _dot(a_rb[g], pair(U[g])) + from_v[g][C:]
        o_ref[0, g // P, :, (g % P) * 2 * HD:(g % P + 1) * 2 * HD] = o
    for g in range(G):
        upd = _dot_tn(jnp.concatenate([U[g], v[g]], axis=0), jnp.concatenate([bh[g], kh[g]], axis=0))
        s_ref[g] = S[g] * p_end[g] + jnp.where(same_head, upd, 0.0)


WKV_BATCH = 4


def _wkv(r, v, kk, lw, bq, kd, nb=WKV_BATCH):
    B = r.shape[0]
    C = WKV_CHUNK
    n_lat = T_L // C

    def chunk(d, i):
        fwd = jnp.where(i < N_CTX_CHUNKS, n_lat + i, i - N_CTX_CHUNKS)
        return jnp.where(d == 0, fwd, N_CHUNKS - 1 - i)

    shared = pl.BlockSpec((nb, C, RWKV_WIDTH), lambda d, b, i: (b, chunk(d, i), 0))
    perdir = pl.BlockSpec((1, nb, C, RWKV_WIDTH), lambda d, b, i: (d, b, chunk(d, i), 0))
    return pl.pallas_call(
        functools.partial(_wkv_kernel, nb=nb),
        grid=(2, B // nb, N_CHUNKS),
        in_specs=[shared, shared, shared, perdir, perdir, perdir],
        out_specs=perdir,
        out_shape=jax.ShapeDtypeStruct((2, B, NT, RWKV_WIDTH), F32),
        scratch_shapes=[pltpu.VMEM((nb * RWKV_HEADS // 2, 2 * RWKV_HEAD, 2 * RWKV_HEAD), F32)],
        compiler_params=_cp("parallel", "parallel", "arbitrary"),
        name="wkv7",
    )(r, v, kk, lw, bq, kd)


def _rwkv_finish_kernel(o_ref, r_ref, v_ref, g_ref, kd_ref, rk_ref, lg_ref, lb_ref, bd_ref, y_ref):
    bd = bd_ref[...]
    o = o_ref[0, 0] + o_ref[1, 0]
    mu = _headsum(o, bd) * (1.0 / RWKV_HEAD)
    xc = o - mu
    var = _headsum(xc * xc, bd) * (1.0 / RWKV_HEAD)
    on = xc * lax.rsqrt(var + LNX_EPS) * lg_ref[...] + lb_ref[...]
    k_sum = kd_ref[0, 0].astype(F32) + kd_ref[1, 0].astype(F32)
    bonus = _headsum(r_ref[0].astype(F32) * k_sum * rk_ref[...], bd) * v_ref[0].astype(F32)
    y_ref[0] = ((on + bonus) * g_ref[0].astype(F32)).astype(y_ref.dtype)


def _rwkv_finish(o, r, v, g, kd, r_k, lnx_g, lnx_b, bd, need_ctx):
    B = r.shape[0]
    W = RWKV_WIDTH
    tok = pl.BlockSpec((1, TM, W), lambda b, t: (b, t, 0))
    tok2 = pl.BlockSpec((2, 1, TM, W), lambda b, t: (0, b, t, 0))
    row = pl.BlockSpec((1, W), lambda b, t: (0, 0))
    return pl.pallas_call(
        _rwkv_finish_kernel,
        grid=(B, N_TILES if need_ctx else N_LAT_TILES),
        in_specs=[tok2, tok, tok, tok, tok2, row, row, row, pl.BlockSpec((W, W), lambda b, t: (0, 0))],
        out_specs=tok,
        out_shape=jax.ShapeDtypeStruct((B, NT, W), BF16),
        compiler_params=_cp("parallel", "parallel"),
        name="rwkv_finish",
    )(o, r, v, g, kd, r_k.reshape(1, W), lnx_g.reshape(1, W), lnx_b.reshape(1, W), bd)


MLA_QW = MLA_HEADS * MLA_SLOT


def _mla_weights(w_uq, w_ukv, partner):
    dq = MLA_NOPE + MLA_ROPE
    wq = jnp.zeros((MLA_Q_RANK, 2 * MLA_QW), F32)
    wk = jnp.zeros((MLA_KV_RANK, MLA_QW), F32)
    wv = jnp.zeros((MLA_KV_RANK, MLA_HEADS * MLA_V), F32)
    place = np.zeros((128, MLA_QW), np.float32)
    for h in range(MLA_HEADS):
        qh = w_uq[:, h * dq:(h + 1) * dq]
        wq = wq.at[:, h * MLA_SLOT:h * MLA_SLOT + dq].set(qh)
        wq = wq.at[:, MLA_QW + h * MLA_SLOT + MLA_NOPE:MLA_QW + h * MLA_SLOT + dq].set(qh[:, MLA_NOPE + partner])
        kvh = w_ukv[:, h * (MLA_NOPE + MLA_V):(h + 1) * (MLA_NOPE + MLA_V)]
        wk = wk.at[:, h * MLA_SLOT:h * MLA_SLOT + MLA_NOPE].set(kvh[:, :MLA_NOPE])
        wv = wv.at[:, h * MLA_V:(h + 1) * MLA_V].set(kvh[:, MLA_NOPE:])
        place[np.arange(MLA_ROPE), h * MLA_SLOT + MLA_NOPE + np.arange(MLA_ROPE)] = 1.0
    return wq.astype(BF16), wk.astype(BF16), wv.astype(BF16), jnp.asarray(place).astype(BF16)


def _mla_proj_kernel(u_ref, qg_ref, kg_ref, wq_ref, wk_ref, wv_ref, pl_ref, cq_ref, sq_ref, ck_ref, sk_ref,
                     q_ref, k_ref, v_ref):
    u = u_ref[0]
    qn = _rmsnorm(u[:, :MLA_Q_RANK], qg_ref[...]).astype(BF16)
    kvn = _rmsnorm(u[:, MLA_Q_RANK:MLA_Q_RANK + MLA_KV_RANK], kg_ref[...]).astype(BF16)
    o = MLA_Q_RANK + MLA_KV_RANK
    kr = u[:, o:o + 128] * ck_ref[...] + u[:, o + 128:o + 256] * sk_ref[...]
    qq = _dot(qn, wq_ref[...])
    q_ref[0] = (qq[:, :MLA_QW] * cq_ref[...] + qq[:, MLA_QW:] * sq_ref[...]).astype(BF16)
    k_ref[0] = (_dot(kvn, wk_ref[...]) + _dot2(kr, pl_ref[...])).astype(BF16)
    v_ref[0] = _dot(kvn, wv_ref[...]).astype(BF16)


def _mla_proj(um, q_norm_g, kv_norm_g, weights, tables):
    B = um.shape[0]
    wq, wk, wv, place = weights
    cq, sq, ck, sk = tables
    tok = lambda n: pl.BlockSpec((1, TM, n), lambda b, t: (b, t, 0))
    tab = lambda n: pl.BlockSpec((TM, n), lambda b, t: (t, 0))
    full = lambda a: pl.BlockSpec(a.shape, lambda b, t: (0, 0))
    return pl.pallas_call(
        _mla_proj_kernel,
        grid=(B, N_TILES),
        in_specs=[tok(MLA_U), pl.BlockSpec((1, MLA_Q_RANK), lambda b, t: (0, 0)),
                  pl.BlockSpec((1, MLA_KV_RANK), lambda b, t: (0, 0)),
                  full(wq), full(wk), full(wv), full(place),
                  tab(MLA_QW), tab(MLA_QW), tab(MLA_SLOT), tab(MLA_SLOT)],
        out_specs=[tok(MLA_QW), tok(MLA_QW), tok(MLA_HEADS * MLA_V)],
        out_shape=[jax.ShapeDtypeStruct((B, NT, MLA_QW), BF16),
                   jax.ShapeDtypeStruct((B, NT, MLA_QW), BF16),
                   jax.ShapeDtypeStruct((B, NT, MLA_HEADS * MLA_V), BF16)],
        compiler_params=_cp("parallel", "parallel"),
        name="mla_proj",
    )(um, q_norm_g.reshape(1, -1), kv_norm_g.reshape(1, -1), wq, wk, wv, place, cq, sq, ck, sk)


MLA_TQ = 512


def _mla_attn_kernel(q_ref, k_ref, v_ref, o_ref):
    t = pl.program_id(2)
    is_ctx = t == T_L // MLA_TQ
    lo_lanes = lax.broadcasted_iota(I32, (MLA_TQ, 128), 1) < MLA_V

    def attend(keys):
        n_keys = keys.stop - keys.start
        lo_rows = lax.broadcasted_iota(I32, (n_keys, 128), 1) < MLA_V
        vall = v_ref[0, keys, :]
        one = jnp.ones((n_keys, 128), vall.dtype)
        vh = [jnp.where(lo_rows, vall, one), jnp.where(lo_rows, one, vall)]
        s = [_dot_nt(q_ref[0, :, hh * MLA_SLOT:(hh + 1) * MLA_SLOT], k_ref[0, keys, hh * MLA_SLOT:(hh + 1) * MLA_SLOT])
             for hh in range(2)]
        e = [jnp.exp(x - jnp.max(x, axis=-1, keepdims=True)).astype(BF16) for x in s]
        r = [_dot(e[hh], vh[hh]) for hh in range(2)]
        num = jnp.where(lo_lanes, r[0], r[1])
        den = pltpu.roll(jnp.where(lo_lanes, r[1], r[0]), MLA_V, 1)
        o_ref[0] = (num * (1.0 / den)).astype(o_ref.dtype)

    @pl.when(jnp.logical_not(is_ctx))
    def _():
        attend(slice(0, NT))

    @pl.when(is_ctx)
    def _():
        attend(slice(T_L, NT))


def _mla_attn(q, k, v, need_ctx):
    B = q.shape[0]
    n_tiles = pl.cdiv(NT, MLA_TQ) if need_ctx else T_L // MLA_TQ
    return pl.pallas_call(
        _mla_attn_kernel,
        grid=(B, MLA_HEADS // 2, n_tiles),
        in_specs=[pl.BlockSpec((1, MLA_TQ, 2 * MLA_SLOT), lambda b, p, t: (b, t, p)),
                  pl.BlockSpec((1, NT, 2 * MLA_SLOT), lambda b, p, t: (b, 0, p)),
                  pl.BlockSpec((1, NT, 2 * MLA_V), lambda b, p, t: (b, 0, p))],
        out_specs=pl.BlockSpec((1, MLA_TQ, 2 * MLA_V), lambda b, p, t: (b, t, p)),
        out_shape=jax.ShapeDtypeStruct((B, NT, MLA_HEADS * MLA_V), BF16),
        compiler_params=_cp("parallel", "parallel", "parallel"),
        name="mla_attn",
    )(q, k, v)


def kernel(x, c, ctx, c_ctx, ada_w, ada_b, norm_mix_g, norm_ffn_g, router_w, exp_w_gate, exp_w_up, exp_w_down, ev_w_in, ev_w_out, ev_sink, ev_pool_w, ev_pool_scale, od_w_in, od_w_out, od_mu_prev, od_mu_next, od_w0, od_w2, od_a0, od_a2, od_g2, od_k_k, od_k_a, od_r_k, od_lnx_g, od_lnx_b, od_q_norm_g, od_w_uq, od_kv_norm_g, od_w_ukv, final_norm_g):
    depth = ada_w.shape[0]
    assert x.shape[1:] == (T_L, D) and ctx.shape[1:] == (T_C, D)
    mods = _modvec(c, c_ctx, ada_w, ada_b)
    ev_tables, ev_partner = _even_tables()
    mla_tables, mla_partner = _mla_tables()
    bd = _head_ones()
    xs = jnp.concatenate([x, ctx], axis=1)
    for l in range(depth):
        need_ctx = l < depth - 1
        i = l // 2
        mod = mods[l]
        if l % 2 == 0:
            q, kd, vd, pu = _inproj_even(xs, mod, norm_mix_g[l], _even_weights(ev_w_in[i], ev_partner), ev_tables)
            mix_a = _attn_even(q, kd, vd, ev_sink[i])
            mix_b = _pool(pu, ev_pool_w[i], ev_pool_scale[i])
            w_out = ev_w_out[i]
        else:
            ur, um = _inproj_odd(xs, mod, norm_mix_g[l], _odd_weights(od_w_in[i], mla_partner))
            r, v, kk, g, lw, bq, kdir = _rwkv_prep(ur, od_mu_prev[i], od_mu_next[i], od_w0[i], od_w2[i],
                                                   od_a0[i], od_a2[i], od_g2[i], od_k_k[i], od_k_a[i], bd)
            o = _wkv(r, v, kk, lw, bq, kdir)
            mix_a = _rwkv_finish(o, r, v, g, kdir, od_r_k[i], od_lnx_g[i], od_lnx_b[i], bd, need_ctx)
            mq, mk, mv = _mla_proj(um, od_q_norm_g[i], od_kv_norm_g[i],
                                   _mla_weights(od_w_uq[i], od_w_ukv[i], mla_partner), mla_tables)
            mix_b = _mla_attn(mq, mk, mv, need_ctx)
            w_out = od_w_out[i]
        xs, h, logits = _outproj(mix_a, mix_b, w_out, xs, mod, norm_ffn_g[l], router_w[l], need_ctx)
        xs = _moe(xs, h, logits, mod, exp_w_gate, exp_w_up, exp_w_down, l, need_ctx,
                  None if need_ctx else final_norm_g)
    return xs
```

```python
import functools

import numpy as np
import jax
import jax.numpy as jnp
from jax import lax
from jax.experimental import pallas as pl
from jax.experimental.pallas import tpu as pltpu

F32 = jnp.float32
BF16 = jnp.bfloat16
I32 = jnp.int32

D = 1024
T_L = 2048
T_C = 256
NT = T_L + T_C
GRID_W = 64
EPS = 1e-6
ROPE_BASE = 10000.0

TM = 512
N_LAT_TILES = T_L // TM
N_TILES = N_LAT_TILES + 1
assert T_C <= TM and T_L % TM == 0

A_HEADS = 8
A_KV_HEADS = 2
HEAD_DIM = 64
A_WINDOW = 128
A_BLOCK = 128
N_ABLK = NT // A_BLOCK
N_LAT_ABLK = T_L // A_BLOCK

POOL_WINDOWS = (2, 4, 8, 16)
POOL_GROUP = 128

RWKV_HEADS = 8
RWKV_HEAD = 64
RWKV_WIDTH = 512
RWKV_IN = 1920
LNX_EPS = 64e-5
WKV_CHUNK = 64
N_CHUNKS = NT // WKV_CHUNK
N_CTX_CHUNKS = T_C // WKV_CHUNK

MLA_HEADS = 8
MLA_Q_RANK = 256
MLA_KV_RANK = 128
MLA_NOPE = 64
MLA_ROPE = 32
MLA_V = 64
MLA_SLOT = 128
MLA_U = 640

N_EXPERTS = 16
EXPERT_FF = 2048
CAP_L = 2 * T_L // N_EXPERTS
CAP_C = 2 * T_C // N_EXPERTS
FF_CHUNK = 512

VMEM_LIMIT_BYTES = 56 * 1024 * 1024


def _cp(*sem):
    return pltpu.CompilerParams(dimension_semantics=sem, vmem_limit_bytes=VMEM_LIMIT_BYTES)


def _split_bf16(x):
    hi = x.astype(BF16)
    lo = (x - hi.astype(F32)).astype(BF16)
    return hi, lo


def _dot(a, b):
    return jnp.dot(a, b, preferred_element_type=F32)


def _dot_nt(a, b):
    return lax.dot_general(a, b, (((1,), (1,)), ((), ())), preferred_element_type=F32)


def _dot_tn(a, b):
    return lax.dot_general(a, b, (((0,), (0,)), ((), ())), preferred_element_type=F32)


def _dot2(a_f32, b_bf16):
    hi, lo = _split_bf16(a_f32)
    return _dot(hi, b_bf16) + _dot(lo, b_bf16)


def _modulate(x, g, shift, scale):
    ms = jnp.mean(x * x, axis=-1, keepdims=True)
    return x * lax.rsqrt(ms + EPS) * g * (1.0 + scale) + shift


def _rmsnorm(x, g):
    ms = jnp.mean(x * x, axis=-1, keepdims=True)
    return x * lax.rsqrt(ms + EPS) * g


def _modvec_kernel(a_ref, w_ref, b_ref, o_ref):
    a = a_ref[...]
    a = a * jax.nn.sigmoid(a)
    a_hi, a_lo = _split_bf16(a)
    w_hi, w_lo = _split_bf16(w_ref[0])
    o_ref[0] = _dot(a_hi, w_hi) + _dot(a_lo, w_hi) + _dot(a_hi, w_lo) + b_ref[0]


def _modvec(c, c_ctx, ada_w, ada_b):
    L = ada_w.shape[0]
    B = c.shape[0]
    assert B < 16
    rows = jnp.zeros((16, D), F32).at[:B].set(c).at[B].set(c_ctx)
    tn = 512
    out = pl.pallas_call(
        _modvec_kernel,
        grid=(L, 6 * D // tn),
        in_specs=[pl.BlockSpec((16, D), lambda l, n: (0, 0)),
                  pl.BlockSpec((1, D, tn), lambda l, n: (l, 0, n)),
                  pl.BlockSpec((1, 1, tn), lambda l, n: (l, 0, n))],
        out_specs=pl.BlockSpec((1, 16, tn), lambda l, n: (l, 0, n)),
        out_shape=jax.ShapeDtypeStruct((L, 16, 6 * D), F32),
        compiler_params=_cp("parallel", "parallel"),
        name="modvec",
    )(rows, ada_w, ada_b.reshape(L, 1, 6 * D))
    m = out.reshape(L, 16, 6, D)
    ml = m[:, :B]
    mc = jnp.broadcast_to(m[:, B:B + 1], (L, B, 6, D))
    mod = jnp.stack([ml, mc], axis=2)
    mod = jnp.pad(mod, ((0, 0), (0, 0), (0, 0), (0, 2), (0, 0)))
    return mod.reshape(L, B * 2, 8, D)


def _mod_spec():
    return pl.BlockSpec((1, 8, D), lambda b, t: (b * 2 + t // N_LAT_TILES, 0, 0))


def _rope_block(width, pos_row, pos_col):
    d = width // 2
    half = d // 2
    inv = ROPE_BASE ** (-np.arange(half, dtype=np.float32) / half)
    cos = np.zeros((pos_row.shape[0], width), np.float32)
    sin = np.zeros_like(cos)
    partner = np.zeros((width,), np.int64)
    for part, pos in ((0, pos_row), (1, pos_col)):
        ang = pos.astype(np.float32)[:, None] * inv[None, :]
        c, s = np.cos(ang), np.sin(ang)
        o = part * d
        cos[:, o:o + half] = c
        cos[:, o + half:o + d] = c
        sin[:, o:o + half] = -s
        sin[:, o + half:o + d] = s
        partner[o:o + half] = np.arange(o + half, o + d)
        partner[o + half:o + d] = np.arange(o, o + half)
    return cos, sin, partner


def _latent_positions():
    t = np.arange(T_L)
    return t // GRID_W, t % GRID_W


def _even_tables():
    cos, sin, partner = _rope_block(HEAD_DIM, *_latent_positions())
    scale = HEAD_DIM ** -0.5
    ones = np.ones((T_C, HEAD_DIM), np.float32)
    zeros = np.zeros((T_C, HEAD_DIM), np.float32)
    cos = np.concatenate([cos, ones], axis=0)
    sin = np.concatenate([sin, zeros], axis=0)
    cq = np.tile(cos, (1, A_HEADS)) * scale
    sq = np.tile(sin, (1, A_HEADS)) * scale
    ck = np.tile(cos, (1, 2 * A_KV_HEADS))
    sk = np.tile(sin, (1, 2 * A_KV_HEADS))
    return (jnp.asarray(cq), jnp.asarray(sq), jnp.asarray(ck), jnp.asarray(sk)), partner


def _mla_tables():
    cos, sin, partner = _rope_block(MLA_ROPE, *_latent_positions())
    cos = np.concatenate([cos, np.ones((T_C, MLA_ROPE), np.float32)], axis=0)
    sin = np.concatenate([sin, np.zeros((T_C, MLA_ROPE), np.float32)], axis=0)
    scale = (MLA_NOPE + MLA_ROPE) ** -0.5
    cq1 = np.zeros((NT, MLA_SLOT), np.float32)
    sq1 = np.zeros((NT, MLA_SLOT), np.float32)
    cq1[:, :MLA_NOPE] = 1.0
    cq1[:, MLA_NOPE:MLA_NOPE + MLA_ROPE] = cos
    sq1[:, MLA_NOPE:MLA_NOPE + MLA_ROPE] = sin
    cq = np.tile(cq1, (1, MLA_HEADS)) * scale
    sq = np.tile(sq1, (1, MLA_HEADS)) * scale
    ck = np.zeros((NT, MLA_SLOT), np.float32)
    sk = np.zeros((NT, MLA_SLOT), np.float32)
    ck[:, :MLA_ROPE] = cos
    sk[:, :MLA_ROPE] = sin
    return (jnp.asarray(cq), jnp.asarray(sq), jnp.asarray(ck), jnp.asarray(sk)), partner


EV_Q, EV_K, EV_V, EV_P = 512, 256, 256, 512
EV_N = 2 * EV_Q + 2 * EV_K + EV_V + EV_P


def _gather_columns(w, index):
    w0 = jnp.concatenate([w, jnp.zeros((w.shape[0], 1), w.dtype)], axis=1).astype(BF16)
    return w0[:, np.where(index < 0, w.shape[1], index)]


def _even_weights(w_in, partner):
    pq = np.concatenate([h * HEAD_DIM + partner for h in range(A_HEADS)])
    pk = np.concatenate([h * HEAD_DIM + partner for h in range(A_KV_HEADS)])
    dup = np.concatenate([np.arange(HEAD_DIM), np.arange(HEAD_DIM),
                          HEAD_DIM + np.arange(HEAD_DIM), HEAD_DIM + np.arange(HEAD_DIM)])
    index = np.concatenate([np.arange(512), pq, 512 + dup, 512 + pk[dup], 640 + dup, 768 + np.arange(512)])
    return _gather_columns(w_in, index)


def _inproj_even_kernel(x_ref, mod_ref, g_ref, w_ref, cq_ref, sq_ref, ck_ref, sk_ref,
                        q_ref, k_ref, v_ref, p_ref):
    mod = mod_ref[0]
    h = _modulate(x_ref[0], g_ref[...], mod[0:1], mod[1:2])
    y = _dot(h.astype(BF16), w_ref[...])
    o = 0
    q = y[:, o:o + EV_Q] * cq_ref[...] + y[:, o + EV_Q:o + 2 * EV_Q] * sq_ref[...]
    o += 2 * EV_Q
    k = y[:, o:o + EV_K] * ck_ref[...] + y[:, o + EV_K:o + 2 * EV_K] * sk_ref[...]
    o += 2 * EV_K
    q_ref[0] = q.astype(BF16)
    k_ref[0] = k.astype(BF16)
    v_ref[0] = y[:, o:o + EV_V].astype(BF16)
    p_ref[0] = y[:, o + EV_V:o + EV_V + EV_P]


def _inproj_even(x, mod, g, w, tables):
    B = x.shape[0]
    cq, sq, ck, sk = tables
    tok = lambda n: pl.BlockSpec((1, TM, n), lambda b, t: (b, t, 0))
    tab = lambda n: pl.BlockSpec((TM, n), lambda b, t: (t, 0))
    return pl.pallas_call(
        _inproj_even_kernel,
        grid=(B, N_TILES),
        in_specs=[tok(D), _mod_spec(), pl.BlockSpec((1, D), lambda b, t: (0, 0)),
                  pl.BlockSpec((D, EV_N), lambda b, t: (0, 0)),
                  tab(EV_Q), tab(EV_Q), tab(EV_K), tab(EV_K)],
        out_specs=[tok(EV_Q), tok(EV_K), tok(EV_V), tok(EV_P)],
        out_shape=[jax.ShapeDtypeStruct((B, NT, EV_Q), BF16),
                   jax.ShapeDtypeStruct((B, NT, EV_K), BF16),
                   jax.ShapeDtypeStruct((B, NT, EV_V), BF16),
                   jax.ShapeDtypeStruct((B, NT, EV_P), F32)],
        compiler_params=_cp("parallel", "parallel"),
        name="inproj_even",
    )(x, mod, g.reshape(1, D), w, cq, sq, ck, sk)


def _attn_even_kernel(sink_ref, q_ref, kp_ref, kc_ref, kn_ref, kx_ref,
                      vp_ref, vc_ref, vn_ref, vx_ref, o_ref):
    j = pl.program_id(1)
    is_lat = j < N_LAT_ABLK
    q = q_ref[0]
    qpos = j * A_BLOCK + lax.broadcasted_iota(I32, (A_BLOCK, 3 * A_BLOCK), 0)
    kpos = (j - 1) * A_BLOCK + lax.broadcasted_iota(I32, (A_BLOCK, 3 * A_BLOCK), 1)
    valid = (jnp.abs(kpos - qpos) <= A_WINDOW) & (kpos >= 0) & (kpos < T_L) & is_lat
    lo_lanes = lax.broadcasted_iota(I32, (A_BLOCK, 128), 1) < HEAD_DIM
    kall, vall = [], []
    for kv in range(A_KV_HEADS):
        ks = slice(kv * 128, (kv + 1) * 128)
        kall.append(jnp.concatenate([kx_ref[0, :, ks], kp_ref[0, :, ks], kc_ref[0, :, ks], kn_ref[0, :, ks]], axis=0))
        vall.append(jnp.concatenate([vx_ref[0, :, ks], vp_ref[0, :, ks], vc_ref[0, :, ks], vn_ref[0, :, ks]], axis=0))
    heads = range(A_HEADS)
    group = A_HEADS // A_KV_HEADS
    qm = [jnp.where(lo_lanes if h % 2 == 0 else jnp.logical_not(lo_lanes), q[:, (h // 2) * 128:(h // 2 + 1) * 128],
                    jnp.zeros((A_BLOCK, 128), q.dtype)) for h in heads]
    s = [_dot_nt(qm[h], kall[h // group]) for h in heads]
    e, inv_den = [], []
    for h in heads:
        snk = sink_ref[h]
        s_ctx = s[h][:, :T_C]
        s_loc = jnp.where(valid, s[h][:, T_C:], -jnp.inf)
        m = jnp.maximum(jnp.max(s_ctx, axis=-1, keepdims=True), jnp.max(s_loc, axis=-1, keepdims=True))
        m = jnp.maximum(m, snk)
        e_ctx = jnp.exp(s_ctx - m)
        e_loc = jnp.exp(s_loc - m)
        den = (jnp.sum(e_ctx, axis=-1, keepdims=True) + jnp.sum(e_loc, axis=-1, keepdims=True)
               + jnp.exp(snk - m))
        e.append(jnp.concatenate([e_ctx, e_loc], axis=1).astype(BF16))
        inv_den.append(1.0 / den)
    o = [_dot(e[h], vall[h // group]) * inv_den[h] for h in heads]
    for p in range(A_HEADS // 2):
        o_ref[0, :, p * 128:(p + 1) * 128] = jnp.where(lo_lanes, o[2 * p], o[2 * p + 1]).astype(o_ref.dtype)


def _attn_even(q, kd, vd, sink):
    B = q.shape[0]
    last = N_LAT_ABLK - 1
    blk = lambda f: pl.BlockSpec((1, A_BLOCK, 256), f)
    prev = lambda b, j: (b, jnp.clip(j - 1, 0, last), 0)
    cur = lambda b, j: (b, j, 0)
    nxt = lambda b, j: (b, jnp.clip(j + 1, 0, last), 0)
    ctx = pl.BlockSpec((1, T_C, 256), lambda b, j: (b, T_L // T_C, 0))
    return pl.pallas_call(
        _attn_even_kernel,
        grid=(B, N_ABLK),
        in_specs=[pl.BlockSpec(memory_space=pltpu.SMEM),
                  pl.BlockSpec((1, A_BLOCK, EV_Q), cur),
                  blk(prev), blk(cur), blk(nxt), ctx,
                  blk(prev), blk(cur), blk(nxt), ctx],
        out_specs=pl.BlockSpec((1, A_BLOCK, 512), cur),
        out_shape=jax.ShapeDtypeStruct((B, NT, 512), BF16),
        compiler_params=_cp("parallel", "parallel"),
        name="attn_even",
    )(sink, q, kd, kd, kd, kd, vd, vd, vd, vd)


POOL_BLOCK = 256
POOL_HALO = 8


def _pool_kernel(up_ref, uc_ref, un_ref, w_ref, sc_ref, o_ref):
    j = pl.program_id(1)
    R = POOL_BLOCK
    is_lat = j < T_L // R
    seg_lo = jnp.where(is_lat, 0, T_L)
    seg_hi = jnp.where(is_lat, T_L, NT)
    uc = uc_ref[0]
    halo = jnp.concatenate([up_ref[0], un_ref[0]], axis=0)
    r = j * R + lax.broadcasted_iota(I32, (R, R), 0)
    c = j * R + lax.broadcasted_iota(I32, (R, R), 1)
    rh = j * R + lax.broadcasted_iota(I32, (R, 2 * POOL_HALO), 0)
    ch = lax.broadcasted_iota(I32, (R, 2 * POOL_HALO), 1)
    ch = j * R + jnp.where(ch < POOL_HALO, ch - POOL_HALO, ch - POOL_HALO + R)
    r1 = j * R + lax.broadcasted_iota(I32, (R, 1), 0)
    hi_u, lo_u = _split_bf16(uc)
    hi_h, lo_h = _split_bf16(halo)
    groups = range(len(POOL_WINDOWS))
    cols = [slice(gi * POOL_GROUP, (gi + 1) * POOL_GROUP) for gi in groups]
    bands, bands_h, inv_cnt = [], [], []
    for w in POOL_WINDOWS:
        in_window = lambda rows, cc: ((cc >= jnp.maximum(rows - w // 2, seg_lo))
                                      & (cc < jnp.minimum(rows + w // 2, seg_hi))).astype(BF16)
        bands.append(in_window(r, c))
        bands_h.append(in_window(rh, ch))
        cnt = (jnp.minimum(r1 + w // 2, seg_hi) - jnp.maximum(r1 - w // 2, seg_lo)).astype(F32)
        inv_cnt.append(1.0 / cnt)
    sums = [_dot(bands[gi], jnp.concatenate([hi_u[:, cols[gi]], lo_u[:, cols[gi]]], axis=1))
            + _dot(bands_h[gi], jnp.concatenate([hi_h[:, cols[gi]], lo_h[:, cols[gi]]], axis=1)) for gi in groups]
    dev = [((s[:, :POOL_GROUP] + s[:, POOL_GROUP:]) * inv_cnt[gi] - uc[:, cols[gi]]).astype(BF16)
           for gi, s in enumerate(sums)]
    outs = [_dot(dev[gi], w_ref[gi]) for gi in groups]
    o_ref[0] = (jnp.concatenate(outs, axis=1) * sc_ref[...]).astype(o_ref.dtype)


def _pool(u, pool_w, pool_scale):
    B = u.shape[0]
    slabs = POOL_BLOCK // POOL_HALO
    slab = lambda f: pl.BlockSpec((1, POOL_HALO, 512), f)
    blk = pl.BlockSpec((1, POOL_BLOCK, 512), lambda b, j: (b, j, 0))
    return pl.pallas_call(
        _pool_kernel,
        grid=(B, NT // POOL_BLOCK),
        in_specs=[slab(lambda b, j: (b, jnp.maximum(j * slabs - 1, 0), 0)),
                  blk,
                  slab(lambda b, j: (b, jnp.minimum((j + 1) * slabs, NT // POOL_HALO - 1), 0)),
                  pl.BlockSpec((4, POOL_GROUP, POOL_GROUP), lambda b, j: (0, 0, 0)),
                  pl.BlockSpec((1, 512), lambda b, j: (0, 0))],
        out_specs=blk,
        out_shape=jax.ShapeDtypeStruct((B, NT, 512), BF16),
        compiler_params=_cp("parallel", "parallel"),
        name="pool",
    )(u, u, u, pool_w.astype(BF16), pool_scale.reshape(1, 512))


def _outproj_kernel(a_ref, p_ref, w_ref, x_ref, mod_ref, g_ref, wr_ref, xo_ref, h_ref, lg_ref):
    mod = mod_ref[0]
    y = _dot(a_ref[0], w_ref[0:512, :]) + _dot(p_ref[0], w_ref[512:1024, :])
    x = x_ref[0] + mod[2:3] * y
    xo_ref[0] = x
    h = _modulate(x, g_ref[...], mod[3:4], mod[4:5])
    w_hi, w_lo = _split_bf16(wr_ref[...])
    h_hi, h_lo = _split_bf16(h)
    h_ref[0] = h_hi
    lg_ref[0] = _dot_nt(w_hi, h_hi) + _dot_nt(w_lo, h_hi) + _dot_nt(w_hi, h_lo)


def _outproj(a, p, w_out, x, mod, g_ffn, w_router, need_ctx):
    B = x.shape[0]
    tok = lambda n: pl.BlockSpec((1, TM, n), lambda b, t: (b, t, 0))
    return pl.pallas_call(
        _outproj_kernel,
        grid=(B, N_TILES if need_ctx else N_LAT_TILES),
        in_specs=[tok(512), tok(512), pl.BlockSpec((D, D), lambda b, t: (0, 0)), tok(D), _mod_spec(),
                  pl.BlockSpec((1, D), lambda b, t: (0, 0)),
                  pl.BlockSpec((N_EXPERTS, D), lambda b, t: (0, 0))],
        out_specs=[tok(D), tok(D), pl.BlockSpec((1, N_EXPERTS, TM), lambda b, t: (b, 0, t))],
        out_shape=[jax.ShapeDtypeStruct((B, NT, D), F32),
                   jax.ShapeDtypeStruct((B, NT, D), BF16),
                   jax.ShapeDtypeStruct((B, N_EXPERTS, NT), F32)],
        compiler_params=_cp("parallel", "parallel"),
        name="outproj",
    )(a, p, w_out.astype(BF16), x, mod, g_ffn.reshape(1, D), w_router.T)


def _topk_kernel(lg_ref, sid_ref, gate_ref, *, cap):
    B, E, T = lg_ref.shape
    R = B * E
    lg = lg_ref[...]
    m = jnp.max(lg, axis=1, keepdims=True)
    e = jnp.exp(lg - m)
    aff = (e / jnp.sum(e, axis=1, keepdims=True)).reshape(R, T)
    bits = lax.bitcast_convert_type(aff, I32)

    def step(i, thr):
        cand = thr | jnp.left_shift(jnp.int32(1), 30 - i)
        cnt = jnp.sum((bits >= cand).astype(F32), axis=1, keepdims=True)
        return jnp.where(cnt >= cap, cand, thr)

    thr = lax.fori_loop(0, 31, step, jnp.zeros((R, 1), I32))
    gt = bits > thr
    eq = bits == thr
    need = cap - jnp.sum(gt.astype(F32), axis=1, keepdims=True)

    blk = 256
    tri = (lax.broadcasted_iota(I32, (blk, blk), 0) <= lax.broadcasted_iota(I32, (blk, blk), 1)).astype(BF16)

    def prefix(mask_f32):
        off = jnp.zeros((R, 1), F32)
        parts = []
        for k in range(T // blk):
            cb = _dot(mask_f32[:, k * blk:(k + 1) * blk].astype(BF16), tri) + off
            parts.append(cb)
            off = cb[:, blk - 1:blk]
        return parts

    eq_parts = prefix(eq.astype(F32))
    sel_parts = []
    for k in range(T // blk):
        sl = slice(k * blk, (k + 1) * blk)
        sel_parts.append(gt[:, sl] | (eq[:, sl] & (eq_parts[k] <= need)))
    sel = jnp.concatenate(sel_parts, axis=1)
    pos_parts = prefix(sel.astype(F32))
    for k in range(T // blk):
        sl = slice(k * blk, (k + 1) * blk)
        sid = jnp.where(sel_parts[k], pos_parts[k].astype(I32) - 1, -1)
        sid_ref[:, :, sl] = sid.reshape(B, E, blk)
        gate_ref[:, :, sl] = jnp.where(sel_parts[k], aff[:, sl], 0.0).reshape(B, E, blk)


def _topk(logits, cap):
    B, E, T = logits.shape
    full = pl.BlockSpec((B, E, T), lambda i: (0, 0, 0))
    return pl.pallas_call(
        functools.partial(_topk_kernel, cap=cap),
        grid=(1,),
        in_specs=[full],
        out_specs=[full, full],
        out_shape=[jax.ShapeDtypeStruct((B, E, T), I32), jax.ShapeDtypeStruct((B, E, T), F32)],
        compiler_params=_cp("arbitrary"),
        name=f"topk_{T}",
    )(logits)


GATHER_COLS = 512


def _gather_kernel(h_ref, sid_ref, gate_ref, xs_ref, gs_ref, *, cap, chunk):
    T = h_ref.shape[1]
    E = sid_ref.shape[1]
    acc = jnp.zeros((E * cap, GATHER_COLS), F32)
    slot = lax.broadcasted_iota(I32, (cap, chunk), 0)
    for k in range(T // chunk):
        sl = slice(k * chunk, (k + 1) * chunk)
        hit = jnp.concatenate([(slot == sid_ref[0, e, :, sl]).astype(BF16) for e in range(E)], axis=0)
        acc = acc + _dot(hit, h_ref[0, sl, :])
    for e in range(E):
        xs_ref[e] = acc[e * cap:(e + 1) * cap].astype(BF16)

    @pl.when(pl.program_id(1) == 0)
    def _():
        slot_t = lax.broadcasted_iota(I32, (cap, T), 0)
        for e in range(E):
            picked = jnp.where(slot_t == sid_ref[0, e], gate_ref[0, e], 0.0)
            gs_ref[e] = jnp.broadcast_to(jnp.sum(picked, axis=1, keepdims=True), (cap, 128))


def _gather(h, sid, gate, cap, row_block):
    B, E, T = sid.shape
    chunk = min(T, 512)
    idx = lambda b, j: (b, 0, 0, 0)
    return pl.pallas_call(
        functools.partial(_gather_kernel, cap=cap, chunk=chunk),
        grid=(B, D // GATHER_COLS),
        in_specs=[pl.BlockSpec((1, T, GATHER_COLS), lambda b, j: (b, row_block, j)),
                  pl.BlockSpec((1, E, 1, T), idx), pl.BlockSpec((1, E, 1, T), idx)],
        out_specs=[pl.BlockSpec((E, cap, GATHER_COLS), lambda b, j: (0, b, j)),
                   pl.BlockSpec((E, cap, 128), lambda b, j: (0, b, 0))],
        out_shape=[jax.ShapeDtypeStruct((E, B * cap, D), BF16),
                   jax.ShapeDtypeStruct((E, B * cap, 128), F32)],
        compiler_params=_cp("parallel", "arbitrary"),
        name=f"gather_{T}",
    )(h, sid.reshape(B, E, 1, T), gate.reshape(B, E, 1, T))


DOWN_CHUNK = 256


def _ffn_kernel(*refs, row_tiles):
    n_seg = len(row_tiles)
    xs_refs = refs[:n_seg]
    gs_refs = refs[n_seg:2 * n_seg]
    wg_ref, wu_ref, wd_ref = refs[2 * n_seg:2 * n_seg + 3]
    y_refs = refs[2 * n_seg + 3:3 * n_seg + 3]
    hid_refs = refs[3 * n_seg + 3:]
    s = pl.program_id(1)
    nf = EXPERT_FF // FF_CHUNK

    @pl.when(s < nf)
    def _():
        wg = wg_ref[0].astype(BF16)
        wu = wu_ref[0].astype(BF16)
        for xs_ref, hid_ref, rt in zip(xs_refs, hid_refs, row_tiles):
            for r0 in range(0, xs_ref.shape[1], rt):
                rows = pl.ds(r0, rt)
                x = xs_ref[0, rows, :]
                g = _dot(x, wg)
                u = _dot(x, wu)
                hid_ref[s, rows, :] = (g * jax.nn.sigmoid(g) * u).astype(BF16)

    @pl.when(s >= nf)
    def _():
        d = s - nf
        wd = wd_ref[0].astype(BF16)
        for gs_ref, y_ref, hid_ref, rt in zip(gs_refs, y_refs, hid_refs, row_tiles):
            for r0 in range(0, hid_ref.shape[1], rt):
                rows = pl.ds(r0, rt)
                acc = _dot(hid_ref[0, rows, :], wd[0:FF_CHUNK, :])
                for f in range(1, nf):
                    acc = acc + _dot(hid_ref[f, rows, :], wd[f * FF_CHUNK:(f + 1) * FF_CHUNK, :])
                gs = gs_ref[0, rows, :]
                y_ref[0, d, rows, :] = jnp.concatenate(
                    [acc[:, k * 128:(k + 1) * 128] * gs for k in range(DOWN_CHUNK // 128)], axis=1).astype(y_ref.dtype)


def _ffn(xs_list, gs_list, w_gate, w_up, w_down, layer):
    E = w_gate.shape[1]
    row_tiles = tuple(min(xs.shape[1], 512) for xs in xs_list)
    nf = EXPERT_FF // FF_CHUNK
    nd = D // DOWN_CHUNK
    in_specs = [pl.BlockSpec((1, xs.shape[1], D), lambda e, s: (e, 0, 0)) for xs in xs_list]
    in_specs += [pl.BlockSpec((1, gs.shape[1], 128), lambda e, s: (e, 0, 0)) for gs in gs_list]
    up_map = lambda e, s: (layer, e, 0, jnp.minimum(s, nf - 1))
    in_specs += [pl.BlockSpec((None, 1, D, FF_CHUNK), up_map),
                 pl.BlockSpec((None, 1, D, FF_CHUNK), up_map),
                 pl.BlockSpec((None, 1, EXPERT_FF, DOWN_CHUNK), lambda e, s: (layer, e, 0, jnp.maximum(s - nf, 0)))]
    return pl.pallas_call(
        functools.partial(_ffn_kernel, row_tiles=row_tiles),
        grid=(E, nf + nd),
        in_specs=in_specs,
        out_specs=[pl.BlockSpec((1, nd, xs.shape[1], DOWN_CHUNK), lambda e, s: (e, 0, 0, 0)) for xs in xs_list],
        out_shape=[jax.ShapeDtypeStruct((E, nd, xs.shape[1], DOWN_CHUNK), BF16) for xs in xs_list],
        scratch_shapes=[pltpu.VMEM((nf, xs.shape[1], FF_CHUNK), BF16) for xs in xs_list],
        compiler_params=_cp("parallel", "arbitrary"),
        name="expert_ffn",
    )(*xs_list, *gs_list, w_gate, w_up, w_down)


def _combine_kernel(x_ref, sid_ref, y_ref, mod_ref, *rest, cap):
    o_ref = rest[-1]
    tbs = x_ref.shape[1]
    E = sid_ref.shape[1]
    slot = lax.broadcasted_iota(I32, (cap, tbs), 0)
    hit = jnp.concatenate([(slot == sid_ref[0, e]).astype(BF16) for e in range(E)], axis=0)
    gate = mod_ref[0][5:6]
    for d in range(D // DOWN_CHUNK):
        cs = slice(d * DOWN_CHUNK, (d + 1) * DOWN_CHUNK)
        z = _dot_tn(hit, y_ref[:, d].reshape(E * cap, DOWN_CHUNK))
        o_ref[0, :, cs] = x_ref[0, :, cs] + gate[:, cs] * z
    if len(rest) == 2:
        o_ref[0] = _rmsnorm(o_ref[0], rest[0][...])


def _combine(x, sid, y, mod, cap, tbs, row_block0, segment, final_g=None):
    B, E, T = sid.shape
    ntb = T // tbs
    xmap = lambda b, tb: (b, row_block0 + tb, 0)
    in_specs = [pl.BlockSpec((1, tbs, D), xmap),
                pl.BlockSpec((1, E, 1, tbs), lambda b, tb: (b, 0, 0, tb)),
                pl.BlockSpec((E, D // DOWN_CHUNK, cap, DOWN_CHUNK), lambda b, tb: (0, 0, b, 0)),
                pl.BlockSpec((1, 8, D), lambda b, tb: (b * 2 + segment, 0, 0))]
    args = [x, sid.reshape(B, E, 1, T), y, mod]
    if final_g is None:
        out_spec, out_shape, alias = pl.BlockSpec((1, tbs, D), xmap), x.shape, {0: 0}
    else:
        assert row_block0 == 0
        in_specs.append(pl.BlockSpec((1, D), lambda b, tb: (0, 0)))
        args.append(final_g.reshape(1, D))
        out_spec, out_shape, alias = pl.BlockSpec((1, tbs, D), lambda b, tb: (b, tb, 0)), (B, T, D), {}
    return pl.pallas_call(
        functools.partial(_combine_kernel, cap=cap),
        grid=(B, ntb),
        in_specs=in_specs,
        out_specs=out_spec,
        out_shape=jax.ShapeDtypeStruct(out_shape, F32),
        input_output_aliases=alias,
        compiler_params=_cp("parallel", "parallel"),
        name=f"combine_{T}" + ("" if final_g is None else "_final"),
    )(*args)


def _moe(x, h, logits, mod, w_gate, w_up, w_down, layer, need_ctx, final_g):
    sid_l, gate_l = _topk(logits[:, :, :T_L], CAP_L)
    xs_l, gs_l = _gather(h, sid_l, gate_l, CAP_L, 0)
    if need_ctx:
        sid_c, gate_c = _topk(logits[:, :, T_L:], CAP_C)
        xs_c, gs_c = _gather(h, sid_c, gate_c, CAP_C, T_L // T_C)
        y_l, y_c = _ffn([xs_l, xs_c], [gs_l, gs_c], w_gate, w_up, w_down, layer)
    else:
        (y_l,) = _ffn([xs_l], [gs_l], w_gate, w_up, w_down, layer)
    x = _combine(x, sid_l, y_l, mod, CAP_L, 512, 0, 0, final_g)
    if need_ctx:
        x = _combine(x, sid_c, y_c, mod, CAP_C, T_C, T_L // T_C, 1)
    return x


OD_N = RWKV_IN + MLA_U


def _odd_weights(w_in, partner):
    kr = RWKV_IN + MLA_Q_RANK + MLA_KV_RANK
    pad = np.full((128 - MLA_ROPE,), -1)
    index = np.concatenate([np.arange(kr), kr + np.arange(MLA_ROPE), pad, kr + partner, pad])
    return _gather_columns(w_in, index)


def _inproj_odd_kernel(x_ref, mod_ref, g_ref, w_ref, ur_ref, um_ref):
    mod = mod_ref[0]
    h = _modulate(x_ref[0], g_ref[...], mod[0:1], mod[1:2])
    y = _dot(h.astype(BF16), w_ref[...])
    ur_ref[0] = y[:, :RWKV_IN]
    um_ref[0] = y[:, RWKV_IN:]


def _inproj_odd(x, mod, g, w):
    B = x.shape[0]
    tok = lambda n: pl.BlockSpec((1, TM, n), lambda b, t: (b, t, 0))
    return pl.pallas_call(
        _inproj_odd_kernel,
        grid=(B, N_TILES),
        in_specs=[tok(D), _mod_spec(), pl.BlockSpec((1, D), lambda b, t: (0, 0)),
                  pl.BlockSpec((D, OD_N), lambda b, t: (0, 0))],
        out_specs=[tok(RWKV_IN), tok(MLA_U)],
        out_shape=[jax.ShapeDtypeStruct((B, NT, RWKV_IN), F32),
                   jax.ShapeDtypeStruct((B, NT, MLA_U), F32)],
        compiler_params=_cp("parallel", "parallel"),
        name="inproj_odd",
    )(x, mod, g.reshape(1, D), w)


def _head_ones():
    h = np.arange(RWKV_WIDTH) // RWKV_HEAD
    return jnp.asarray((h[:, None] == h[None, :]).astype(np.float32)).astype(BF16)


def _headsum(x, bd):
    return _dot2(x, bd)


def _rwkv_prep_kernel(u_ref, up_ref, un_ref, mup_ref, mun_ref, w0_ref, w2_ref, a0_ref, a2_ref, g2_ref,
                      kk_ref, ka_ref, bd_ref,
                      r_ref, v_ref, kkn_ref, g_ref, lw_ref, bq_ref, kd_ref):
    t = pl.program_id(1)
    u = u_ref[0]
    rows = lax.broadcasted_iota(I32, (TM, 1), 0)
    is_ctx = t == N_LAT_TILES
    seg_start = (t == 0) | is_ctx
    seg_end = (t == N_LAT_TILES - 1) | is_ctx
    last_row = jnp.where(is_ctx, T_C - 1, TM - 1)
    prev_row = jnp.where(seg_start, 0.0, up_ref[0, 7:8, :])
    next_row = jnp.where(seg_end, 0.0, un_ref[0, 0:1, :])
    prev = jnp.where(rows == 0, prev_row, pltpu.roll(u, 1, 0))
    nxt = jnp.where(rows == last_row, next_row, pltpu.roll(u, TM - 1, 0))
    us = u + mup_ref[...] * (prev - u) + mun_ref[...] * (nxt - u)
    W = RWKV_WIDTH
    r = us[:, :W]
    k = us[:, W:2 * W]
    v = us[:, 2 * W:3 * W]
    wl = jnp.tanh(us[:, 3 * W:3 * W + 128]).astype(BF16)
    al = us[:, 3 * W + 128:3 * W + 256].astype(BF16)
    gl = jax.nn.sigmoid(us[:, 3 * W + 256:3 * W + 384]).astype(BF16)
    bd = bd_ref[...]
    kk = k * kk_ref[...]
    kk = kk * lax.rsqrt(jnp.maximum(_headsum(kk * kk, bd), 1e-24))
    r_ref[0] = r.astype(BF16)
    v_ref[0] = v.astype(BF16)
    kkn_ref[0] = kk.astype(BF16)
    g_ref[0] = _dot(gl, g2_ref[...]).astype(BF16)
    for d in range(2):
        z = w0_ref[d:d + 1, :] + _dot(wl, w2_ref[d])
        a = jax.nn.sigmoid(a0_ref[d:d + 1, :] + _dot(al, a2_ref[d]))
        lw_ref[d, 0] = jax.nn.sigmoid(z) * (-float(np.exp(-0.5)))
        bq_ref[d, 0] = (kk * a).astype(BF16)
        kd_ref[d, 0] = (k * (1.0 + (a - 1.0) * ka_ref[...])).astype(BF16)


def _rwkv_prep(ur, mu_prev, mu_next, w0, w2, a0, a2, g2, k_k, k_a, bd):
    B = ur.shape[0]
    W = RWKV_WIDTH
    z = jnp.zeros((64, W), F32)
    w2p = jnp.stack([jnp.concatenate([w2[0], z]), jnp.concatenate([z, w2[1]])]).astype(BF16)
    a2p = jnp.stack([jnp.concatenate([a2[0], z]), jnp.concatenate([z, a2[1]])]).astype(BF16)
    tok = lambda n: pl.BlockSpec((1, TM, n), lambda b, t: (b, t, 0))
    tok2 = pl.BlockSpec((2, 1, TM, W), lambda b, t: (0, b, t, 0))
    row = lambda n: pl.BlockSpec((1, n), lambda b, t: (0, 0))
    full = lambda *s: pl.BlockSpec(s, lambda b, t: (0,) * len(s))
    slab = TM // 8
    one = lambda: jax.ShapeDtypeStruct((B, NT, W), BF16)
    per_dir = lambda dt: jax.ShapeDtypeStruct((2, B, NT, W), dt)
    return pl.pallas_call(
        _rwkv_prep_kernel,
        grid=(B, N_TILES),
        in_specs=[tok(RWKV_IN),
                  pl.BlockSpec((1, 8, RWKV_IN), lambda b, t: (b, jnp.maximum(t * slab - 1, 0), 0)),
                  pl.BlockSpec((1, 8, RWKV_IN), lambda b, t: (b, jnp.minimum((t + 1) * slab, NT // 8 - 1), 0)),
                  row(RWKV_IN), row(RWKV_IN), full(2, W), full(2, 128, W), full(2, W), full(2, 128, W),
                  full(128, W), row(W), row(W), full(W, W)],
        out_specs=[tok(W), tok(W), tok(W), tok(W), tok2, tok2, tok2],
        out_shape=[one(), one(), one(), one(), per_dir(F32), per_dir(BF16), per_dir(BF16)],
        compiler_params=_cp("parallel", "parallel"),
        name="rwkv_prep",
    )(ur, ur, ur, mu_prev.reshape(1, RWKV_IN), mu_next.reshape(1, RWKV_IN), w0, w2p, a0, a2p,
      g2.astype(BF16), k_k.reshape(1, W), k_a.reshape(1, W), bd)


def _wkv_kernel(r_ref, v_ref, kk_ref, lw_ref, bq_ref, kd_ref, o_ref, s_ref, *, nb):
    d = pl.program_id(0)
    i = pl.program_id(2)
    C = WKV_CHUNK
    P = RWKV_HEADS // 2
    G = nb * P
    HD = RWKV_HEAD

    @pl.when(i == 0)
    def _():
        s_ref[...] = jnp.zeros_like(s_ref)

    sgn = 1 - 2 * d
    rel = (lax.broadcasted_iota(I32, (C, C), 0) - lax.broadcasted_iota(I32, (C, C), 1)) * sgn
    incl_b = (rel >= 0).astype(BF16)
    row_w = lax.broadcasted_iota(I32, (C, 2 * HD), 0)
    col_w = lax.broadcasted_iota(I32, (C, 2 * HD), 1) & (HD - 1)
    rel_w = (row_w - col_w) * sgn
    incl = rel_w >= 0
    strict = rel_w > 0
    eye = (rel_w == 0).astype(F32)
    lo = lax.broadcasted_iota(I32, (C, 2 * HD), 1) < HD
    same_head = ((lax.broadcasted_iota(I32, (2 * HD, 2 * HD), 0) < HD)
                 == (lax.broadcasted_iota(I32, (2 * HD, 2 * HD), 1) < HD))

    def same_block(k):
        return (row_w >> k) == (col_w >> k)

    def pair(y):
        zero = jnp.zeros_like(y)
        return jnp.concatenate([jnp.where(lo, y, zero), jnp.where(lo, zero, y)], axis=0)

    at, rt, bt, kt, a0, r0, bh, kh, v, p_end = ([] for _ in range(10))
    for b in range(nb):
        lw = lw_ref[0, b]
        lw_hi, lw_lo = _split_bf16(lw)
        ci = _dot(incl_b, lw_hi) + _dot(incl_b, lw_lo)
        ce = ci - lw
        tot = jnp.sum(lw, axis=0, keepdims=True)
        m = 0.5 * tot
        r = r_ref[b].astype(F32)
        kk = kk_ref[b].astype(F32)
        bq = bq_ref[0, b].astype(F32)
        kd = kd_ref[0, b].astype(F32)
        e_ci = jnp.exp(ci - m)
        e_mci = jnp.exp(m - ci)
        e_end = jnp.exp(tot - ci)
        full = ((at, (-kk * jnp.exp(ce - m)).astype(BF16)), (rt, (r * e_ci).astype(BF16)),
                (bt, (bq * e_mci).astype(BF16)), (kt, (kd * e_mci).astype(BF16)),
                (a0, (-kk * jnp.exp(ce)).astype(BF16)), (r0, (r * (e_ci * jnp.exp(m))).astype(BF16)),
                (bh, (bq * e_end).astype(BF16)), (kh, (kd * e_end).astype(BF16)),
                (v, v_ref[b]), (p_end, jnp.exp(tot)))
        for dst, x in full:
            dst.extend(x[:, p * 2 * HD:(p + 1) * 2 * HD] for p in range(P))

    S = [s_ref[g] for g in range(G)]
    s_b = [s.astype(BF16) for s in S]
    left = [jnp.concatenate([at[g], rt[g]], axis=0) for g in range(G)]
    nb_ = [_dot_nt(left[g], pair(bt[g])) for g in range(G)]
    nk_ = [_dot_nt(left[g], pair(kt[g])) for g in range(G)]
    n_ab = [jnp.where(strict, x[:C], 0.0) for x in nb_]
    a_rb = [jnp.where(incl, x[C:], 0.0).astype(BF16) for x in nb_]
    mask_k = jnp.concatenate([strict, incl], axis=0)
    a_k = [jnp.where(mask_k, x, 0.0).astype(BF16) for x in nk_]
    from_state = [_dot_nt(jnp.concatenate([a0[g], r0[g]], axis=0), s_b[g]) for g in range(G)]
    from_v = [_dot(a_k[g], pair(v[g])) for g in range(G)]
    w_rhs = [from_state[g][:C] + from_v[g][:C] for g in range(G)]
    inv = [eye + jnp.where(same_block(1), n, 0.0) for n in n_ab]
    for k in range(1, 6):
        off_mask = same_block(k + 1) & jnp.logical_not(same_block(k))
        inv_b = [x.astype(BF16) for x in inv]
        e = [_dot(jnp.where(off_mask, n, 0.0).astype(BF16), pair(x)) for n, x in zip(n_ab, inv_b)]
        inv = [x + _dot(xb, pair(y.astype(BF16))) for x, xb, y in zip(inv, inv_b, e)]
    U = [_dot(inv[g].astype(BF16), pair(w_rhs[g].astype(BF16))).astype(BF16) for g in range(G)]
    for g in range(G):
        o = from_state[g][C:] + _dot(a_rb[g], pair(U[g])) + from_v[g][C:]
        o_ref[0, g // P, :, (g % P) * 2 * HD:(g % P + 1) * 2 * HD] = o
    for g in range(G):
        upd = _dot_tn(jnp.concatenate([U[g], v[g]], axis=0), jnp.concatenate([bh[g], kh[g]], axis=0))
        s_ref[g] = S[g] * p_end[g] + jnp.where(same_head, upd, 0.0)


WKV_BATCH = 4


def _wkv(r, v, kk, lw, bq, kd, nb=WKV_BATCH):
    B = r.shape[0]
    C = WKV_CHUNK
    n_lat = T_L // C

    def chunk(d, i):
        fwd = jnp.where(i < N_CTX_CHUNKS, n_lat + i, i - N_CTX_CHUNKS)
        return jnp.where(d == 0, fwd, N_CHUNKS - 1 - i)

    shared = pl.BlockSpec((nb, C, RWKV_WIDTH), lambda d, b, i: (b, chunk(d, i), 0))
    perdir = pl.BlockSpec((1, nb, C, RWKV_WIDTH), lambda d, b, i: (d, b, chunk(d, i), 0))
    return pl.pallas_call(
        functools.partial(_wkv_kernel, nb=nb),
        grid=(2, B // nb, N_CHUNKS),
        in_specs=[shared, shared, shared, perdir, perdir, perdir],
        out_specs=perdir,
        out_shape=jax.ShapeDtypeStruct((2, B, NT, RWKV_WIDTH), F32),
        scratch_shapes=[pltpu.VMEM((nb * RWKV_HEADS // 2, 2 * RWKV_HEAD, 2 * RWKV_HEAD), F32)],
        compiler_params=_cp("parallel", "parallel", "arbitrary"),
        name="wkv7",
    )(r, v, kk, lw, bq, kd)


def _rwkv_finish_kernel(o_ref, r_ref, v_ref, g_ref, kd_ref, rk_ref, lg_ref, lb_ref, bd_ref, y_ref):
    bd = bd_ref[...]
    o = o_ref[0, 0] + o_ref[1, 0]
    mu = _headsum(o, bd) * (1.0 / RWKV_HEAD)
    xc = o - mu
    var = _headsum(xc * xc, bd) * (1.0 / RWKV_HEAD)
    on = xc * lax.rsqrt(var + LNX_EPS) * lg_ref[...] + lb_ref[...]
    k_sum = kd_ref[0, 0].astype(F32) + kd_ref[1, 0].astype(F32)
    bonus = _headsum(r_ref[0].astype(F32) * k_sum * rk_ref[...], bd) * v_ref[0].astype(F32)
    y_ref[0] = ((on + bonus) * g_ref[0].astype(F32)).astype(y_ref.dtype)


def _rwkv_finish(o, r, v, g, kd, r_k, lnx_g, lnx_b, bd, need_ctx):
    B = r.shape[0]
    W = RWKV_WIDTH
    tok = pl.BlockSpec((1, TM, W), lambda b, t: (b, t, 0))
    tok2 = pl.BlockSpec((2, 1, TM, W), lambda b, t: (0, b, t, 0))
    row = pl.BlockSpec((1, W), lambda b, t: (0, 0))
    return pl.pallas_call(
        _rwkv_finish_kernel,
        grid=(B, N_TILES if need_ctx else N_LAT_TILES),
        in_specs=[tok2, tok, tok, tok, tok2, row, row, row, pl.BlockSpec((W, W), lambda b, t: (0, 0))],
        out_specs=tok,
        out_shape=jax.ShapeDtypeStruct((B, NT, W), BF16),
        compiler_params=_cp("parallel", "parallel"),
        name="rwkv_finish",
    )(o, r, v, g, kd, r_k.reshape(1, W), lnx_g.reshape(1, W), lnx_b.reshape(1, W), bd)


MLA_QW = MLA_HEADS * MLA_SLOT


def _mla_weights(w_uq, w_ukv, partner):
    dq = MLA_NOPE + MLA_ROPE
    dkv = MLA_NOPE + MLA_V
    iq = np.full((2 * MLA_QW,), -1)
    ik = np.full((MLA_QW,), -1)
    iv = np.zeros((MLA_HEADS * MLA_V,), np.int64)
    place = np.zeros((128, MLA_QW), np.float32)
    for h in range(MLA_HEADS):
        iq[h * MLA_SLOT:h * MLA_SLOT + dq] = h * dq + np.arange(dq)
        iq[MLA_QW + h * MLA_SLOT + MLA_NOPE:MLA_QW + h * MLA_SLOT + dq] = h * dq + MLA_NOPE + partner
        ik[h * MLA_SLOT:h * MLA_SLOT + MLA_NOPE] = h * dkv + np.arange(MLA_NOPE)
        iv[h * MLA_V:(h + 1) * MLA_V] = h * dkv + MLA_NOPE + np.arange(MLA_V)
        place[np.arange(MLA_ROPE), h * MLA_SLOT + MLA_NOPE + np.arange(MLA_ROPE)] = 1.0
    return (_gather_columns(w_uq, iq), _gather_columns(w_ukv, ik), _gather_columns(w_ukv, iv),
            jnp.asarray(place).astype(BF16))


def _mla_proj_kernel(u_ref, qg_ref, kg_ref, wq_ref, wk_ref, wv_ref, pl_ref, cq_ref, sq_ref, ck_ref, sk_ref,
                     q_ref, k_ref, v_ref):
    u = u_ref[0]
    qn = _rmsnorm(u[:, :MLA_Q_RANK], qg_ref[...]).astype(BF16)
    kvn = _rmsnorm(u[:, MLA_Q_RANK:MLA_Q_RANK + MLA_KV_RANK], kg_ref[...]).astype(BF16)
    o = MLA_Q_RANK + MLA_KV_RANK
    kr = u[:, o:o + 128] * ck_ref[...] + u[:, o + 128:o + 256] * sk_ref[...]
    qq = _dot(qn, wq_ref[...])
    q_ref[0] = (qq[:, :MLA_QW] * cq_ref[...] + qq[:, MLA_QW:] * sq_ref[...]).astype(BF16)
    k_ref[0] = (_dot(kvn, wk_ref[...]) + _dot2(kr, pl_ref[...])).astype(BF16)
    v_ref[0] = _dot(kvn, wv_ref[...]).astype(BF16)


def _mla_proj(um, q_norm_g, kv_norm_g, weights, tables):
    B = um.shape[0]
    wq, wk, wv, place = weights
    cq, sq, ck, sk = tables
    tok = lambda n: pl.BlockSpec((1, TM, n), lambda b, t: (b, t, 0))
    tab = lambda n: pl.BlockSpec((TM, n), lambda b, t: (t, 0))
    full = lambda a: pl.BlockSpec(a.shape, lambda b, t: (0, 0))
    return pl.pallas_call(
        _mla_proj_kernel,
        grid=(B, N_TILES),
        in_specs=[tok(MLA_U), pl.BlockSpec((1, MLA_Q_RANK), lambda b, t: (0, 0)),
                  pl.BlockSpec((1, MLA_KV_RANK), lambda b, t: (0, 0)),
                  full(wq), full(wk), full(wv), full(place),
                  tab(MLA_QW), tab(MLA_QW), tab(MLA_SLOT), tab(MLA_SLOT)],
        out_specs=[tok(MLA_QW), tok(MLA_QW), tok(MLA_HEADS * MLA_V)],
        out_shape=[jax.ShapeDtypeStruct((B, NT, MLA_QW), BF16),
                   jax.ShapeDtypeStruct((B, NT, MLA_QW), BF16),
                   jax.ShapeDtypeStruct((B, NT, MLA_HEADS * MLA_V), BF16)],
        compiler_params=_cp("parallel", "parallel"),
        name="mla_proj",
    )(um, q_norm_g.reshape(1, -1), kv_norm_g.reshape(1, -1), wq, wk, wv, place, cq, sq, ck, sk)


MLA_TQ = 512


def _mla_attn_kernel(q_ref, k_ref, v_ref, o_ref):
    t = pl.program_id(2)
    is_ctx = t == T_L // MLA_TQ
    lo_lanes = lax.broadcasted_iota(I32, (MLA_TQ, 128), 1) < MLA_V

    def attend(keys):
        n_keys = keys.stop - keys.start
        lo_rows = lax.broadcasted_iota(I32, (n_keys, 128), 1) < MLA_V
        vall = v_ref[0, keys, :]
        one = jnp.ones((n_keys, 128), vall.dtype)
        vh = [jnp.where(lo_rows, vall, one), jnp.where(lo_rows, one, vall)]
        s = [_dot_nt(q_ref[0, :, hh * MLA_SLOT:(hh + 1) * MLA_SLOT], k_ref[0, keys, hh * MLA_SLOT:(hh + 1) * MLA_SLOT])
             for hh in range(2)]
        e = [jnp.exp(x - jnp.max(x, axis=-1, keepdims=True)).astype(BF16) for x in s]
        r = [_dot(e[hh], vh[hh]) for hh in range(2)]
        num = jnp.where(lo_lanes, r[0], r[1])
        den = pltpu.roll(jnp.where(lo_lanes, r[1], r[0]), MLA_V, 1)
        o_ref[0] = (num * (1.0 / den)).astype(o_ref.dtype)

    @pl.when(jnp.logical_not(is_ctx))
    def _():
        attend(slice(0, NT))

    @pl.when(is_ctx)
    def _():
        attend(slice(T_L, NT))


def _mla_attn(q, k, v, need_ctx):
    B = q.shape[0]
    n_tiles = pl.cdiv(NT, MLA_TQ) if need_ctx else T_L // MLA_TQ
    return pl.pallas_call(
        _mla_attn_kernel,
        grid=(B, MLA_HEADS // 2, n_tiles),
        in_specs=[pl.BlockSpec((1, MLA_TQ, 2 * MLA_SLOT), lambda b, p, t: (b, t, p)),
                  pl.BlockSpec((1, NT, 2 * MLA_SLOT), lambda b, p, t: (b, 0, p)),
                  pl.BlockSpec((1, NT, 2 * MLA_V), lambda b, p, t: (b, 0, p))],
        out_specs=pl.BlockSpec((1, MLA_TQ, 2 * MLA_V), lambda b, p, t: (b, t, p)),
        out_shape=jax.ShapeDtypeStruct((B, NT, MLA_HEADS * MLA_V), BF16),
        compiler_params=_cp("parallel", "parallel", "parallel"),
        name="mla_attn",
    )(q, k, v)


def kernel(x, c, ctx, c_ctx, ada_w, ada_b, norm_mix_g, norm_ffn_g, router_w, exp_w_gate, exp_w_up, exp_w_down, ev_w_in, ev_w_out, ev_sink, ev_pool_w, ev_pool_scale, od_w_in, od_w_out, od_mu_prev, od_mu_next, od_w0, od_w2, od_a0, od_a2, od_g2, od_k_k, od_k_a, od_r_k, od_lnx_g, od_lnx_b, od_q_norm_g, od_w_uq, od_kv_norm_g, od_w_ukv, final_norm_g):
    depth = ada_w.shape[0]
    assert x.shape[1:] == (T_L, D) and ctx.shape[1:] == (T_C, D)
    mods = _modvec(c, c_ctx, ada_w, ada_b)
    ev_tables, ev_partner = _even_tables()
    mla_tables, mla_partner = _mla_tables()
    bd = _head_ones()
    xs = jnp.concatenate([x, ctx], axis=1)
    for l in range(depth):
        need_ctx = l < depth - 1
        i = l // 2
        mod = mods[l]
        if l % 2 == 0:
            q, kd, vd, pu = _inproj_even(xs, mod, norm_mix_g[l], _even_weights(ev_w_in[i], ev_partner), ev_tables)
            mix_a = _attn_even(q, kd, vd, ev_sink[i])
            mix_b = _pool(pu, ev_pool_w[i], ev_pool_scale[i])
            w_out = ev_w_out[i]
        else:
            ur, um = _inproj_odd(xs, mod, norm_mix_g[l], _odd_weights(od_w_in[i], mla_partner))
            r, v, kk, g, lw, bq, kdir = _rwkv_prep(ur, od_mu_prev[i], od_mu_next[i], od_w0[i], od_w2[i],
                                                   od_a0[i], od_a2[i], od_g2[i], od_k_k[i], od_k_a[i], bd)
            o = _wkv(r, v, kk, lw, bq, kdir)
            mix_a = _rwkv_finish(o, r, v, g, kdir, od_r_k[i], od_lnx_g[i], od_lnx_b[i], bd, need_ctx)
            mq, mk, mv = _mla_proj(um, od_q_norm_g[i], od_kv_norm_g[i],
                                   _mla_weights(od_w_uq[i], od_w_ukv[i], mla_partner), mla_tables)
            mix_b = _mla_attn(mq, mk, mv, need_ctx)
            w_out = od_w_out[i]
        xs, h, logits = _outproj(mix_a, mix_b, w_out, xs, mod, norm_ffn_g[l], router_w[l], need_ctx)
        xs = _moe(xs, h, logits, mod, exp_w_gate, exp_w_up, exp_w_down, l, need_ctx,
                  None if need_ctx else final_norm_g)
    return xs
```

```python
import functools

import numpy as np
import jax
import jax.numpy as jnp
from jax import lax
from jax.experimental import pallas as pl
from jax.experimental.pallas import tpu as pltpu

F32 = jnp.float32
BF16 = jnp.bfloat16
I32 = jnp.int32

D = 1024
T_L = 2048
T_C = 256
NT = T_L + T_C
GRID_W = 64
EPS = 1e-6
ROPE_BASE = 10000.0

TM = 512
N_LAT_TILES = T_L // TM
N_TILES = N_LAT_TILES + 1
assert T_C <= TM and T_L % TM == 0

A_HEADS = 8
A_KV_HEADS = 2
HEAD_DIM = 64
A_WINDOW = 128
A_BLOCK = 128
N_ABLK = NT // A_BLOCK
N_LAT_ABLK = T_L // A_BLOCK

POOL_WINDOWS = (2, 4, 8, 16)
POOL_GROUP = 128

RWKV_HEADS = 8
RWKV_HEAD = 64
RWKV_WIDTH = 512
RWKV_IN = 1920
LNX_EPS = 64e-5
WKV_CHUNK = 64
N_CHUNKS = NT // WKV_CHUNK
N_CTX_CHUNKS = T_C // WKV_CHUNK

MLA_HEADS = 8
MLA_Q_RANK = 256
MLA_KV_RANK = 128
MLA_NOPE = 64
MLA_ROPE = 32
MLA_V = 64
MLA_SLOT = 128
MLA_U = 640

N_EXPERTS = 16
EXPERT_FF = 2048
CAP_L = 2 * T_L // N_EXPERTS
CAP_C = 2 * T_C // N_EXPERTS
FF_CHUNK = 512

VMEM_LIMIT_BYTES = 56 * 1024 * 1024


def _cp(*sem):
    return pltpu.CompilerParams(dimension_semantics=sem, vmem_limit_bytes=VMEM_LIMIT_BYTES)


def _split_bf16(x):
    hi = x.astype(BF16)
    lo = (x - hi.astype(F32)).astype(BF16)
    return hi, lo


def _dot(a, b):
    return jnp.dot(a, b, preferred_element_type=F32)


def _dot_nt(a, b):
    return lax.dot_general(a, b, (((1,), (1,)), ((), ())), preferred_element_type=F32)


def _dot_tn(a, b):
    return lax.dot_general(a, b, (((0,), (0,)), ((), ())), preferred_element_type=F32)


def _dot2(a_f32, b_bf16):
    hi, lo = _split_bf16(a_f32)
    return _dot(hi, b_bf16) + _dot(lo, b_bf16)


def _modulate(x, g, shift, scale):
    ms = jnp.mean(x * x, axis=-1, keepdims=True)
    return x * lax.rsqrt(ms + EPS) * g * (1.0 + scale) + shift


def _rmsnorm(x, g):
    ms = jnp.mean(x * x, axis=-1, keepdims=True)
    return x * lax.rsqrt(ms + EPS) * g


def _modvec_kernel(a_ref, w_ref, b_ref, o_ref):
    a = a_ref[...]
    a = a * jax.nn.sigmoid(a)
    a_hi, a_lo = _split_bf16(a)
    w_hi, w_lo = _split_bf16(w_ref[0])
    o_ref[0] = _dot(a_hi, w_hi) + _dot(a_lo, w_hi) + _dot(a_hi, w_lo) + b_ref[0]


def _modvec(c, c_ctx, ada_w, ada_b):
    L = ada_w.shape[0]
    B = c.shape[0]
    assert B < 16
    rows = jnp.zeros((16, D), F32).at[:B].set(c).at[B].set(c_ctx)
    tn = 512
    out = pl.pallas_call(
        _modvec_kernel,
        grid=(L, 6 * D // tn),
        in_specs=[pl.BlockSpec((16, D), lambda l, n: (0, 0)),
                  pl.BlockSpec((1, D, tn), lambda l, n: (l, 0, n)),
                  pl.BlockSpec((1, 1, tn), lambda l, n: (l, 0, n))],
        out_specs=pl.BlockSpec((1, 16, tn), lambda l, n: (l, 0, n)),
        out_shape=jax.ShapeDtypeStruct((L, 16, 6 * D), F32),
        compiler_params=_cp("parallel", "parallel"),
        name="modvec",
    )(rows, ada_w, ada_b.reshape(L, 1, 6 * D))
    m = out.reshape(L, 16, 6, D)
    ml = m[:, :B]
    mc = jnp.broadcast_to(m[:, B:B + 1], (L, B, 6, D))
    mod = jnp.stack([ml, mc], axis=2)
    mod = jnp.pad(mod, ((0, 0), (0, 0), (0, 0), (0, 2), (0, 0)))
    return mod.reshape(L, B * 2, 8, D)


def _mod_spec():
    return pl.BlockSpec((1, 8, D), lambda b, t: (b * 2 + t // N_LAT_TILES, 0, 0))


def _rope_block(width, pos_row, pos_col):
    d = width // 2
    half = d // 2
    inv = ROPE_BASE ** (-np.arange(half, dtype=np.float32) / half)
    cos = np.zeros((pos_row.shape[0], width), np.float32)
    sin = np.zeros_like(cos)
    partner = np.zeros((width,), np.int64)
    for part, pos in ((0, pos_row), (1, pos_col)):
        ang = pos.astype(np.float32)[:, None] * inv[None, :]
        c, s = np.cos(ang), np.sin(ang)
        o = part * d
        cos[:, o:o + half] = c
        cos[:, o + half:o + d] = c
        sin[:, o:o + half] = -s
        sin[:, o + half:o + d] = s
        partner[o:o + half] = np.arange(o + half, o + d)
        partner[o + half:o + d] = np.arange(o, o + half)
    return cos, sin, partner


def _latent_positions():
    t = np.arange(T_L)
    return t // GRID_W, t % GRID_W


def _even_tables():
    cos, sin, partner = _rope_block(HEAD_DIM, *_latent_positions())
    scale = HEAD_DIM ** -0.5
    ones = np.ones((T_C, HEAD_DIM), np.float32)
    zeros = np.zeros((T_C, HEAD_DIM), np.float32)
    cos = np.concatenate([cos, ones], axis=0)
    sin = np.concatenate([sin, zeros], axis=0)
    cq = np.tile(cos, (1, A_HEADS)) * scale
    sq = np.tile(sin, (1, A_HEADS)) * scale
    ck = np.tile(cos, (1, 2 * A_KV_HEADS))
    sk = np.tile(sin, (1, 2 * A_KV_HEADS))
    return (jnp.asarray(cq), jnp.asarray(sq), jnp.asarray(ck), jnp.asarray(sk)), partner


def _mla_tables():
    cos, sin, partner = _rope_block(MLA_ROPE, *_latent_positions())
    cos = np.concatenate([cos, np.ones((T_C, MLA_ROPE), np.float32)], axis=0)
    sin = np.concatenate([sin, np.zeros((T_C, MLA_ROPE), np.float32)], axis=0)
    scale = (MLA_NOPE + MLA_ROPE) ** -0.5
    cq1 = np.zeros((NT, MLA_SLOT), np.float32)
    sq1 = np.zeros((NT, MLA_SLOT), np.float32)
    cq1[:, :MLA_NOPE] = 1.0
    cq1[:, MLA_NOPE:MLA_NOPE + MLA_ROPE] = cos
    sq1[:, MLA_NOPE:MLA_NOPE + MLA_ROPE] = sin
    cq = np.tile(cq1, (1, MLA_HEADS)) * scale
    sq = np.tile(sq1, (1, MLA_HEADS)) * scale
    ck = np.zeros((NT, MLA_SLOT), np.float32)
    sk = np.zeros((NT, MLA_SLOT), np.float32)
    ck[:, :MLA_ROPE] = cos
    sk[:, :MLA_ROPE] = sin
    return (jnp.asarray(cq), jnp.asarray(sq), jnp.asarray(ck), jnp.asarray(sk)), partner


EV_Q, EV_K, EV_V, EV_P = 512, 256, 256, 512
EV_N = 2 * EV_Q + 2 * EV_K + EV_V + EV_P


def _gather_columns(w, index):
    w0 = jnp.concatenate([w, jnp.zeros((w.shape[0], 1), w.dtype)], axis=1).astype(BF16)
    return w0[:, np.where(index < 0, w.shape[1], index)]


def _even_weights(w_in, partner):
    pq = np.concatenate([h * HEAD_DIM + partner for h in range(A_HEADS)])
    pk = np.concatenate([h * HEAD_DIM + partner for h in range(A_KV_HEADS)])
    dup = np.concatenate([np.arange(HEAD_DIM), np.arange(HEAD_DIM),
                          HEAD_DIM + np.arange(HEAD_DIM), HEAD_DIM + np.arange(HEAD_DIM)])
    index = np.concatenate([np.arange(512), pq, 512 + dup, 512 + pk[dup], 640 + dup, 768 + np.arange(512)])
    return _gather_columns(w_in, index)


def _inproj_even_kernel(x_ref, mod_ref, g_ref, w_ref, cq_ref, sq_ref, ck_ref, sk_ref,
                        q_ref, k_ref, v_ref, p_ref):
    mod = mod_ref[0]
    h = _modulate(x_ref[0], g_ref[...], mod[0:1], mod[1:2])
    y = _dot(h.astype(BF16), w_ref[...])
    o = 0
    q = y[:, o:o + EV_Q] * cq_ref[...] + y[:, o + EV_Q:o + 2 * EV_Q] * sq_ref[...]
    o += 2 * EV_Q
    k = y[:, o:o + EV_K] * ck_ref[...] + y[:, o + EV_K:o + 2 * EV_K] * sk_ref[...]
    o += 2 * EV_K
    q_ref[0] = q.astype(BF16)
    k_ref[0] = k.astype(BF16)
    v_ref[0] = y[:, o:o + EV_V].astype(BF16)
    p_ref[0] = y[:, o + EV_V:o + EV_V + EV_P]


def _inproj_even(x, mod, g, w, tables):
    B = x.shape[0]
    cq, sq, ck, sk = tables
    tok = lambda n: pl.BlockSpec((1, TM, n), lambda b, t: (b, t, 0))
    tab = lambda n: pl.BlockSpec((TM, n), lambda b, t: (t, 0))
    return pl.pallas_call(
        _inproj_even_kernel,
        grid=(B, N_TILES),
        in_specs=[tok(D), _mod_spec(), pl.BlockSpec((1, D), lambda b, t: (0, 0)),
                  pl.BlockSpec((D, EV_N), lambda b, t: (0, 0)),
                  tab(EV_Q), tab(EV_Q), tab(EV_K), tab(EV_K)],
        out_specs=[tok(EV_Q), tok(EV_K), tok(EV_V), tok(EV_P)],
        out_shape=[jax.ShapeDtypeStruct((B, NT, EV_Q), BF16),
                   jax.ShapeDtypeStruct((B, NT, EV_K), BF16),
                   jax.ShapeDtypeStruct((B, NT, EV_V), BF16),
                   jax.ShapeDtypeStruct((B, NT, EV_P), F32)],
        compiler_params=_cp("parallel", "parallel"),
        name="inproj_even",
    )(x, mod, g.reshape(1, D), w, cq, sq, ck, sk)


A_STEP = 2


def _attn_even_kernel(sink_ref, q_ref, kp_ref, kc_ref, kn_ref, kx_ref,
                      vp_ref, vc_ref, vn_ref, vx_ref, o_ref):
    j = pl.program_id(1)
    is_lat = j < N_LAT_ABLK // A_STEP
    lo_lanes = lax.broadcasted_iota(I32, (A_BLOCK, 128), 1) < HEAD_DIM
    group = A_HEADS // A_KV_HEADS
    heads = range(A_HEADS)
    qm, kall, vall, valid = [], [], [], []
    for sb in range(A_STEP):
        jb = j * A_STEP + sb
        rows = slice(sb * A_BLOCK, (sb + 1) * A_BLOCK)
        qpos = jb * A_BLOCK + lax.broadcasted_iota(I32, (A_BLOCK, 3 * A_BLOCK), 0)
        kpos = (jb - 1) * A_BLOCK + lax.broadcasted_iota(I32, (A_BLOCK, 3 * A_BLOCK), 1)
        valid.append((jnp.abs(kpos - qpos) <= A_WINDOW) & (kpos >= 0) & (kpos < T_L) & is_lat)
        k_sb, v_sb = [], []
        for kv in range(A_KV_HEADS):
            ks = slice(kv * 128, (kv + 1) * 128)
            kblocks = [kp_ref[0, :, ks]] + [kc_ref[0, i * A_BLOCK:(i + 1) * A_BLOCK, ks] for i in range(A_STEP)] \
                + [kn_ref[0, :, ks]]
            vblocks = [vp_ref[0, :, ks]] + [vc_ref[0, i * A_BLOCK:(i + 1) * A_BLOCK, ks] for i in range(A_STEP)] \
                + [vn_ref[0, :, ks]]
            k_sb.append(jnp.concatenate([kx_ref[0, :, ks]] + kblocks[sb:sb + 3], axis=0))
            v_sb.append(jnp.concatenate([vx_ref[0, :, ks]] + vblocks[sb:sb + 3], axis=0))
        kall.append(k_sb)
        vall.append(v_sb)
        q = q_ref[0, rows, :]
        qm.append([jnp.where(lo_lanes if h % 2 == 0 else jnp.logical_not(lo_lanes),
                             q[:, (h // 2) * 128:(h // 2 + 1) * 128], jnp.zeros((A_BLOCK, 128), q.dtype))
                   for h in heads])
    units = [(sb, h) for sb in range(A_STEP) for h in heads]
    s = [_dot_nt(qm[sb][h], kall[sb][h // group]) for sb, h in units]
    e, inv_den = [], []
    for (sb, h), sc in zip(units, s):
        snk = sink_ref[h]
        s_ctx = sc[:, :T_C]
        s_loc = jnp.where(valid[sb], sc[:, T_C:], -jnp.inf)
        m = jnp.maximum(jnp.max(s_ctx, axis=-1, keepdims=True), jnp.max(s_loc, axis=-1, keepdims=True))
        m = jnp.maximum(m, snk)
        e_ctx = jnp.exp(s_ctx - m)
        e_loc = jnp.exp(s_loc - m)
        den = (jnp.sum(e_ctx, axis=-1, keepdims=True) + jnp.sum(e_loc, axis=-1, keepdims=True)
               + jnp.exp(snk - m))
        e.append(jnp.concatenate([e_ctx, e_loc], axis=1).astype(BF16))
        inv_den.append(1.0 / den)
    o = [_dot(e[i], vall[sb][h // group]) * inv_den[i] for i, (sb, h) in enumerate(units)]
    for sb in range(A_STEP):
        for p in range(A_HEADS // 2):
            ev, od = o[sb * A_HEADS + 2 * p], o[sb * A_HEADS + 2 * p + 1]
            o_ref[0, sb * A_BLOCK:(sb + 1) * A_BLOCK, p * 128:(p + 1) * 128] = \
                jnp.where(lo_lanes, ev, od).astype(o_ref.dtype)


def _attn_even(q, kd, vd, sink):
    B = q.shape[0]
    last = N_LAT_ABLK - 1
    step = A_STEP * A_BLOCK
    one = lambda f: pl.BlockSpec((1, A_BLOCK, 256), f)
    prev = lambda b, j: (b, jnp.clip(j * A_STEP - 1, 0, last), 0)
    nxt = lambda b, j: (b, jnp.clip((j + 1) * A_STEP, 0, last), 0)
    cur = lambda n: pl.BlockSpec((1, step, n), lambda b, j: (b, j, 0))
    ctx = pl.BlockSpec((1, T_C, 256), lambda b, j: (b, T_L // T_C, 0))
    return pl.pallas_call(
        _attn_even_kernel,
        grid=(B, NT // step),
        in_specs=[pl.BlockSpec(memory_space=pltpu.SMEM),
                  cur(EV_Q),
                  one(prev), cur(256), one(nxt), ctx,
                  one(prev), cur(256), one(nxt), ctx],
        out_specs=cur(512),
        out_shape=jax.ShapeDtypeStruct((B, NT, 512), BF16),
        compiler_params=_cp("parallel", "parallel"),
        name="attn_even",
    )(sink, q, kd, kd, kd, kd, vd, vd, vd, vd)


POOL_BLOCK = 256
POOL_HALO = 8


def _pool_kernel(up_ref, uc_ref, un_ref, w_ref, sc_ref, o_ref):
    j = pl.program_id(1)
    R = POOL_BLOCK
    is_lat = j < T_L // R
    seg_lo = jnp.where(is_lat, 0, T_L)
    seg_hi = jnp.where(is_lat, T_L, NT)
    uc = uc_ref[0]
    halo = jnp.concatenate([up_ref[0], un_ref[0]], axis=0)
    r = j * R + lax.broadcasted_iota(I32, (R, R), 0)
    c = j * R + lax.broadcasted_iota(I32, (R, R), 1)
    rh = j * R + lax.broadcasted_iota(I32, (R, 2 * POOL_HALO), 0)
    ch = lax.broadcasted_iota(I32, (R, 2 * POOL_HALO), 1)
    ch = j * R + jnp.where(ch < POOL_HALO, ch - POOL_HALO, ch - POOL_HALO + R)
    r1 = j * R + lax.broadcasted_iota(I32, (R, 1), 0)
    hi_u, lo_u = _split_bf16(uc)
    hi_h, lo_h = _split_bf16(halo)
    groups = range(len(POOL_WINDOWS))
    cols = [slice(gi * POOL_GROUP, (gi + 1) * POOL_GROUP) for gi in groups]
    bands, bands_h, inv_cnt = [], [], []
    for w in POOL_WINDOWS:
        in_window = lambda rows, cc: ((cc >= jnp.maximum(rows - w // 2, seg_lo))
                                      & (cc < jnp.minimum(rows + w // 2, seg_hi))).astype(BF16)
        bands.append(in_window(r, c))
        bands_h.append(in_window(rh, ch))
        cnt = (jnp.minimum(r1 + w // 2, seg_hi) - jnp.maximum(r1 - w // 2, seg_lo)).astype(F32)
        inv_cnt.append(1.0 / cnt)
    sums = [_dot(bands[gi], jnp.concatenate([hi_u[:, cols[gi]], lo_u[:, cols[gi]]], axis=1))
            + _dot(bands_h[gi], jnp.concatenate([hi_h[:, cols[gi]], lo_h[:, cols[gi]]], axis=1)) for gi in groups]
    dev = [((s[:, :POOL_GROUP] + s[:, POOL_GROUP:]) * inv_cnt[gi] - uc[:, cols[gi]]).astype(BF16)
           for gi, s in enumerate(sums)]
    outs = [_dot(dev[gi], w_ref[gi]) for gi in groups]
    o_ref[0] = (jnp.concatenate(outs, axis=1) * sc_ref[...]).astype(o_ref.dtype)


def _pool(u, pool_w, pool_scale):
    B = u.shape[0]
    slabs = POOL_BLOCK // POOL_HALO
    slab = lambda f: pl.BlockSpec((1, POOL_HALO, 512), f)
    blk = pl.BlockSpec((1, POOL_BLOCK, 512), lambda b, j: (b, j, 0))
    return pl.pallas_call(
        _pool_kernel,
        grid=(B, NT // POOL_BLOCK),
        in_specs=[slab(lambda b, j: (b, jnp.maximum(j * slabs - 1, 0), 0)),
                  blk,
                  slab(lambda b, j: (b, jnp.minimum((j + 1) * slabs, NT // POOL_HALO - 1), 0)),
                  pl.BlockSpec((4, POOL_GROUP, POOL_GROUP), lambda b, j: (0, 0, 0)),
                  pl.BlockSpec((1, 512), lambda b, j: (0, 0))],
        out_specs=blk,
        out_shape=jax.ShapeDtypeStruct((B, NT, 512), BF16),
        compiler_params=_cp("parallel", "parallel"),
        name="pool",
    )(u, u, u, pool_w.astype(BF16), pool_scale.reshape(1, 512))


def _outproj_kernel(a_ref, p_ref, w_ref, x_ref, mod_ref, g_ref, wr_ref, xo_ref, h_ref, lg_ref):
    mod = mod_ref[0]
    y = _dot(a_ref[0], w_ref[0:512, :]) + _dot(p_ref[0], w_ref[512:1024, :])
    x = x_ref[0] + mod[2:3] * y
    xo_ref[0] = x
    h = _modulate(x, g_ref[...], mod[3:4], mod[4:5])
    w_hi, w_lo = _split_bf16(wr_ref[...])
    h_hi, h_lo = _split_bf16(h)
    h_ref[0] = h_hi
    lg_ref[0] = _dot_nt(w_hi, h_hi) + _dot_nt(w_lo, h_hi) + _dot_nt(w_hi, h_lo)


def _outproj(a, p, w_out, x, mod, g_ffn, w_router, need_ctx):
    B = x.shape[0]
    tok = lambda n: pl.BlockSpec((1, TM, n), lambda b, t: (b, t, 0))
    return pl.pallas_call(
        _outproj_kernel,
        grid=(B, N_TILES if need_ctx else N_LAT_TILES),
        in_specs=[tok(512), tok(512), pl.BlockSpec((D, D), lambda b, t: (0, 0)), tok(D), _mod_spec(),
                  pl.BlockSpec((1, D), lambda b, t: (0, 0)),
                  pl.BlockSpec((N_EXPERTS, D), lambda b, t: (0, 0))],
        out_specs=[tok(D), tok(D), pl.BlockSpec((1, N_EXPERTS, TM), lambda b, t: (b, 0, t))],
        out_shape=[jax.ShapeDtypeStruct((B, NT, D), F32),
                   jax.ShapeDtypeStruct((B, NT, D), BF16),
                   jax.ShapeDtypeStruct((B, N_EXPERTS, NT), F32)],
        compiler_params=_cp("parallel", "parallel"),
        name="outproj",
    )(a, p, w_out.astype(BF16), x, mod, g_ffn.reshape(1, D), w_router.T)


def _topk_kernel(lg_ref, sid_ref, gate_ref, *, cap):
    B, E, T = lg_ref.shape
    R = B * E
    lg = lg_ref[...]
    m = jnp.max(lg, axis=1, keepdims=True)
    e = jnp.exp(lg - m)
    aff = (e / jnp.sum(e, axis=1, keepdims=True)).reshape(R, T)
    bits = lax.bitcast_convert_type(aff, I32)

    def step(i, thr):
        cand = thr | jnp.left_shift(jnp.int32(1), 30 - i)
        cnt = jnp.sum((bits >= cand).astype(F32), axis=1, keepdims=True)
        return jnp.where(cnt >= cap, cand, thr)

    thr = lax.fori_loop(0, 31, step, jnp.zeros((R, 1), I32))
    gt = bits > thr
    eq = bits == thr
    need = cap - jnp.sum(gt.astype(F32), axis=1, keepdims=True)

    blk = 256
    tri = (lax.broadcasted_iota(I32, (blk, blk), 0) <= lax.broadcasted_iota(I32, (blk, blk), 1)).astype(BF16)

    def prefix(mask_f32):
        off = jnp.zeros((R, 1), F32)
        parts = []
        for k in range(T // blk):
            cb = _dot(mask_f32[:, k * blk:(k + 1) * blk].astype(BF16), tri) + off
            parts.append(cb)
            off = cb[:, blk - 1:blk]
        return parts

    eq_parts = prefix(eq.astype(F32))
    sel_parts = []
    for k in range(T // blk):
        sl = slice(k * blk, (k + 1) * blk)
        sel_parts.append(gt[:, sl] | (eq[:, sl] & (eq_parts[k] <= need)))
    sel = jnp.concatenate(sel_parts, axis=1)
    pos_parts = prefix(sel.astype(F32))
    for k in range(T // blk):
        sl = slice(k * blk, (k + 1) * blk)
        sid = jnp.where(sel_parts[k], pos_parts[k].astype(I32) - 1, -1)
        sid_ref[:, :, sl] = sid.reshape(B, E, blk)
        gate_ref[:, :, sl] = jnp.where(sel_parts[k], aff[:, sl], 0.0).reshape(B, E, blk)


def _topk(logits, cap):
    B, E, T = logits.shape
    full = pl.BlockSpec((B, E, T), lambda i: (0, 0, 0))
    return pl.pallas_call(
        functools.partial(_topk_kernel, cap=cap),
        grid=(1,),
        in_specs=[full],
        out_specs=[full, full],
        out_shape=[jax.ShapeDtypeStruct((B, E, T), I32), jax.ShapeDtypeStruct((B, E, T), F32)],
        compiler_params=_cp("arbitrary"),
        name=f"topk_{T}",
    )(logits)


GATHER_COLS = 512


def _gather_kernel(h_ref, sid_ref, gate_ref, xs_ref, gs_ref, *, cap, chunk):
    T = h_ref.shape[1]
    E = sid_ref.shape[1]
    acc = jnp.zeros((E * cap, GATHER_COLS), F32)
    slot = lax.broadcasted_iota(I32, (cap, chunk), 0)
    for k in range(T // chunk):
        sl = slice(k * chunk, (k + 1) * chunk)
        hit = jnp.concatenate([(slot == sid_ref[0, e, :, sl]).astype(BF16) for e in range(E)], axis=0)
        acc = acc + _dot(hit, h_ref[0, sl, :])
    for e in range(E):
        xs_ref[e] = acc[e * cap:(e + 1) * cap].astype(BF16)

    @pl.when(pl.program_id(1) == 0)
    def _():
        slot_t = lax.broadcasted_iota(I32, (cap, T), 0)
        for e in range(E):
            picked = jnp.where(slot_t == sid_ref[0, e], gate_ref[0, e], 0.0)
            gs_ref[e] = jnp.broadcast_to(jnp.sum(picked, axis=1, keepdims=True), (cap, 128))


def _gather(h, sid, gate, cap, row_block):
    B, E, T = sid.shape
    chunk = min(T, 512)
    idx = lambda b, j: (b, 0, 0, 0)
    return pl.pallas_call(
        functools.partial(_gather_kernel, cap=cap, chunk=chunk),
        grid=(B, D // GATHER_COLS),
        in_specs=[pl.BlockSpec((1, T, GATHER_COLS), lambda b, j: (b, row_block, j)),
                  pl.BlockSpec((1, E, 1, T), idx), pl.BlockSpec((1, E, 1, T), idx)],
        out_specs=[pl.BlockSpec((E, cap, GATHER_COLS), lambda b, j: (0, b, j)),
                   pl.BlockSpec((E, cap, 128), lambda b, j: (0, b, 0))],
        out_shape=[jax.ShapeDtypeStruct((E, B * cap, D), BF16),
                   jax.ShapeDtypeStruct((E, B * cap, 128), F32)],
        compiler_params=_cp("parallel", "arbitrary"),
        name=f"gather_{T}",
    )(h, sid.reshape(B, E, 1, T), gate.reshape(B, E, 1, T))


DOWN_CHUNK = 256


def _ffn_kernel(*refs, row_tiles):
    n_seg = len(row_tiles)
    xs_refs = refs[:n_seg]
    gs_refs = refs[n_seg:2 * n_seg]
    wg_ref, wu_ref, wd_ref = refs[2 * n_seg:2 * n_seg + 3]
    y_refs = refs[2 * n_seg + 3:3 * n_seg + 3]
    hid_refs = refs[3 * n_seg + 3:]
    s = pl.program_id(1)
    nf = EXPERT_FF // FF_CHUNK

    @pl.when(s < nf)
    def _():
        wg = wg_ref[0].astype(BF16)
        wu = wu_ref[0].astype(BF16)
        for xs_ref, hid_ref, rt in zip(xs_refs, hid_refs, row_tiles):
            for r0 in range(0, xs_ref.shape[1], rt):
                rows = pl.ds(r0, rt)
                x = xs_ref[0, rows, :]
                g = _dot(x, wg)
                u = _dot(x, wu)
                hid_ref[s, rows, :] = (g * jax.nn.sigmoid(g) * u).astype(BF16)

    @pl.when(s >= nf)
    def _():
        d = s - nf
        wd = wd_ref[0].astype(BF16)
        for gs_ref, y_ref, hid_ref, rt in zip(gs_refs, y_refs, hid_refs, row_tiles):
            for r0 in range(0, hid_ref.shape[1], rt):
                rows = pl.ds(r0, rt)
                acc = _dot(hid_ref[0, rows, :], wd[0:FF_CHUNK, :])
                for f in range(1, nf):
                    acc = acc + _dot(hid_ref[f, rows, :], wd[f * FF_CHUNK:(f + 1) * FF_CHUNK, :])
                gs = gs_ref[0, rows, :]
                y_ref[0, d, rows, :] = jnp.concatenate(
                    [acc[:, k * 128:(k + 1) * 128] * gs for k in range(DOWN_CHUNK // 128)], axis=1).astype(y_ref.dtype)


def _ffn(xs_list, gs_list, w_gate, w_up, w_down, layer):
    E = w_gate.shape[1]
    row_tiles = tuple(min(xs.shape[1], 512) for xs in xs_list)
    nf = EXPERT_FF // FF_CHUNK
    nd = D // DOWN_CHUNK
    in_specs = [pl.BlockSpec((1, xs.shape[1], D), lambda e, s: (e, 0, 0)) for xs in xs_list]
    in_specs += [pl.BlockSpec((1, gs.shape[1], 128), lambda e, s: (e, 0, 0)) for gs in gs_list]
    up_map = lambda e, s: (layer, e, 0, jnp.minimum(s, nf - 1))
    in_specs += [pl.BlockSpec((None, 1, D, FF_CHUNK), up_map),
                 pl.BlockSpec((None, 1, D, FF_CHUNK), up_map),
                 pl.BlockSpec((None, 1, EXPERT_FF, DOWN_CHUNK), lambda e, s: (layer, e, 0, jnp.maximum(s - nf, 0)))]
    return pl.pallas_call(
        functools.partial(_ffn_kernel, row_tiles=row_tiles),
        grid=(E, nf + nd),
        in_specs=in_specs,
        out_specs=[pl.BlockSpec((1, nd, xs.shape[1], DOWN_CHUNK), lambda e, s: (e, 0, 0, 0)) for xs in xs_list],
        out_shape=[jax.ShapeDtypeStruct((E, nd, xs.shape[1], DOWN_CHUNK), BF16) for xs in xs_list],
        scratch_shapes=[pltpu.VMEM((nf, xs.shape[1], FF_CHUNK), BF16) for xs in xs_list],
        compiler_params=_cp("parallel", "arbitrary"),
        name="expert_ffn",
    )(*xs_list, *gs_list, w_gate, w_up, w_down)


def _combine_kernel(x_ref, sid_ref, y_ref, mod_ref, *rest, cap):
    o_ref = rest[-1]
    tbs = x_ref.shape[1]
    E = sid_ref.shape[1]
    slot = lax.broadcasted_iota(I32, (cap, tbs), 0)
    hit = jnp.concatenate([(slot == sid_ref[0, e]).astype(BF16) for e in range(E)], axis=0)
    gate = mod_ref[0][5:6]
    for d in range(D // DOWN_CHUNK):
        cs = slice(d * DOWN_CHUNK, (d + 1) * DOWN_CHUNK)
        z = _dot_tn(hit, y_ref[:, d].reshape(E * cap, DOWN_CHUNK))
        o_ref[0, :, cs] = x_ref[0, :, cs] + gate[:, cs] * z
    if len(rest) == 2:
        o_ref[0] = _rmsnorm(o_ref[0], rest[0][...])


def _combine(x, sid, y, mod, cap, tbs, row_block0, segment, final_g=None):
    B, E, T = sid.shape
    ntb = T // tbs
    xmap = lambda b, tb: (b, row_block0 + tb, 0)
    in_specs = [pl.BlockSpec((1, tbs, D), xmap),
                pl.BlockSpec((1, E, 1, tbs), lambda b, tb: (b, 0, 0, tb)),
                pl.BlockSpec((E, D // DOWN_CHUNK, cap, DOWN_CHUNK), lambda b, tb: (0, 0, b, 0)),
                pl.BlockSpec((1, 8, D), lambda b, tb: (b * 2 + segment, 0, 0))]
    args = [x, sid.reshape(B, E, 1, T), y, mod]
    if final_g is None:
        out_spec, out_shape, alias = pl.BlockSpec((1, tbs, D), xmap), x.shape, {0: 0}
    else:
        assert row_block0 == 0
        in_specs.append(pl.BlockSpec((1, D), lambda b, tb: (0, 0)))
        args.append(final_g.reshape(1, D))
        out_spec, out_shape, alias = pl.BlockSpec((1, tbs, D), lambda b, tb: (b, tb, 0)), (B, T, D), {}
    return pl.pallas_call(
        functools.partial(_combine_kernel, cap=cap),
        grid=(B, ntb),
        in_specs=in_specs,
        out_specs=out_spec,
        out_shape=jax.ShapeDtypeStruct(out_shape, F32),
        input_output_aliases=alias,
        compiler_params=_cp("parallel", "parallel"),
        name=f"combine_{T}" + ("" if final_g is None else "_final"),
    )(*args)


def _moe(x, h, logits, mod, w_gate, w_up, w_down, layer, need_ctx, final_g):
    sid_l, gate_l = _topk(logits[:, :, :T_L], CAP_L)
    xs_l, gs_l = _gather(h, sid_l, gate_l, CAP_L, 0)
    if need_ctx:
        sid_c, gate_c = _topk(logits[:, :, T_L:], CAP_C)
        xs_c, gs_c = _gather(h, sid_c, gate_c, CAP_C, T_L // T_C)
        y_l, y_c = _ffn([xs_l, xs_c], [gs_l, gs_c], w_gate, w_up, w_down, layer)
    else:
        (y_l,) = _ffn([xs_l], [gs_l], w_gate, w_up, w_down, layer)
    x = _combine(x, sid_l, y_l, mod, CAP_L, 512, 0, 0, final_g)
    if need_ctx:
        x = _combine(x, sid_c, y_c, mod, CAP_C, T_C, T_L // T_C, 1)
    return x


OD_N = RWKV_IN + MLA_U


def _odd_weights(w_in, partner):
    kr = RWKV_IN + MLA_Q_RANK + MLA_KV_RANK
    pad = np.full((128 - MLA_ROPE,), -1)
    index = np.concatenate([np.arange(kr), kr + np.arange(MLA_ROPE), pad, kr + partner, pad])
    return _gather_columns(w_in, index)


def _inproj_odd_kernel(x_ref, mod_ref, g_ref, w_ref, qg_ref, kg_ref, wq_ref, wk_ref, wv_ref, pl_ref,
                       cq_ref, sq_ref, ck_ref, sk_ref, ur_ref, q_ref, k_ref, v_ref):
    mod = mod_ref[0]
    h = _modulate(x_ref[0], g_ref[...], mod[0:1], mod[1:2])
    y = _dot(h.astype(BF16), w_ref[...])
    ur_ref[0] = y[:, :RWKV_IN]
    u = y[:, RWKV_IN:]
    qn = _rmsnorm(u[:, :MLA_Q_RANK], qg_ref[...]).astype(BF16)
    kvn = _rmsnorm(u[:, MLA_Q_RANK:MLA_Q_RANK + MLA_KV_RANK], kg_ref[...]).astype(BF16)
    o = MLA_Q_RANK + MLA_KV_RANK
    kr = u[:, o:o + 128] * ck_ref[...] + u[:, o + 128:o + 256] * sk_ref[...]
    qq = _dot(qn, wq_ref[...])
    q_ref[0] = (qq[:, :MLA_QW] * cq_ref[...] + qq[:, MLA_QW:] * sq_ref[...]).astype(BF16)
    k_ref[0] = (_dot(kvn, wk_ref[...]) + _dot2(kr, pl_ref[...])).astype(BF16)
    v_ref[0] = _dot(kvn, wv_ref[...]).astype(BF16)


def _inproj_odd(x, mod, g, w, q_norm_g, kv_norm_g, mla_weights, mla_tables):
    B = x.shape[0]
    wq, wk, wv, place = mla_weights
    cq, sq, ck, sk = mla_tables
    tok = lambda n: pl.BlockSpec((1, TM, n), lambda b, t: (b, t, 0))
    tab = lambda n: pl.BlockSpec((TM, n), lambda b, t: (t, 0))
    full = lambda a: pl.BlockSpec(a.shape, lambda b, t: (0, 0))
    row = lambda n: pl.BlockSpec((1, n), lambda b, t: (0, 0))
    return pl.pallas_call(
        _inproj_odd_kernel,
        grid=(B, N_TILES),
        in_specs=[tok(D), _mod_spec(), row(D), full(w), row(MLA_Q_RANK), row(MLA_KV_RANK),
                  full(wq), full(wk), full(wv), full(place),
                  tab(MLA_QW), tab(MLA_QW), tab(MLA_SLOT), tab(MLA_SLOT)],
        out_specs=[tok(RWKV_IN), tok(MLA_QW), tok(MLA_QW), tok(MLA_HEADS * MLA_V)],
        out_shape=[jax.ShapeDtypeStruct((B, NT, RWKV_IN), F32),
                   jax.ShapeDtypeStruct((B, NT, MLA_QW), BF16),
                   jax.ShapeDtypeStruct((B, NT, MLA_QW), BF16),
                   jax.ShapeDtypeStruct((B, NT, MLA_HEADS * MLA_V), BF16)],
        compiler_params=_cp("parallel", "parallel"),
        name="inproj_odd",
    )(x, mod, g.reshape(1, D), w, q_norm_g.reshape(1, -1), kv_norm_g.reshape(1, -1), wq, wk, wv, place,
      cq, sq, ck, sk)


def _head_ones():
    h = np.arange(RWKV_WIDTH) // RWKV_HEAD
    return jnp.asarray((h[:, None] == h[None, :]).astype(np.float32)).astype(BF16)


def _headsum(x, bd):
    return _dot2(x, bd)


def _rwkv_prep_kernel(u_ref, up_ref, un_ref, mup_ref, mun_ref, w0_ref, w2_ref, a0_ref, a2_ref, g2_ref,
                      kk_ref, ka_ref, bd_ref,
                      r_ref, v_ref, kkn_ref, g_ref, lw_ref, bq_ref, kd_ref):
    t = pl.program_id(1)
    u = u_ref[0]
    rows = lax.broadcasted_iota(I32, (TM, 1), 0)
    is_ctx = t == N_LAT_TILES
    seg_start = (t == 0) | is_ctx
    seg_end = (t == N_LAT_TILES - 1) | is_ctx
    last_row = jnp.where(is_ctx, T_C - 1, TM - 1)
    prev_row = jnp.where(seg_start, 0.0, up_ref[0, 7:8, :])
    next_row = jnp.where(seg_end, 0.0, un_ref[0, 0:1, :])
    prev = jnp.where(rows == 0, prev_row, pltpu.roll(u, 1, 0))
    nxt = jnp.where(rows == last_row, next_row, pltpu.roll(u, TM - 1, 0))
    us = u + mup_ref[...] * (prev - u) + mun_ref[...] * (nxt - u)
    W = RWKV_WIDTH
    r = us[:, :W]
    k = us[:, W:2 * W]
    v = us[:, 2 * W:3 * W]
    wl = jnp.tanh(us[:, 3 * W:3 * W + 128]).astype(BF16)
    al = us[:, 3 * W + 128:3 * W + 256].astype(BF16)
    gl = jax.nn.sigmoid(us[:, 3 * W + 256:3 * W + 384]).astype(BF16)
    bd = bd_ref[...]
    kk = k * kk_ref[...]
    kk = kk * lax.rsqrt(jnp.maximum(_headsum(kk * kk, bd), 1e-24))
    r_ref[0] = r.astype(BF16)
    v_ref[0] = v.astype(BF16)
    kkn_ref[0] = kk.astype(BF16)
    g_ref[0] = _dot(gl, g2_ref[...]).astype(BF16)
    for d in range(2):
        z = w0_ref[d:d + 1, :] + _dot(wl, w2_ref[d])
        a = jax.nn.sigmoid(a0_ref[d:d + 1, :] + _dot(al, a2_ref[d]))
        lw_ref[d, 0] = jax.nn.sigmoid(z) * (-float(np.exp(-0.5)))
        bq_ref[d, 0] = (kk * a).astype(BF16)
        kd_ref[d, 0] = (k * (1.0 + (a - 1.0) * ka_ref[...])).astype(BF16)


def _rwkv_prep(ur, mu_prev, mu_next, w0, w2, a0, a2, g2, k_k, k_a, bd):
    B = ur.shape[0]
    W = RWKV_WIDTH
    z = jnp.zeros((64, W), F32)
    w2p = jnp.stack([jnp.concatenate([w2[0], z]), jnp.concatenate([z, w2[1]])]).astype(BF16)
    a2p = jnp.stack([jnp.concatenate([a2[0], z]), jnp.concatenate([z, a2[1]])]).astype(BF16)
    tok = lambda n: pl.BlockSpec((1, TM, n), lambda b, t: (b, t, 0))
    tok2 = pl.BlockSpec((2, 1, TM, W), lambda b, t: (0, b, t, 0))
    row = lambda n: pl.BlockSpec((1, n), lambda b, t: (0, 0))
    full = lambda *s: pl.BlockSpec(s, lambda b, t: (0,) * len(s))
    slab = TM // 8
    one = lambda: jax.ShapeDtypeStruct((B, NT, W), BF16)
    per_dir = lambda dt: jax.ShapeDtypeStruct((2, B, NT, W), dt)
    return pl.pallas_call(
        _rwkv_prep_kernel,
        grid=(B, N_TILES),
        in_specs=[tok(RWKV_IN),
                  pl.BlockSpec((1, 8, RWKV_IN), lambda b, t: (b, jnp.maximum(t * slab - 1, 0), 0)),
                  pl.BlockSpec((1, 8, RWKV_IN), lambda b, t: (b, jnp.minimum((t + 1) * slab, NT // 8 - 1), 0)),
                  row(RWKV_IN), row(RWKV_IN), full(2, W), full(2, 128, W), full(2, W), full(2, 128, W),
                  full(128, W), row(W), row(W), full(W, W)],
        out_specs=[tok(W), tok(W), tok(W), tok(W), tok2, tok2, tok2],
        out_shape=[one(), one(), one(), one(), per_dir(F32), per_dir(BF16), per_dir(BF16)],
        compiler_params=_cp("parallel", "parallel"),
        name="rwkv_prep",
    )(ur, ur, ur, mu_prev.reshape(1, RWKV_IN), mu_next.reshape(1, RWKV_IN), w0, w2p, a0, a2p,
      g2.astype(BF16), k_k.reshape(1, W), k_a.reshape(1, W), bd)


def _wkv_kernel(r_ref, v_ref, kk_ref, lw_ref, bq_ref, kd_ref, o_ref, s_ref, *, nb):
    d = pl.program_id(0)
    i = pl.program_id(2)
    C = WKV_CHUNK
    P = RWKV_HEADS // 2
    G = nb * P
    HD = RWKV_HEAD

    @pl.when(i == 0)
    def _():
        s_ref[...] = jnp.zeros_like(s_ref)

    sgn = 1 - 2 * d
    rel = (lax.broadcasted_iota(I32, (C, C), 0) - lax.broadcasted_iota(I32, (C, C), 1)) * sgn
    incl_b = (rel >= 0).astype(BF16)
    row_w = lax.broadcasted_iota(I32, (C, 2 * HD), 0)
    col_w = lax.broadcasted_iota(I32, (C, 2 * HD), 1) & (HD - 1)
    rel_w = (row_w - col_w) * sgn
    incl = rel_w >= 0
    strict = rel_w > 0
    eye = (rel_w == 0).astype(F32)
    lo = lax.broadcasted_iota(I32, (C, 2 * HD), 1) < HD
    same_head = ((lax.broadcasted_iota(I32, (2 * HD, 2 * HD), 0) < HD)
                 == (lax.broadcasted_iota(I32, (2 * HD, 2 * HD), 1) < HD))

    def same_block(k):
        return (row_w >> k) == (col_w >> k)

    def pair(y):
        zero = jnp.zeros_like(y)
        return jnp.concatenate([jnp.where(lo, y, zero), jnp.where(lo, zero, y)], axis=0)

    at, rt, bt, kt, a0, r0, bh, kh, v, p_end = ([] for _ in range(10))
    for b in range(nb):
        lw = lw_ref[0, b]
        lw_hi, lw_lo = _split_bf16(lw)
        ci = _dot(incl_b, lw_hi) + _dot(incl_b, lw_lo)
        ce = ci - lw
        tot = jnp.sum(lw, axis=0, keepdims=True)
        m = 0.5 * tot
        r = r_ref[b].astype(F32)
        kk = kk_ref[b].astype(F32)
        bq = bq_ref[0, b].astype(F32)
        kd = kd_ref[0, b].astype(F32)
        e_ci = jnp.exp(ci - m)
        e_mci = jnp.exp(m - ci)
        e_end = jnp.exp(tot - ci)
        full = ((at, (-kk * jnp.exp(ce - m)).astype(BF16)), (rt, (r * e_ci).astype(BF16)),
                (bt, (bq * e_mci).astype(BF16)), (kt, (kd * e_mci).astype(BF16)),
                (a0, (-kk * jnp.exp(ce)).astype(BF16)), (r0, (r * (e_ci * jnp.exp(m))).astype(BF16)),
                (bh, (bq * e_end).astype(BF16)), (kh, (kd * e_end).astype(BF16)),
                (v, v_ref[b]), (p_end, jnp.exp(tot)))
        for dst, x in full:
            dst.extend(x[:, p * 2 * HD:(p + 1) * 2 * HD] for p in range(P))

    S = [s_ref[g] for g in range(G)]
    s_b = [s.astype(BF16) for s in S]
    left = [jnp.concatenate([at[g], rt[g]], axis=0) for g in range(G)]
    nb_ = [_dot_nt(left[g], pair(bt[g])) for g in range(G)]
    nk_ = [_dot_nt(left[g], pair(kt[g])) for g in range(G)]
    n_ab = [jnp.where(strict, x[:C], 0.0) for x in nb_]
    a_rb = [jnp.where(incl, x[C:], 0.0).astype(BF16) for x in nb_]
    mask_k = jnp.concatenate([strict, incl], axis=0)
    a_k = [jnp.where(mask_k, x, 0.0).astype(BF16) for x in nk_]
    from_state = [_dot_nt(jnp.concatenate([a0[g], r0[g]], axis=0), s_b[g]) for g in range(G)]
    from_v = [_dot(a_k[g], pair(v[g])) for g in range(G)]
    w_rhs = [from_state[g][:C] + from_v[g][:C] for g in range(G)]
    inv = [eye + jnp.where(same_block(1), n, 0.0) for n in n_ab]
    for k in range(1, 6):
        off_mask = same_block(k + 1) & jnp.logical_not(same_block(k))
        inv_b = [x.astype(BF16) for x in inv]
        e = [_dot(jnp.where(off_mask, n, 0.0).astype(BF16), pair(x)) for n, x in zip(n_ab, inv_b)]
        inv = [x + _dot(xb, pair(y.astype(BF16))) for x, xb, y in zip(inv, inv_b, e)]
    U = [_dot(inv[g].astype(BF16), pair(w_rhs[g].astype(BF16))).astype(BF16) for g in range(G)]
    for g in range(G):
        o = from_state[g][C:] + _dot(a_rb[g], pair(U[g])) + from_v[g][C:]
        o_ref[0, g // P, :, (g % P) * 2 * HD:(g % P + 1) * 2 * HD] = o
    for g in range(G):
        upd = _dot_tn(jnp.concatenate([U[g], v[g]], axis=0), jnp.concatenate([bh[g], kh[g]], axis=0))
        s_ref[g] = S[g] * p_end[g] + jnp.where(same_head, upd, 0.0)


WKV_BATCH = 4


def _wkv(r, v, kk, lw, bq, kd, nb=WKV_BATCH):
    B = r.shape[0]
    C = WKV_CHUNK
    n_lat = T_L // C

    def chunk(d, i):
        fwd = jnp.where(i < N_CTX_CHUNKS, n_lat + i, i - N_CTX_CHUNKS)
        return jnp.where(d == 0, fwd, N_CHUNKS - 1 - i)

    shared = pl.BlockSpec((nb, C, RWKV_WIDTH), lambda d, b, i: (b, chunk(d, i), 0))
    perdir = pl.BlockSpec((1, nb, C, RWKV_WIDTH), lambda d, b, i: (d, b, chunk(d, i), 0))
    return pl.pallas_call(
        functools.partial(_wkv_kernel, nb=nb),
        grid=(2, B // nb, N_CHUNKS),
        in_specs=[shared, shared, shared, perdir, perdir, perdir],
        out_specs=perdir,
        out_shape=jax.ShapeDtypeStruct((2, B, NT, RWKV_WIDTH), F32),
        scratch_shapes=[pltpu.VMEM((nb * RWKV_HEADS // 2, 2 * RWKV_HEAD, 2 * RWKV_HEAD), F32)],
        compiler_params=_cp("parallel", "parallel", "arbitrary"),
        name="wkv7",
    )(r, v, kk, lw, bq, kd)


def _rwkv_finish_kernel(o_ref, r_ref, v_ref, g_ref, kd_ref, rk_ref, lg_ref, lb_ref, bd_ref, y_ref):
    bd = bd_ref[...]
    o = o_ref[0, 0] + o_ref[1, 0]
    mu = _headsum(o, bd) * (1.0 / RWKV_HEAD)
    xc = o - mu
    var = _headsum(xc * xc, bd) * (1.0 / RWKV_HEAD)
    on = xc * lax.rsqrt(var + LNX_EPS) * lg_ref[...] + lb_ref[...]
    k_sum = kd_ref[0, 0].astype(F32) + kd_ref[1, 0].astype(F32)
    bonus = _headsum(r_ref[0].astype(F32) * k_sum * rk_ref[...], bd) * v_ref[0].astype(F32)
    y_ref[0] = ((on + bonus) * g_ref[0].astype(F32)).astype(y_ref.dtype)


def _rwkv_finish(o, r, v, g, kd, r_k, lnx_g, lnx_b, bd, need_ctx):
    B = r.shape[0]
    W = RWKV_WIDTH
    tok = pl.BlockSpec((1, TM, W), lambda b, t: (b, t, 0))
    tok2 = pl.BlockSpec((2, 1, TM, W), lambda b, t: (0, b, t, 0))
    row = pl.BlockSpec((1, W), lambda b, t: (0, 0))
    return pl.pallas_call(
        _rwkv_finish_kernel,
        grid=(B, N_TILES if need_ctx else N_LAT_TILES),
        in_specs=[tok2, tok, tok, tok, tok2, row, row, row, pl.BlockSpec((W, W), lambda b, t: (0, 0))],
        out_specs=tok,
        out_shape=jax.ShapeDtypeStruct((B, NT, W), BF16),
        compiler_params=_cp("parallel", "parallel"),
        name="rwkv_finish",
    )(o, r, v, g, kd, r_k.reshape(1, W), lnx_g.reshape(1, W), lnx_b.reshape(1, W), bd)


MLA_QW = MLA_HEADS * MLA_SLOT


def _mla_weights(w_uq, w_ukv, partner):
    dq = MLA_NOPE + MLA_ROPE
    dkv = MLA_NOPE + MLA_V
    iq = np.full((2 * MLA_QW,), -1)
    ik = np.full((MLA_QW,), -1)
    iv = np.zeros((MLA_HEADS * MLA_V,), np.int64)
    place = np.zeros((128, MLA_QW), np.float32)
    for h in range(MLA_HEADS):
        iq[h * MLA_SLOT:h * MLA_SLOT + dq] = h * dq + np.arange(dq)
        iq[MLA_QW + h * MLA_SLOT + MLA_NOPE:MLA_QW + h * MLA_SLOT + dq] = h * dq + MLA_NOPE + partner
        ik[h * MLA_SLOT:h * MLA_SLOT + MLA_NOPE] = h * dkv + np.arange(MLA_NOPE)
        iv[h * MLA_V:(h + 1) * MLA_V] = h * dkv + MLA_NOPE + np.arange(MLA_V)
        place[np.arange(MLA_ROPE), h * MLA_SLOT + MLA_NOPE + np.arange(MLA_ROPE)] = 1.0
    return (_gather_columns(w_uq, iq), _gather_columns(w_ukv, ik), _gather_columns(w_ukv, iv),
            jnp.asarray(place).astype(BF16))


MLA_TQ = 512


def _mla_attn_kernel(q_ref, k_ref, v_ref, o_ref):
    t = pl.program_id(2)
    is_ctx = t == T_L // MLA_TQ
    lo_lanes = lax.broadcasted_iota(I32, (MLA_TQ, 128), 1) < MLA_V

    def attend(keys):
        n_keys = keys.stop - keys.start
        lo_rows = lax.broadcasted_iota(I32, (n_keys, 128), 1) < MLA_V
        vall = v_ref[0, keys, :]
        one = jnp.ones((n_keys, 128), vall.dtype)
        vh = [jnp.where(lo_rows, vall, one), jnp.where(lo_rows, one, vall)]
        s = [_dot_nt(q_ref[0, :, hh * MLA_SLOT:(hh + 1) * MLA_SLOT], k_ref[0, keys, hh * MLA_SLOT:(hh + 1) * MLA_SLOT])
             for hh in range(2)]
        e = [jnp.exp(x - jnp.max(x, axis=-1, keepdims=True)).astype(BF16) for x in s]
        r = [_dot(e[hh], vh[hh]) for hh in range(2)]
        num = jnp.where(lo_lanes, r[0], r[1])
        den = pltpu.roll(jnp.where(lo_lanes, r[1], r[0]), MLA_V, 1)
        o_ref[0] = (num * (1.0 / den)).astype(o_ref.dtype)

    @pl.when(jnp.logical_not(is_ctx))
    def _():
        attend(slice(0, NT))

    @pl.when(is_ctx)
    def _():
        attend(slice(T_L, NT))


def _mla_attn(q, k, v, need_ctx):
    B = q.shape[0]
    n_tiles = pl.cdiv(NT, MLA_TQ) if need_ctx else T_L // MLA_TQ
    return pl.pallas_call(
        _mla_attn_kernel,
        grid=(B, MLA_HEADS // 2, n_tiles),
        in_specs=[pl.BlockSpec((1, MLA_TQ, 2 * MLA_SLOT), lambda b, p, t: (b, t, p)),
                  pl.BlockSpec((1, NT, 2 * MLA_SLOT), lambda b, p, t: (b, 0, p)),
                  pl.BlockSpec((1, NT, 2 * MLA_V), lambda b, p, t: (b, 0, p))],
        out_specs=pl.BlockSpec((1, MLA_TQ, 2 * MLA_V), lambda b, p, t: (b, t, p)),
        out_shape=jax.ShapeDtypeStruct((B, NT, MLA_HEADS * MLA_V), BF16),
        compiler_params=_cp("parallel", "parallel", "parallel"),
        name="mla_attn",
    )(q, k, v)


def kernel(x, c, ctx, c_ctx, ada_w, ada_b, norm_mix_g, norm_ffn_g, router_w, exp_w_gate, exp_w_up, exp_w_down, ev_w_in, ev_w_out, ev_sink, ev_pool_w, ev_pool_scale, od_w_in, od_w_out, od_mu_prev, od_mu_next, od_w0, od_w2, od_a0, od_a2, od_g2, od_k_k, od_k_a, od_r_k, od_lnx_g, od_lnx_b, od_q_norm_g, od_w_uq, od_kv_norm_g, od_w_ukv, final_norm_g):
    depth = ada_w.shape[0]
    assert x.shape[1:] == (T_L, D) and ctx.shape[1:] == (T_C, D)
    mods = _modvec(c, c_ctx, ada_w, ada_b)
    ev_tables, ev_partner = _even_tables()
    mla_tables, mla_partner = _mla_tables()
    bd = _head_ones()
    xs = jnp.concatenate([x, ctx], axis=1)
    for l in range(depth):
        need_ctx = l < depth - 1
        i = l // 2
        mod = mods[l]
        if l % 2 == 0:
            q, kd, vd, pu = _inproj_even(xs, mod, norm_mix_g[l], _even_weights(ev_w_in[i], ev_partner), ev_tables)
            mix_a = _attn_even(q, kd, vd, ev_sink[i])
            mix_b = _pool(pu, ev_pool_w[i], ev_pool_scale[i])
            w_out = ev_w_out[i]
        else:
            ur, mq, mk, mv = _inproj_odd(xs, mod, norm_mix_g[l], _odd_weights(od_w_in[i], mla_partner),
                                         od_q_norm_g[i], od_kv_norm_g[i],
                                         _mla_weights(od_w_uq[i], od_w_ukv[i], mla_partner), mla_tables)
            r, v, kk, g, lw, bq, kdir = _rwkv_prep(ur, od_mu_prev[i], od_mu_next[i], od_w0[i], od_w2[i],
                                                   od_a0[i], od_a2[i], od_g2[i], od_k_k[i], od_k_a[i], bd)
            o = _wkv(r, v, kk, lw, bq, kdir)
            mix_a = _rwkv_finish(o, r, v, g, kdir, od_r_k[i], od_lnx_g[i], od_lnx_b[i], bd, need_ctx)
            mix_b = _mla_attn(mq, mk, mv, need_ctx)
            w_out = od_w_out[i]
        xs, h, logits = _outproj(mix_a, mix_b, w_out, xs, mod, norm_ffn_g[l], router_w[l], need_ctx)
        xs = _moe(xs, h, logits, mod, exp_w_gate, exp_w_up, exp_w_down, l, need_ctx,
                  None if need_ctx else final_norm_g)
    return xs
```

```python
import functools

import numpy as np
import jax
import jax.numpy as jnp
from jax import lax
from jax.experimental import pallas as pl
from jax.experimental.pallas import tpu as pltpu

F32 = jnp.float32
BF16 = jnp.bfloat16
I32 = jnp.int32

D = 1024
T_L = 2048
T_C = 256
NT = T_L + T_C
GRID_W = 64
EPS = 1e-6
ROPE_BASE = 10000.0

TM = 512
N_LAT_TILES = T_L // TM
N_TILES = N_LAT_TILES + 1
assert T_C <= TM and T_L % TM == 0

A_HEADS = 8
A_KV_HEADS = 2
HEAD_DIM = 64
A_WINDOW = 128
A_BLOCK = 128
N_ABLK = NT // A_BLOCK
N_LAT_ABLK = T_L // A_BLOCK

POOL_WINDOWS = (2, 4, 8, 16)
POOL_GROUP = 128

RWKV_HEADS = 8
RWKV_HEAD = 64
RWKV_WIDTH = 512
RWKV_IN = 1920
LNX_EPS = 64e-5
WKV_CHUNK = 64
N_CHUNKS = NT // WKV_CHUNK
N_CTX_CHUNKS = T_C // WKV_CHUNK

MLA_HEADS = 8
MLA_Q_RANK = 256
MLA_KV_RANK = 128
MLA_NOPE = 64
MLA_ROPE = 32
MLA_V = 64
MLA_SLOT = 128
MLA_U = 640

N_EXPERTS = 16
EXPERT_FF = 2048
CAP_L = 2 * T_L // N_EXPERTS
CAP_C = 2 * T_C // N_EXPERTS
FF_CHUNK = 512

VMEM_LIMIT_BYTES = 56 * 1024 * 1024


def _cp(*sem):
    return pltpu.CompilerParams(dimension_semantics=sem, vmem_limit_bytes=VMEM_LIMIT_BYTES)


def _split_bf16(x):
    hi = x.astype(BF16)
    lo = (x - hi.astype(F32)).astype(BF16)
    return hi, lo


def _dot(a, b):
    return jnp.dot(a, b, preferred_element_type=F32)


def _dot_nt(a, b):
    return lax.dot_general(a, b, (((1,), (1,)), ((), ())), preferred_element_type=F32)


def _dot_tn(a, b):
    return lax.dot_general(a, b, (((0,), (0,)), ((), ())), preferred_element_type=F32)


def _dot2(a_f32, b_bf16):
    hi, lo = _split_bf16(a_f32)
    return _dot(hi, b_bf16) + _dot(lo, b_bf16)


def _modulate(x, g, shift, scale):
    ms = jnp.mean(x * x, axis=-1, keepdims=True)
    return x * lax.rsqrt(ms + EPS) * g * (1.0 + scale) + shift


def _rmsnorm(x, g):
    ms = jnp.mean(x * x, axis=-1, keepdims=True)
    return x * lax.rsqrt(ms + EPS) * g


def _modvec_kernel(a_ref, w_ref, b_ref, o_ref):
    a = a_ref[...]
    a = a * jax.nn.sigmoid(a)
    a_hi, a_lo = _split_bf16(a)
    w_hi, w_lo = _split_bf16(w_ref[0])
    o_ref[0] = _dot(a_hi, w_hi) + _dot(a_lo, w_hi) + _dot(a_hi, w_lo) + b_ref[0]


def _modvec(c, c_ctx, ada_w, ada_b):
    L = ada_w.shape[0]
    B = c.shape[0]
    assert B < 16
    rows = jnp.zeros((16, D), F32).at[:B].set(c).at[B].set(c_ctx)
    tn = 512
    out = pl.pallas_call(
        _modvec_kernel,
        grid=(L, 6 * D // tn),
        in_specs=[pl.BlockSpec((16, D), lambda l, n: (0, 0)),
                  pl.BlockSpec((1, D, tn), lambda l, n: (l, 0, n)),
                  pl.BlockSpec((1, 1, tn), lambda l, n: (l, 0, n))],
        out_specs=pl.BlockSpec((1, 16, tn), lambda l, n: (l, 0, n)),
        out_shape=jax.ShapeDtypeStruct((L, 16, 6 * D), F32),
        compiler_params=_cp("parallel", "parallel"),
        name="modvec",
    )(rows, ada_w, ada_b.reshape(L, 1, 6 * D))
    m = out.reshape(L, 16, 6, D)
    ml = m[:, :B]
    mc = jnp.broadcast_to(m[:, B:B + 1], (L, B, 6, D))
    mod = jnp.stack([ml, mc], axis=2)
    mod = jnp.pad(mod, ((0, 0), (0, 0), (0, 0), (0, 2), (0, 0)))
    return mod.reshape(L, B * 2, 8, D)


def _mod_spec():
    return pl.BlockSpec((1, 8, D), lambda b, t: (b * 2 + t // N_LAT_TILES, 0, 0))


def _rope_block(width, pos_row, pos_col):
    d = width // 2
    half = d // 2
    inv = ROPE_BASE ** (-np.arange(half, dtype=np.float32) / half)
    cos = np.zeros((pos_row.shape[0], width), np.float32)
    sin = np.zeros_like(cos)
    partner = np.zeros((width,), np.int64)
    for part, pos in ((0, pos_row), (1, pos_col)):
        ang = pos.astype(np.float32)[:, None] * inv[None, :]
        c, s = np.cos(ang), np.sin(ang)
        o = part * d
        cos[:, o:o + half] = c
        cos[:, o + half:o + d] = c
        sin[:, o:o + half] = -s
        sin[:, o + half:o + d] = s
        partner[o:o + half] = np.arange(o + half, o + d)
        partner[o + half:o + d] = np.arange(o, o + half)
    return cos, sin, partner


def _latent_positions():
    t = np.arange(T_L)
    return t // GRID_W, t % GRID_W


def _even_tables():
    cos, sin, partner = _rope_block(HEAD_DIM, *_latent_positions())
    scale = HEAD_DIM ** -0.5
    ones = np.ones((T_C, HEAD_DIM), np.float32)
    zeros = np.zeros((T_C, HEAD_DIM), np.float32)
    cos = np.concatenate([cos, ones], axis=0)
    sin = np.concatenate([sin, zeros], axis=0)
    cq = np.tile(cos, (1, A_HEADS)) * scale
    sq = np.tile(sin, (1, A_HEADS)) * scale
    ck = np.tile(cos, (1, 2 * A_KV_HEADS))
    sk = np.tile(sin, (1, 2 * A_KV_HEADS))
    return (jnp.asarray(cq), jnp.asarray(sq), jnp.asarray(ck), jnp.asarray(sk)), partner


def _mla_tables():
    cos, sin, partner = _rope_block(MLA_ROPE, *_latent_positions())
    cos = np.concatenate([cos, np.ones((T_C, MLA_ROPE), np.float32)], axis=0)
    sin = np.concatenate([sin, np.zeros((T_C, MLA_ROPE), np.float32)], axis=0)
    scale = (MLA_NOPE + MLA_ROPE) ** -0.5
    cq1 = np.zeros((NT, MLA_SLOT), np.float32)
    sq1 = np.zeros((NT, MLA_SLOT), np.float32)
    cq1[:, :MLA_NOPE] = 1.0
    cq1[:, MLA_NOPE:MLA_NOPE + MLA_ROPE] = cos
    sq1[:, MLA_NOPE:MLA_NOPE + MLA_ROPE] = sin
    cq = np.tile(cq1, (1, MLA_HEADS)) * scale
    sq = np.tile(sq1, (1, MLA_HEADS)) * scale
    ck = np.zeros((NT, MLA_SLOT), np.float32)
    sk = np.zeros((NT, MLA_SLOT), np.float32)
    ck[:, :MLA_ROPE] = cos
    sk[:, :MLA_ROPE] = sin
    return (jnp.asarray(cq), jnp.asarray(sq), jnp.asarray(ck), jnp.asarray(sk)), partner


EV_Q, EV_K, EV_V, EV_P = 512, 256, 256, 512
EV_N = EV_Q + EV_K + EV_V + EV_P
ROPE_HALF = HEAD_DIM // 4


def _gather_columns(w, index):
    w0 = jnp.concatenate([w, jnp.zeros((w.shape[0], 1), w.dtype)], axis=1).astype(BF16)
    return w0[:, np.where(index < 0, w.shape[1], index)]


def _even_weights(w_in):
    qw = A_HEADS * HEAD_DIM
    kw = A_KV_HEADS * HEAD_DIM
    dup = np.concatenate([h * HEAD_DIM + np.tile(np.arange(HEAD_DIM), 2) for h in range(A_KV_HEADS)])
    index = np.concatenate([np.arange(qw), qw + dup, qw + kw + dup, qw + 2 * kw + np.arange(EV_P)])
    return _gather_columns(w_in, index)


def _swap_halves(x, half):
    n = x.shape[1]
    first = (lax.broadcasted_iota(I32, x.shape, 1) & (2 * half - 1)) < half
    return jnp.where(first, pltpu.roll(x, n - half, 1), pltpu.roll(x, half, 1))


def _inproj_even_kernel(x_ref, mod_ref, g_ref, w_ref, cq_ref, sq_ref, ck_ref, sk_ref,
                        q_ref, k_ref, v_ref, p_ref):
    mod = mod_ref[0]
    h = _modulate(x_ref[0], g_ref[...], mod[0:1], mod[1:2])
    y = _dot(h.astype(BF16), w_ref[...])
    yq = y[:, :EV_Q]
    yk = y[:, EV_Q:EV_Q + EV_K]
    o = EV_Q + EV_K
    q_ref[0] = (yq * cq_ref[...] + _swap_halves(yq, ROPE_HALF) * sq_ref[...]).astype(BF16)
    k_ref[0] = (yk * ck_ref[...] + _swap_halves(yk, ROPE_HALF) * sk_ref[...]).astype(BF16)
    v_ref[0] = y[:, o:o + EV_V].astype(BF16)
    p_ref[0] = y[:, o + EV_V:o + EV_V + EV_P]


def _inproj_even(x, mod, g, w, tables):
    B = x.shape[0]
    cq, sq, ck, sk = tables
    tok = lambda n: pl.BlockSpec((1, TM, n), lambda b, t: (b, t, 0))
    tab = lambda n: pl.BlockSpec((TM, n), lambda b, t: (t, 0))
    return pl.pallas_call(
        _inproj_even_kernel,
        grid=(B, N_TILES),
        in_specs=[tok(D), _mod_spec(), pl.BlockSpec((1, D), lambda b, t: (0, 0)),
                  pl.BlockSpec((D, EV_N), lambda b, t: (0, 0)),
                  tab(EV_Q), tab(EV_Q), tab(EV_K), tab(EV_K)],
        out_specs=[tok(EV_Q), tok(EV_K), tok(EV_V), tok(EV_P)],
        out_shape=[jax.ShapeDtypeStruct((B, NT, EV_Q), BF16),
                   jax.ShapeDtypeStruct((B, NT, EV_K), BF16),
                   jax.ShapeDtypeStruct((B, NT, EV_V), BF16),
                   jax.ShapeDtypeStruct((B, NT, EV_P), F32)],
        compiler_params=_cp("parallel", "parallel"),
        name="inproj_even",
    )(x, mod, g.reshape(1, D), w, cq, sq, ck, sk)


A_STEP = 2


def _attn_even_kernel(sink_ref, q_ref, kp_ref, kc_ref, kn_ref, kx_ref,
                      vp_ref, vc_ref, vn_ref, vx_ref, o_ref):
    j = pl.program_id(1)
    is_lat = j < N_LAT_ABLK // A_STEP
    lo_lanes = lax.broadcasted_iota(I32, (A_BLOCK, 128), 1) < HEAD_DIM
    group = A_HEADS // A_KV_HEADS
    heads = range(A_HEADS)
    qm, kall, vall, valid = [], [], [], []
    for sb in range(A_STEP):
        jb = j * A_STEP + sb
        rows = slice(sb * A_BLOCK, (sb + 1) * A_BLOCK)
        qpos = jb * A_BLOCK + lax.broadcasted_iota(I32, (A_BLOCK, 3 * A_BLOCK), 0)
        kpos = (jb - 1) * A_BLOCK + lax.broadcasted_iota(I32, (A_BLOCK, 3 * A_BLOCK), 1)
        valid.append((jnp.abs(kpos - qpos) <= A_WINDOW) & (kpos >= 0) & (kpos < T_L) & is_lat)
        k_sb, v_sb = [], []
        for kv in range(A_KV_HEADS):
            ks = slice(kv * 128, (kv + 1) * 128)
            kblocks = [kp_ref[0, :, ks]] + [kc_ref[0, i * A_BLOCK:(i + 1) * A_BLOCK, ks] for i in range(A_STEP)] \
                + [kn_ref[0, :, ks]]
            vblocks = [vp_ref[0, :, ks]] + [vc_ref[0, i * A_BLOCK:(i + 1) * A_BLOCK, ks] for i in range(A_STEP)] \
                + [vn_ref[0, :, ks]]
            k_sb.append(jnp.concatenate([kx_ref[0, :, ks]] + kblocks[sb:sb + 3], axis=0))
            v_sb.append(jnp.concatenate([vx_ref[0, :, ks]] + vblocks[sb:sb + 3], axis=0))
        kall.append(k_sb)
        vall.append(v_sb)
        q = q_ref[0, rows, :]
        qm.append([jnp.where(lo_lanes if h % 2 == 0 else jnp.logical_not(lo_lanes),
                             q[:, (h // 2) * 128:(h // 2 + 1) * 128], jnp.zeros((A_BLOCK, 128), q.dtype))
                   for h in heads])
    units = [(sb, h) for sb in range(A_STEP) for h in heads]
    s = [_dot_nt(qm[sb][h], kall[sb][h // group]) for sb, h in units]
    e, inv_den = [], []
    for (sb, h), sc in zip(units, s):
        snk = sink_ref[h]
        s_ctx = sc[:, :T_C]
        s_loc = jnp.where(valid[sb], sc[:, T_C:], -jnp.inf)
        m = jnp.maximum(jnp.max(s_ctx, axis=-1, keepdims=True), jnp.max(s_loc, axis=-1, keepdims=True))
        m = jnp.maximum(m, snk)
        e_ctx = jnp.exp(s_ctx - m)
        e_loc = jnp.exp(s_loc - m)
        den = (jnp.sum(e_ctx, axis=-1, keepdims=True) + jnp.sum(e_loc, axis=-1, keepdims=True)
               + jnp.exp(snk - m))
        e.append(jnp.concatenate([e_ctx, e_loc], axis=1).astype(BF16))
        inv_den.append(1.0 / den)
    o = [_dot(e[i], vall[sb][h // group]) * inv_den[i] for i, (sb, h) in enumerate(units)]
    for sb in range(A_STEP):
        for p in range(A_HEADS // 2):
            ev, od = o[sb * A_HEADS + 2 * p], o[sb * A_HEADS + 2 * p + 1]
            o_ref[0, sb * A_BLOCK:(sb + 1) * A_BLOCK, p * 128:(p + 1) * 128] = \
                jnp.where(lo_lanes, ev, od).astype(o_ref.dtype)


def _attn_even(q, kd, vd, sink):
    B = q.shape[0]
    last = N_LAT_ABLK - 1
    step = A_STEP * A_BLOCK
    one = lambda f: pl.BlockSpec((1, A_BLOCK, 256), f)
    prev = lambda b, j: (b, jnp.clip(j * A_STEP - 1, 0, last), 0)
    nxt = lambda b, j: (b, jnp.clip((j + 1) * A_STEP, 0, last), 0)
    cur = lambda n: pl.BlockSpec((1, step, n), lambda b, j: (b, j, 0))
    ctx = pl.BlockSpec((1, T_C, 256), lambda b, j: (b, T_L // T_C, 0))
    return pl.pallas_call(
        _attn_even_kernel,
        grid=(B, NT // step),
        in_specs=[pl.BlockSpec(memory_space=pltpu.SMEM),
                  cur(EV_Q),
                  one(prev), cur(256), one(nxt), ctx,
                  one(prev), cur(256), one(nxt), ctx],
        out_specs=cur(512),
        out_shape=jax.ShapeDtypeStruct((B, NT, 512), BF16),
        compiler_params=_cp("parallel", "parallel"),
        name="attn_even",
    )(sink, q, kd, kd, kd, kd, vd, vd, vd, vd)


POOL_BLOCK = 256
POOL_HALO = 8


def _pool_kernel(up_ref, uc_ref, un_ref, w_ref, sc_ref, o_ref):
    j = pl.program_id(1)
    R = POOL_BLOCK
    is_lat = j < T_L // R
    seg_lo = jnp.where(is_lat, 0, T_L)
    seg_hi = jnp.where(is_lat, T_L, NT)
    uc = uc_ref[0]
    halo = jnp.concatenate([up_ref[0], un_ref[0]], axis=0)
    r = j * R + lax.broadcasted_iota(I32, (R, R), 0)
    c = j * R + lax.broadcasted_iota(I32, (R, R), 1)
    rh = j * R + lax.broadcasted_iota(I32, (R, 2 * POOL_HALO), 0)
    ch = lax.broadcasted_iota(I32, (R, 2 * POOL_HALO), 1)
    ch = j * R + jnp.where(ch < POOL_HALO, ch - POOL_HALO, ch - POOL_HALO + R)
    r1 = j * R + lax.broadcasted_iota(I32, (R, 1), 0)
    hi_u, lo_u = _split_bf16(uc)
    hi_h, lo_h = _split_bf16(halo)
    groups = range(len(POOL_WINDOWS))
    cols = [slice(gi * POOL_GROUP, (gi + 1) * POOL_GROUP) for gi in groups]
    bands, bands_h, inv_cnt = [], [], []
    for w in POOL_WINDOWS:
        in_window = lambda rows, cc: ((cc >= jnp.maximum(rows - w // 2, seg_lo))
                                      & (cc < jnp.minimum(rows + w // 2, seg_hi))).astype(BF16)
        bands.append(in_window(r, c))
        bands_h.append(in_window(rh, ch))
        cnt = (jnp.minimum(r1 + w // 2, seg_hi) - jnp.maximum(r1 - w // 2, seg_lo)).astype(F32)
        inv_cnt.append(1.0 / cnt)
    sums = [_dot(bands[gi], jnp.concatenate([hi_u[:, cols[gi]], lo_u[:, cols[gi]]], axis=1))
            + _dot(bands_h[gi], jnp.concatenate([hi_h[:, cols[gi]], lo_h[:, cols[gi]]], axis=1)) for gi in groups]
    dev = [((s[:, :POOL_GROUP] + s[:, POOL_GROUP:]) * inv_cnt[gi] - uc[:, cols[gi]]).astype(BF16)
           for gi, s in enumerate(sums)]
    outs = [_dot(dev[gi], w_ref[gi]) for gi in groups]
    o_ref[0] = (jnp.concatenate(outs, axis=1) * sc_ref[...]).astype(o_ref.dtype)


def _pool(u, pool_w, pool_scale):
    B = u.shape[0]
    slabs = POOL_BLOCK // POOL_HALO
    slab = lambda f: pl.BlockSpec((1, POOL_HALO, 512), f)
    blk = pl.BlockSpec((1, POOL_BLOCK, 512), lambda b, j: (b, j, 0))
    return pl.pallas_call(
        _pool_kernel,
        grid=(B, NT // POOL_BLOCK),
        in_specs=[slab(lambda b, j: (b, jnp.maximum(j * slabs - 1, 0), 0)),
                  blk,
                  slab(lambda b, j: (b, jnp.minimum((j + 1) * slabs, NT // POOL_HALO - 1), 0)),
                  pl.BlockSpec((4, POOL_GROUP, POOL_GROUP), lambda b, j: (0, 0, 0)),
                  pl.BlockSpec((1, 512), lambda b, j: (0, 0))],
        out_specs=blk,
        out_shape=jax.ShapeDtypeStruct((B, NT, 512), BF16),
        compiler_params=_cp("parallel", "parallel"),
        name="pool",
    )(u, u, u, pool_w.astype(BF16), pool_scale.reshape(1, 512))


def _outproj_kernel(a_ref, p_ref, w_ref, x_ref, mod_ref, g_ref, wr_ref, xo_ref, h_ref, lg_ref):
    mod = mod_ref[0]
    y = _dot(a_ref[0], w_ref[0:512, :]) + _dot(p_ref[0], w_ref[512:1024, :])
    x = x_ref[0] + mod[2:3] * y
    xo_ref[0] = x
    h = _modulate(x, g_ref[...], mod[3:4], mod[4:5])
    w_hi, w_lo = _split_bf16(wr_ref[...])
    h_hi, h_lo = _split_bf16(h)
    h_ref[0] = h_hi
    lg_ref[0] = _dot_nt(w_hi, h_hi) + _dot_nt(w_lo, h_hi) + _dot_nt(w_hi, h_lo)


def _outproj(a, p, w_out, x, mod, g_ffn, w_router, need_ctx):
    B = x.shape[0]
    tok = lambda n: pl.BlockSpec((1, TM, n), lambda b, t: (b, t, 0))
    return pl.pallas_call(
        _outproj_kernel,
        grid=(B, N_TILES if need_ctx else N_LAT_TILES),
        in_specs=[tok(512), tok(512), pl.BlockSpec((D, D), lambda b, t: (0, 0)), tok(D), _mod_spec(),
                  pl.BlockSpec((1, D), lambda b, t: (0, 0)),
                  pl.BlockSpec((N_EXPERTS, D), lambda b, t: (0, 0))],
        out_specs=[tok(D), tok(D), pl.BlockSpec((1, N_EXPERTS, TM), lambda b, t: (b, 0, t))],
        out_shape=[jax.ShapeDtypeStruct((B, NT, D), F32),
                   jax.ShapeDtypeStruct((B, NT, D), BF16),
                   jax.ShapeDtypeStruct((B, N_EXPERTS, NT), F32)],
        compiler_params=_cp("parallel", "parallel"),
        name="outproj",
    )(a, p, w_out.astype(BF16), x, mod, g_ffn.reshape(1, D), w_router.T)


def _topk_kernel(lg_ref, sid_ref, gate_ref, *, cap):
    B, E, T = lg_ref.shape
    R = B * E
    lg = lg_ref[...]
    m = jnp.max(lg, axis=1, keepdims=True)
    e = jnp.exp(lg - m)
    aff = (e / jnp.sum(e, axis=1, keepdims=True)).reshape(R, T)
    bits = lax.bitcast_convert_type(aff, I32)

    def step(i, thr):
        cand = thr | jnp.left_shift(jnp.int32(1), 30 - i)
        cnt = jnp.sum((bits >= cand).astype(F32), axis=1, keepdims=True)
        return jnp.where(cnt >= cap, cand, thr)

    thr = lax.fori_loop(0, 31, step, jnp.zeros((R, 1), I32))
    gt = bits > thr
    eq = bits == thr
    need = cap - jnp.sum(gt.astype(F32), axis=1, keepdims=True)

    blk = 256
    tri = (lax.broadcasted_iota(I32, (blk, blk), 0) <= lax.broadcasted_iota(I32, (blk, blk), 1)).astype(BF16)

    def prefix(mask_f32):
        off = jnp.zeros((R, 1), F32)
        parts = []
        for k in range(T // blk):
            cb = _dot(mask_f32[:, k * blk:(k + 1) * blk].astype(BF16), tri) + off
            parts.append(cb)
            off = cb[:, blk - 1:blk]
        return parts

    eq_parts = prefix(eq.astype(F32))
    sel_parts = []
    for k in range(T // blk):
        sl = slice(k * blk, (k + 1) * blk)
        sel_parts.append(gt[:, sl] | (eq[:, sl] & (eq_parts[k] <= need)))
    sel = jnp.concatenate(sel_parts, axis=1)
    pos_parts = prefix(sel.astype(F32))
    for k in range(T // blk):
        sl = slice(k * blk, (k + 1) * blk)
        sid = jnp.where(sel_parts[k], pos_parts[k].astype(I32) - 1, -1)
        sid_ref[:, :, sl] = sid.reshape(B, E, blk)
        gate_ref[:, :, sl] = jnp.where(sel_parts[k], aff[:, sl], 0.0).reshape(B, E, blk)


def _topk(logits, cap):
    B, E, T = logits.shape
    full = pl.BlockSpec((B, E, T), lambda i: (0, 0, 0))
    return pl.pallas_call(
        functools.partial(_topk_kernel, cap=cap),
        grid=(1,),
        in_specs=[full],
        out_specs=[full, full],
        out_shape=[jax.ShapeDtypeStruct((B, E, T), I32), jax.ShapeDtypeStruct((B, E, T), F32)],
        compiler_params=_cp("arbitrary"),
        name=f"topk_{T}",
    )(logits)


GATHER_COLS = 512


def _gather_kernel(h_ref, sid_ref, gate_ref, xs_ref, gs_ref, *, cap, chunk):
    T = h_ref.shape[1]
    E = sid_ref.shape[1]
    acc = jnp.zeros((E * cap, GATHER_COLS), F32)
    slot = lax.broadcasted_iota(I32, (cap, chunk), 0)
    for k in range(T // chunk):
        sl = slice(k * chunk, (k + 1) * chunk)
        hit = jnp.concatenate([(slot == sid_ref[0, e, :, sl]).astype(BF16) for e in range(E)], axis=0)
        acc = acc + _dot(hit, h_ref[0, sl, :])
    for e in range(E):
        xs_ref[e] = acc[e * cap:(e + 1) * cap].astype(BF16)

    @pl.when(pl.program_id(1) == 0)
    def _():
        slot_t = lax.broadcasted_iota(I32, (cap, T), 0)
        for e in range(E):
            picked = jnp.where(slot_t == sid_ref[0, e], gate_ref[0, e], 0.0)
            gs_ref[e] = jnp.broadcast_to(jnp.sum(picked, axis=1, keepdims=True), (cap, 128))


def _gather(h, sid, gate, cap, row_block):
    B, E, T = sid.shape
    chunk = min(T, 512)
    idx = lambda b, j: (b, 0, 0, 0)
    return pl.pallas_call(
        functools.partial(_gather_kernel, cap=cap, chunk=chunk),
        grid=(B, D // GATHER_COLS),
        in_specs=[pl.BlockSpec((1, T, GATHER_COLS), lambda b, j: (b, row_block, j)),
                  pl.BlockSpec((1, E, 1, T), idx), pl.BlockSpec((1, E, 1, T), idx)],
        out_specs=[pl.BlockSpec((E, cap, GATHER_COLS), lambda b, j: (0, b, j)),
                   pl.BlockSpec((E, cap, 128), lambda b, j: (0, b, 0))],
        out_shape=[jax.ShapeDtypeStruct((E, B * cap, D), BF16),
                   jax.ShapeDtypeStruct((E, B * cap, 128), F32)],
        compiler_params=_cp("parallel", "arbitrary"),
        name=f"gather_{T}",
    )(h, sid.reshape(B, E, 1, T), gate.reshape(B, E, 1, T))


DOWN_CHUNK = 256


def _ffn_kernel(*refs, row_tiles):
    n_seg = len(row_tiles)
    xs_refs = refs[:n_seg]
    gs_refs = refs[n_seg:2 * n_seg]
    wg_ref, wu_ref, wd_ref = refs[2 * n_seg:2 * n_seg + 3]
    y_refs = refs[2 * n_seg + 3:3 * n_seg + 3]
    hid_refs = refs[3 * n_seg + 3:]
    s = pl.program_id(1)
    nf = EXPERT_FF // FF_CHUNK

    @pl.when(s < nf)
    def _():
        wg = wg_ref[0].astype(BF16)
        wu = wu_ref[0].astype(BF16)
        for xs_ref, hid_ref, rt in zip(xs_refs, hid_refs, row_tiles):
            for r0 in range(0, xs_ref.shape[1], rt):
                rows = pl.ds(r0, rt)
                x = xs_ref[0, rows, :]
                g = _dot(x, wg)
                u = _dot(x, wu)
                hid_ref[s, rows, :] = (g * jax.nn.sigmoid(g) * u).astype(BF16)

    @pl.when(s >= nf)
    def _():
        d = s - nf
        wd = wd_ref[0].astype(BF16)
        for gs_ref, y_ref, hid_ref, rt in zip(gs_refs, y_refs, hid_refs, row_tiles):
            for r0 in range(0, hid_ref.shape[1], rt):
                rows = pl.ds(r0, rt)
                acc = _dot(hid_ref[0, rows, :], wd[0:FF_CHUNK, :])
                for f in range(1, nf):
                    acc = acc + _dot(hid_ref[f, rows, :], wd[f * FF_CHUNK:(f + 1) * FF_CHUNK, :])
                gs = gs_ref[0, rows, :]
                y_ref[0, d, rows, :] = jnp.concatenate(
                    [acc[:, k * 128:(k + 1) * 128] * gs for k in range(DOWN_CHUNK // 128)], axis=1).astype(y_ref.dtype)


def _ffn(xs_list, gs_list, w_gate, w_up, w_down, layer):
    E = w_gate.shape[1]
    row_tiles = tuple(min(xs.shape[1], 512) for xs in xs_list)
    nf = EXPERT_FF // FF_CHUNK
    nd = D // DOWN_CHUNK
    in_specs = [pl.BlockSpec((1, xs.shape[1], D), lambda e, s: (e, 0, 0)) for xs in xs_list]
    in_specs += [pl.BlockSpec((1, gs.shape[1], 128), lambda e, s: (e, 0, 0)) for gs in gs_list]
    up_map = lambda e, s: (layer, e, 0, jnp.minimum(s, nf - 1))
    in_specs += [pl.BlockSpec((None, 1, D, FF_CHUNK), up_map),
                 pl.BlockSpec((None, 1, D, FF_CHUNK), up_map),
                 pl.BlockSpec((None, 1, EXPERT_FF, DOWN_CHUNK), lambda e, s: (layer, e, 0, jnp.maximum(s - nf, 0)))]
    return pl.pallas_call(
        functools.partial(_ffn_kernel, row_tiles=row_tiles),
        grid=(E, nf + nd),
        in_specs=in_specs,
        out_specs=[pl.BlockSpec((1, nd, xs.shape[1], DOWN_CHUNK), lambda e, s: (e, 0, 0, 0)) for xs in xs_list],
        out_shape=[jax.ShapeDtypeStruct((E, nd, xs.shape[1], DOWN_CHUNK), BF16) for xs in xs_list],
        scratch_shapes=[pltpu.VMEM((nf, xs.shape[1], FF_CHUNK), BF16) for xs in xs_list],
        compiler_params=_cp("parallel", "arbitrary"),
        name="expert_ffn",
    )(*xs_list, *gs_list, w_gate, w_up, w_down)


def _combine_kernel(x_ref, sid_ref, y_ref, mod_ref, *rest, cap):
    o_ref = rest[-1]
    tbs = x_ref.shape[1]
    E = sid_ref.shape[1]
    slot = lax.broadcasted_iota(I32, (cap, tbs), 0)
    hit = jnp.concatenate([(slot == sid_ref[0, e]).astype(BF16) for e in range(E)], axis=0)
    gate = mod_ref[0][5:6]
    for d in range(D // DOWN_CHUNK):
        cs = slice(d * DOWN_CHUNK, (d + 1) * DOWN_CHUNK)
        z = _dot_tn(hit, y_ref[:, d].reshape(E * cap, DOWN_CHUNK))
        o_ref[0, :, cs] = x_ref[0, :, cs] + gate[:, cs] * z
    if len(rest) == 2:
        o_ref[0] = _rmsnorm(o_ref[0], rest[0][...])


def _combine(x, sid, y, mod, cap, tbs, row_block0, segment, final_g=None):
    B, E, T = sid.shape
    ntb = T // tbs
    xmap = lambda b, tb: (b, row_block0 + tb, 0)
    in_specs = [pl.BlockSpec((1, tbs, D), xmap),
                pl.BlockSpec((1, E, 1, tbs), lambda b, tb: (b, 0, 0, tb)),
                pl.BlockSpec((E, D // DOWN_CHUNK, cap, DOWN_CHUNK), lambda b, tb: (0, 0, b, 0)),
                pl.BlockSpec((1, 8, D), lambda b, tb: (b * 2 + segment, 0, 0))]
    args = [x, sid.reshape(B, E, 1, T), y, mod]
    if final_g is None:
        out_spec, out_shape, alias = pl.BlockSpec((1, tbs, D), xmap), x.shape, {0: 0}
    else:
        assert row_block0 == 0
        in_specs.append(pl.BlockSpec((1, D), lambda b, tb: (0, 0)))
        args.append(final_g.reshape(1, D))
        out_spec, out_shape, alias = pl.BlockSpec((1, tbs, D), lambda b, tb: (b, tb, 0)), (B, T, D), {}
    return pl.pallas_call(
        functools.partial(_combine_kernel, cap=cap),
        grid=(B, ntb),
        in_specs=in_specs,
        out_specs=out_spec,
        out_shape=jax.ShapeDtypeStruct(out_shape, F32),
        input_output_aliases=alias,
        compiler_params=_cp("parallel", "parallel"),
        name=f"combine_{T}" + ("" if final_g is None else "_final"),
    )(*args)


def _moe(x, h, logits, mod, w_gate, w_up, w_down, layer, need_ctx, final_g):
    sid_l, gate_l = _topk(logits[:, :, :T_L], CAP_L)
    xs_l, gs_l = _gather(h, sid_l, gate_l, CAP_L, 0)
    if need_ctx:
        sid_c, gate_c = _topk(logits[:, :, T_L:], CAP_C)
        xs_c, gs_c = _gather(h, sid_c, gate_c, CAP_C, T_L // T_C)
        y_l, y_c = _ffn([xs_l, xs_c], [gs_l, gs_c], w_gate, w_up, w_down, layer)
    else:
        (y_l,) = _ffn([xs_l], [gs_l], w_gate, w_up, w_down, layer)
    x = _combine(x, sid_l, y_l, mod, CAP_L, 512, 0, 0, final_g)
    if need_ctx:
        x = _combine(x, sid_c, y_c, mod, CAP_C, T_C, T_L // T_C, 1)
    return x


OD_N = RWKV_IN + MLA_U


def _odd_weights(w_in, partner):
    kr = RWKV_IN + MLA_Q_RANK + MLA_KV_RANK
    pad = np.full((128 - MLA_ROPE,), -1)
    index = np.concatenate([np.arange(kr), kr + np.arange(MLA_ROPE), pad, kr + partner, pad])
    return _gather_columns(w_in, index)


def _inproj_odd_kernel(x_ref, mod_ref, g_ref, w_ref, qg_ref, kg_ref, wq_ref, wk_ref, wv_ref, pl_ref,
                       cq_ref, sq_ref, ck_ref, sk_ref, ur_ref, q_ref, k_ref, v_ref):
    mod = mod_ref[0]
    h = _modulate(x_ref[0], g_ref[...], mod[0:1], mod[1:2])
    y = _dot(h.astype(BF16), w_ref[...])
    ur_ref[0] = y[:, :RWKV_IN]
    u = y[:, RWKV_IN:]
    qn = _rmsnorm(u[:, :MLA_Q_RANK], qg_ref[...]).astype(BF16)
    kvn = _rmsnorm(u[:, MLA_Q_RANK:MLA_Q_RANK + MLA_KV_RANK], kg_ref[...]).astype(BF16)
    o = MLA_Q_RANK + MLA_KV_RANK
    kr = u[:, o:o + 128] * ck_ref[...] + u[:, o + 128:o + 256] * sk_ref[...]
    qq = _dot(qn, wq_ref[...])
    q_ref[0] = (qq[:, :MLA_QW] * cq_ref[...] + qq[:, MLA_QW:] * sq_ref[...]).astype(BF16)
    k_ref[0] = (_dot(kvn, wk_ref[...]) + _dot2(kr, pl_ref[...])).astype(BF16)
    v_ref[0] = _dot(kvn, wv_ref[...]).astype(BF16)


def _inproj_odd(x, mod, g, w, q_norm_g, kv_norm_g, mla_weights, mla_tables):
    B = x.shape[0]
    wq, wk, wv, place = mla_weights
    cq, sq, ck, sk = mla_tables
    tok = lambda n: pl.BlockSpec((1, TM, n), lambda b, t: (b, t, 0))
    tab = lambda n: pl.BlockSpec((TM, n), lambda b, t: (t, 0))
    full = lambda a: pl.BlockSpec(a.shape, lambda b, t: (0, 0))
    row = lambda n: pl.BlockSpec((1, n), lambda b, t: (0, 0))
    return pl.pallas_call(
        _inproj_odd_kernel,
        grid=(B, N_TILES),
        in_specs=[tok(D), _mod_spec(), row(D), full(w), row(MLA_Q_RANK), row(MLA_KV_RANK),
                  full(wq), full(wk), full(wv), full(place),
                  tab(MLA_QW), tab(MLA_QW), tab(MLA_SLOT), tab(MLA_SLOT)],
        out_specs=[tok(RWKV_IN), tok(MLA_QW), tok(MLA_QW), tok(MLA_HEADS * MLA_V)],
        out_shape=[jax.ShapeDtypeStruct((B, NT, RWKV_IN), F32),
                   jax.ShapeDtypeStruct((B, NT, MLA_QW), BF16),
                   jax.ShapeDtypeStruct((B, NT, MLA_QW), BF16),
                   jax.ShapeDtypeStruct((B, NT, MLA_HEADS * MLA_V), BF16)],
        compiler_params=_cp("parallel", "parallel"),
        name="inproj_odd",
    )(x, mod, g.reshape(1, D), w, q_norm_g.reshape(1, -1), kv_norm_g.reshape(1, -1), wq, wk, wv, place,
      cq, sq, ck, sk)


def _head_ones():
    h = np.arange(RWKV_WIDTH) // RWKV_HEAD
    return jnp.asarray((h[:, None] == h[None, :]).astype(np.float32)).astype(BF16)


def _headsum(x, bd):
    return _dot2(x, bd)


def _rwkv_prep_kernel(u_ref, up_ref, un_ref, mup_ref, mun_ref, w0_ref, w2_ref, a0_ref, a2_ref, g2_ref,
                      kk_ref, ka_ref, bd_ref,
                      r_ref, v_ref, kkn_ref, g_ref, lw_ref, bq_ref, kd_ref):
    t = pl.program_id(1)
    u = u_ref[0]
    rows = lax.broadcasted_iota(I32, (TM, 1), 0)
    is_ctx = t == N_LAT_TILES
    seg_start = (t == 0) | is_ctx
    seg_end = (t == N_LAT_TILES - 1) | is_ctx
    last_row = jnp.where(is_ctx, T_C - 1, TM - 1)
    prev_row = jnp.where(seg_start, 0.0, up_ref[0, 7:8, :])
    next_row = jnp.where(seg_end, 0.0, un_ref[0, 0:1, :])
    prev = jnp.where(rows == 0, prev_row, pltpu.roll(u, 1, 0))
    nxt = jnp.where(rows == last_row, next_row, pltpu.roll(u, TM - 1, 0))
    us = u + mup_ref[...] * (prev - u) + mun_ref[...] * (nxt - u)
    W = RWKV_WIDTH
    r = us[:, :W]
    k = us[:, W:2 * W]
    v = us[:, 2 * W:3 * W]
    wl = jnp.tanh(us[:, 3 * W:3 * W + 128]).astype(BF16)
    al = us[:, 3 * W + 128:3 * W + 256].astype(BF16)
    gl = jax.nn.sigmoid(us[:, 3 * W + 256:3 * W + 384]).astype(BF16)
    bd = bd_ref[...]
    kk = k * kk_ref[...]
    kk = kk * lax.rsqrt(jnp.maximum(_headsum(kk * kk, bd), 1e-24))
    r_ref[0] = r.astype(BF16)
    v_ref[0] = v.astype(BF16)
    kkn_ref[0] = kk.astype(BF16)
    g_ref[0] = _dot(gl, g2_ref[...]).astype(BF16)
    for d in range(2):
        z = w0_ref[d:d + 1, :] + _dot(wl, w2_ref[d])
        a = jax.nn.sigmoid(a0_ref[d:d + 1, :] + _dot(al, a2_ref[d]))
        lw_ref[d, 0] = jax.nn.sigmoid(z) * (-float(np.exp(-0.5)))
        bq_ref[d, 0] = (kk * a).astype(BF16)
        kd_ref[d, 0] = (k * (1.0 + (a - 1.0) * ka_ref[...])).astype(BF16)


def _rwkv_prep(ur, mu_prev, mu_next, w0, w2, a0, a2, g2, k_k, k_a, bd):
    B = ur.shape[0]
    W = RWKV_WIDTH
    z = jnp.zeros((64, W), F32)
    w2p = jnp.stack([jnp.concatenate([w2[0], z]), jnp.concatenate([z, w2[1]])]).astype(BF16)
    a2p = jnp.stack([jnp.concatenate([a2[0], z]), jnp.concatenate([z, a2[1]])]).astype(BF16)
    tok = lambda n: pl.BlockSpec((1, TM, n), lambda b, t: (b, t, 0))
    tok2 = pl.BlockSpec((2, 1, TM, W), lambda b, t: (0, b, t, 0))
    row = lambda n: pl.BlockSpec((1, n), lambda b, t: (0, 0))
    full = lambda *s: pl.BlockSpec(s, lambda b, t: (0,) * len(s))
    slab = TM // 8
    one = lambda: jax.ShapeDtypeStruct((B, NT, W), BF16)
    per_dir = lambda dt: jax.ShapeDtypeStruct((2, B, NT, W), dt)
    return pl.pallas_call(
        _rwkv_prep_kernel,
        grid=(B, N_TILES),
        in_specs=[tok(RWKV_IN),
                  pl.BlockSpec((1, 8, RWKV_IN), lambda b, t: (b, jnp.maximum(t * slab - 1, 0), 0)),
                  pl.BlockSpec((1, 8, RWKV_IN), lambda b, t: (b, jnp.minimum((t + 1) * slab, NT // 8 - 1), 0)),
                  row(RWKV_IN), row(RWKV_IN), full(2, W), full(2, 128, W), full(2, W), full(2, 128, W),
                  full(128, W), row(W), row(W), full(W, W)],
        out_specs=[tok(W), tok(W), tok(W), tok(W), tok2, tok2, tok2],
        out_shape=[one(), one(), one(), one(), per_dir(F32), per_dir(BF16), per_dir(BF16)],
        compiler_params=_cp("parallel", "parallel"),
        name="rwkv_prep",
    )(ur, ur, ur, mu_prev.reshape(1, RWKV_IN), mu_next.reshape(1, RWKV_IN), w0, w2p, a0, a2p,
      g2.astype(BF16), k_k.reshape(1, W), k_a.reshape(1, W), bd)


def _wkv_kernel(r_ref, v_ref, kk_ref, lw_ref, bq_ref, kd_ref, o_ref, s_ref, *, nb):
    d = pl.program_id(0)
    i = pl.program_id(2)
    C = WKV_CHUNK
    P = RWKV_HEADS // 2
    G = nb * P
    HD = RWKV_HEAD

    @pl.when(i == 0)
    def _():
        s_ref[...] = jnp.zeros_like(s_ref)

    sgn = 1 - 2 * d
    rel = (lax.broadcasted_iota(I32, (C, C), 0) - lax.broadcasted_iota(I32, (C, C), 1)) * sgn
    incl_b = (rel >= 0).astype(BF16)
    row_w = lax.broadcasted_iota(I32, (C, 2 * HD), 0)
    col_w = lax.broadcasted_iota(I32, (C, 2 * HD), 1) & (HD - 1)
    rel_w = (row_w - col_w) * sgn
    incl = rel_w >= 0
    strict = rel_w > 0
    eye = (rel_w == 0).astype(F32)
    lo = lax.broadcasted_iota(I32, (C, 2 * HD), 1) < HD
    same_head = ((lax.broadcasted_iota(I32, (2 * HD, 2 * HD), 0) < HD)
                 == (lax.broadcasted_iota(I32, (2 * HD, 2 * HD), 1) < HD))

    def same_block(k):
        return (row_w >> k) == (col_w >> k)

    def pair(y):
        zero = jnp.zeros_like(y)
        return jnp.concatenate([jnp.where(lo, y, zero), jnp.where(lo, zero, y)], axis=0)

    at, rt, bt, kt, a0, r0, bh, kh, v, p_end = ([] for _ in range(10))
    for b in range(nb):
        lw = lw_ref[0, b]
        lw_hi, lw_lo = _split_bf16(lw)
        ci = _dot(incl_b, lw_hi) + _dot(incl_b, lw_lo)
        ce = ci - lw
        tot = jnp.sum(lw, axis=0, keepdims=True)
        m = 0.5 * tot
        r = r_ref[b].astype(F32)
        kk = kk_ref[b].astype(F32)
        bq = bq_ref[0, b].astype(F32)
        kd = kd_ref[0, b].astype(F32)
        e_ci = jnp.exp(ci - m)
        e_mci = jnp.exp(m - ci)
        e_end = jnp.exp(tot - ci)
        full = ((at, (-kk * jnp.exp(ce - m)).astype(BF16)), (rt, (r * e_ci).astype(BF16)),
                (bt, (bq * e_mci).astype(BF16)), (kt, (kd * e_mci).astype(BF16)),
                (a0, (-kk * jnp.exp(ce)).astype(BF16)), (r0, (r * (e_ci * jnp.exp(m))).astype(BF16)),
                (bh, (bq * e_end).astype(BF16)), (kh, (kd * e_end).astype(BF16)),
                (v, v_ref[b]), (p_end, jnp.exp(tot)))
        for dst, x in full:
            dst.extend(x[:, p * 2 * HD:(p + 1) * 2 * HD] for p in range(P))

    S = [s_ref[g] for g in range(G)]
    s_b = [s.astype(BF16) for s in S]
    left = [jnp.concatenate([at[g], rt[g]], axis=0) for g in range(G)]
    nb_ = [_dot_nt(left[g], pair(bt[g])) for g in range(G)]
    nk_ = [_dot_nt(left[g], pair(kt[g])) for g in range(G)]
    n_ab = [jnp.where(strict, x[:C], 0.0) for x in nb_]
    a_rb = [jnp.where(incl, x[C:], 0.0).astype(BF16) for x in nb_]
    mask_k = jnp.concatenate([strict, incl], axis=0)
    a_k = [jnp.where(mask_k, x, 0.0).astype(BF16) for x in nk_]
    from_state = [_dot_nt(jnp.concatenate([a0[g], r0[g]], axis=0), s_b[g]) for g in range(G)]
    from_v = [_dot(a_k[g], pair(v[g])) for g in range(G)]
    w_rhs = [from_state[g][:C] + from_v[g][:C] for g in range(G)]
    inv = [eye + jnp.where(same_block(1), n, 0.0) for n in n_ab]
    for k in range(1, 6):
        off_mask = same_block(k + 1) & jnp.logical_not(same_block(k))
        inv_b = [x.astype(BF16) for x in inv]
        e = [_dot(jnp.where(off_mask, n, 0.0).astype(BF16), pair(x)) for n, x in zip(n_ab, inv_b)]
        inv = [x + _dot(xb, pair(y.astype(BF16))) for x, xb, y in zip(inv, inv_b, e)]
    U = [_dot(inv[g].astype(BF16), pair(w_rhs[g].astype(BF16))).astype(BF16) for g in range(G)]
    for g in range(G):
        o = from_state[g][C:] + _dot(a_rb[g], pair(U[g])) + from_v[g][C:]
        o_ref[0, g // P, :, (g % P) * 2 * HD:(g % P + 1) * 2 * HD] = o
    for g in range(G):
        upd = _dot_tn(jnp.concatenate([U[g], v[g]], axis=0), jnp.concatenate([bh[g], kh[g]], axis=0))
        s_ref[g] = S[g] * p_end[g] + jnp.where(same_head, upd, 0.0)


WKV_BATCH = 4


def _wkv(r, v, kk, lw, bq, kd, nb=WKV_BATCH):
    B = r.shape[0]
    C = WKV_CHUNK
    n_lat = T_L // C

    def chunk(d, i):
        fwd = jnp.where(i < N_CTX_CHUNKS, n_lat + i, i - N_CTX_CHUNKS)
        return jnp.where(d == 0, fwd, N_CHUNKS - 1 - i)

    shared = pl.BlockSpec((nb, C, RWKV_WIDTH), lambda d, b, i: (b, chunk(d, i), 0))
    perdir = pl.BlockSpec((1, nb, C, RWKV_WIDTH), lambda d, b, i: (d, b, chunk(d, i), 0))
    return pl.pallas_call(
        functools.partial(_wkv_kernel, nb=nb),
        grid=(2, B // nb, N_CHUNKS),
        in_specs=[shared, shared, shared, perdir, perdir, perdir],
        out_specs=perdir,
        out_shape=jax.ShapeDtypeStruct((2, B, NT, RWKV_WIDTH), F32),
        scratch_shapes=[pltpu.VMEM((nb * RWKV_HEADS // 2, 2 * RWKV_HEAD, 2 * RWKV_HEAD), F32)],
        compiler_params=_cp("parallel", "parallel", "arbitrary"),
        name="wkv7",
    )(r, v, kk, lw, bq, kd)


def _rwkv_finish_kernel(o_ref, r_ref, v_ref, g_ref, kd_ref, rk_ref, lg_ref, lb_ref, bd_ref, y_ref):
    bd = bd_ref[...]
    o = o_ref[0, 0] + o_ref[1, 0]
    mu = _headsum(o, bd) * (1.0 / RWKV_HEAD)
    xc = o - mu
    var = _headsum(xc * xc, bd) * (1.0 / RWKV_HEAD)
    on = xc * lax.rsqrt(var + LNX_EPS) * lg_ref[...] + lb_ref[...]
    k_sum = kd_ref[0, 0].astype(F32) + kd_ref[1, 0].astype(F32)
    bonus = _headsum(r_ref[0].astype(F32) * k_sum * rk_ref[...], bd) * v_ref[0].astype(F32)
    y_ref[0] = ((on + bonus) * g_ref[0].astype(F32)).astype(y_ref.dtype)


def _rwkv_finish(o, r, v, g, kd, r_k, lnx_g, lnx_b, bd, need_ctx):
    B = r.shape[0]
    W = RWKV_WIDTH
    tok = pl.BlockSpec((1, TM, W), lambda b, t: (b, t, 0))
    tok2 = pl.BlockSpec((2, 1, TM, W), lambda b, t: (0, b, t, 0))
    row = pl.BlockSpec((1, W), lambda b, t: (0, 0))
    return pl.pallas_call(
        _rwkv_finish_kernel,
        grid=(B, N_TILES if need_ctx else N_LAT_TILES),
        in_specs=[tok2, tok, tok, tok, tok2, row, row, row, pl.BlockSpec((W, W), lambda b, t: (0, 0))],
        out_specs=tok,
        out_shape=jax.ShapeDtypeStruct((B, NT, W), BF16),
        compiler_params=_cp("parallel", "parallel"),
        name="rwkv_finish",
    )(o, r, v, g, kd, r_k.reshape(1, W), lnx_g.reshape(1, W), lnx_b.reshape(1, W), bd)


MLA_QW = MLA_HEADS * MLA_SLOT


def _mla_weights(w_uq, w_ukv, partner):
    dq = MLA_NOPE + MLA_ROPE
    dkv = MLA_NOPE + MLA_V
    iq = np.full((2 * MLA_QW,), -1)
    ik = np.full((MLA_QW,), -1)
    iv = np.zeros((MLA_HEADS * MLA_V,), np.int64)
    place = np.zeros((128, MLA_QW), np.float32)
    for h in range(MLA_HEADS):
        iq[h * MLA_SLOT:h * MLA_SLOT + dq] = h * dq + np.arange(dq)
        iq[MLA_QW + h * MLA_SLOT + MLA_NOPE:MLA_QW + h * MLA_SLOT + dq] = h * dq + MLA_NOPE + partner
        ik[h * MLA_SLOT:h * MLA_SLOT + MLA_NOPE] = h * dkv + np.arange(MLA_NOPE)
        iv[h * MLA_V:(h + 1) * MLA_V] = h * dkv + MLA_NOPE + np.arange(MLA_V)
        place[np.arange(MLA_ROPE), h * MLA_SLOT + MLA_NOPE + np.arange(MLA_ROPE)] = 1.0
    return (_gather_columns(w_uq, iq), _gather_columns(w_ukv, ik), _gather_columns(w_ukv, iv),
            jnp.asarray(place).astype(BF16))


MLA_TQ = 512


def _mla_attn_kernel(q_ref, k_ref, v_ref, o_ref):
    t = pl.program_id(2)
    is_ctx = t == T_L // MLA_TQ
    lo_lanes = lax.broadcasted_iota(I32, (MLA_TQ, 128), 1) < MLA_V

    def attend(keys):
        n_keys = keys.stop - keys.start
        lo_rows = lax.broadcasted_iota(I32, (n_keys, 128), 1) < MLA_V
        vall = v_ref[0, keys, :]
        one = jnp.ones((n_keys, 128), vall.dtype)
        vh = [jnp.where(lo_rows, vall, one), jnp.where(lo_rows, one, vall)]
        s = [_dot_nt(q_ref[0, :, hh * MLA_SLOT:(hh + 1) * MLA_SLOT], k_ref[0, keys, hh * MLA_SLOT:(hh + 1) * MLA_SLOT])
             for hh in range(2)]
        e = [jnp.exp(x - jnp.max(x, axis=-1, keepdims=True)).astype(BF16) for x in s]
        r = [_dot(e[hh], vh[hh]) for hh in range(2)]
        num = jnp.where(lo_lanes, r[0], r[1])
        den = pltpu.roll(jnp.where(lo_lanes, r[1], r[0]), MLA_V, 1)
        o_ref[0] = (num * (1.0 / den)).astype(o_ref.dtype)

    @pl.when(jnp.logical_not(is_ctx))
    def _():
        attend(slice(0, NT))

    @pl.when(is_ctx)
    def _():
        attend(slice(T_L, NT))


def _mla_attn(q, k, v, need_ctx):
    B = q.shape[0]
    n_tiles = pl.cdiv(NT, MLA_TQ) if need_ctx else T_L // MLA_TQ
    return pl.pallas_call(
        _mla_attn_kernel,
        grid=(B, MLA_HEADS // 2, n_tiles),
        in_specs=[pl.BlockSpec((1, MLA_TQ, 2 * MLA_SLOT), lambda b, p, t: (b, t, p)),
                  pl.BlockSpec((1, NT, 2 * MLA_SLOT), lambda b, p, t: (b, 0, p)),
                  pl.BlockSpec((1, NT, 2 * MLA_V), lambda b, p, t: (b, 0, p))],
        out_specs=pl.BlockSpec((1, MLA_TQ, 2 * MLA_V), lambda b, p, t: (b, t, p)),
        out_shape=jax.ShapeDtypeStruct((B, NT, MLA_HEADS * MLA_V), BF16),
        compiler_params=_cp("parallel", "parallel", "parallel"),
        name="mla_attn",
    )(q, k, v)


def kernel(x, c, ctx, c_ctx, ada_w, ada_b, norm_mix_g, norm_ffn_g, router_w, exp_w_gate, exp_w_up, exp_w_down, ev_w_in, ev_w_out, ev_sink, ev_pool_w, ev_pool_scale, od_w_in, od_w_out, od_mu_prev, od_mu_next, od_w0, od_w2, od_a0, od_a2, od_g2, od_k_k, od_k_a, od_r_k, od_lnx_g, od_lnx_b, od_q_norm_g, od_w_uq, od_kv_norm_g, od_w_ukv, final_norm_g):
    depth = ada_w.shape[0]
    assert x.shape[1:] == (T_L, D) and ctx.shape[1:] == (T_C, D)
    mods = _modvec(c, c_ctx, ada_w, ada_b)
    ev_tables, ev_partner = _even_tables()
    mla_tables, mla_partner = _mla_tables()
    bd = _head_ones()
    xs = jnp.concatenate([x, ctx], axis=1)
    for l in range(depth):
        need_ctx = l < depth - 1
        i = l // 2
        mod = mods[l]
        if l % 2 == 0:
            q, kd, vd, pu = _inproj_even(xs, mod, norm_mix_g[l], _even_weights(ev_w_in[i]), ev_tables)
            mix_a = _attn_even(q, kd, vd, ev_sink[i])
            mix_b = _pool(pu, ev_pool_w[i], ev_pool_scale[i])
            w_out = ev_w_out[i]
        else:
            ur, mq, mk, mv = _inproj_odd(xs, mod, norm_mix_g[l], _odd_weights(od_w_in[i], mla_partner),
                                         od_q_norm_g[i], od_kv_norm_g[i],
                                         _mla_weights(od_w_uq[i], od_w_ukv[i], mla_partner), mla_tables)
            r, v, kk, g, lw, bq, kdir = _rwkv_prep(ur, od_mu_prev[i], od_mu_next[i], od_w0[i], od_w2[i],
                                                   od_a0[i], od_a2[i], od_g2[i], od_k_k[i], od_k_a[i], bd)
            o = _wkv(r, v, kk, lw, bq, kdir)
            mix_a = _rwkv_finish(o, r, v, g, kdir, od_r_k[i], od_lnx_g[i], od_lnx_b[i], bd, need_ctx)
            mix_b = _mla_attn(mq, mk, mv, need_ctx)
            w_out = od_w_out[i]
        xs, h, logits = _outproj(mix_a, mix_b, w_out, xs, mod, norm_ffn_g[l], router_w[l], need_ctx)
        xs = _moe(xs, h, logits, mod, exp_w_gate, exp_w_up, exp_w_down, l, need_ctx,
                  None if need_ctx else final_norm_g)
    return xs
```
